```python
import math
import jax, jax.numpy as jnp
from jax import lax
import numpy as np

D_MODEL = 1024
BATCH = 4
SEQ = 4096
DEPTH = 2

M_HEADS = 4
M_HEAD_DIM = 256
M_WIDTH = M_HEADS * M_HEAD_DIM
M_CHUNK = 128
CONV_WIDTH = 5
ATT_SLOTS = 8
ATT_HEAD_DIM = 64
ATT_PATTERNS = ((128, 1), (512, 4), (2048, 16))
N_PATTERNS = 3
ATT_HEADS = ATT_SLOTS * N_PATTERNS
ATT_WIDTH = ATT_HEADS * ATT_HEAD_DIM
ATT_OUT = ATT_SLOTS * ATT_HEAD_DIM
D_FF = -(-8 * D_MODEL // (3 * 256)) * 256
DN_ALPHA = (2 * DEPTH) ** 0.25
DN_BETA = (8 * DEPTH) ** -0.25
LN_EPS = 1e-5
NEG = -1e30
IN_SECTIONS = (2 * M_WIDTH, M_WIDTH, M_WIDTH, 4 * M_HEADS, 3 * ATT_WIDTH, 2 * D_MODEL)
N_IN = sum(IN_SECTIONS)

kernel_name = "hybrid_mlstm_dilated_alibi_deepnorm_adaln"


def layer_norm(x, g=None, b=None):
    xf = x.astype(jnp.float32)
    mu = xf.mean(-1, keepdims=True)
    var = jnp.square(xf - mu).mean(-1, keepdims=True)
    y = (xf - mu) * lax.rsqrt(var + LN_EPS)
    if g is not None:
        y = y * g.astype(jnp.float32) + b.astype(jnp.float32)
    return y.astype(x.dtype)


def alibi_slopes():
    h = np.arange(1, ATT_HEADS + 1, dtype=np.float32)
    s = np.exp2(-8.0 * h / ATT_HEADS).astype(np.float32)
    return jnp.asarray(s.reshape(N_PATTERNS, ATT_SLOTS))


def centred_dwconv(x, w, b):
    k, ch = w.shape
    y = lax.conv_general_dilated(x, w[:, None, :], window_strides=(1,),
                                 padding=((k // 2, k // 2),),
                                 dimension_numbers=('NWC', 'WIO', 'NWC'),
                                 feature_group_count=ch)
    return y + b


def mlstm_scan(q, k, v, i_pre, f_pre):
    B, H, S, dk = q.shape
    dv = v.shape[-1]
    nc = S // M_CHUNK

    def chunks(a):
        return jnp.moveaxis(a.reshape(B, H, nc, M_CHUNK, *a.shape[3:]), 2, 0)

    xs = (chunks(q), chunks(k), chunks(v), chunks(i_pre), chunks(jax.nn.log_sigmoid(f_pre)))
    tri = jnp.tril(jnp.ones((M_CHUNK, M_CHUNK), dtype=bool))

    def step(carry, inp):
        C, n, m = carry
        qc, kc, vc, ic, fc = inp
        b = jnp.cumsum(fc, axis=-1)
        log_d = jnp.where(tri, b[..., :, None] - b[..., None, :] + ic[..., None, :], NEG)
        m_inter = b + m[..., None]
        m_t = jnp.maximum(log_d.max(-1), m_inter)
        w_inter = jnp.exp(m_inter - m_t)
        s = jnp.einsum('bhtd,bhsd->bhts', qc, kc) * jnp.exp(log_d - m_t[..., None])
        num = jnp.einsum('bhts,bhsv->bhtv', s, vc) + w_inter[..., None] * jnp.einsum('bhtd,bhdv->bhtv', qc, C)
        den = s.sum(-1) + w_inter * jnp.einsum('bhtd,bhd->bht', qc, n)
        h = num / jnp.maximum(jnp.abs(den), jnp.exp(-m_t))[..., None]
        b_last = b[..., -1]
        log_w = b_last[..., None] - b + ic
        m_new = jnp.maximum(b_last + m, log_w.max(-1))
        w_state = jnp.exp(log_w - m_new[..., None])
        decay = jnp.exp(b_last + m - m_new)
        C = decay[..., None, None] * C + jnp.einsum('bhs,bhsd,bhsv->bhdv', w_state, kc, vc)
        n = decay[..., None] * n + jnp.einsum('bhs,bhsd->bhd', w_state, kc)
        return (C, n, m_new), h

    init = (jnp.zeros((B, H, dk, dv), jnp.float32), jnp.zeros((B, H, dk), jnp.float32),
            jnp.full((B, H), NEG, jnp.float32))
    _, h = lax.scan(step, init, xs)
    return jnp.moveaxis(h, 0, 2).reshape(B, H, S, dv)


def mlstm_branch(qk, v, o_pre, gate_pre, b_gates, conv_w, conv_b, gn_w):
    B, S, _ = v.shape
    f32 = jnp.float32
    qk = jax.nn.silu(centred_dwconv(qk, conv_w.astype(qk.dtype), conv_b.astype(qk.dtype)))

    def heads(a):
        return a.astype(f32).reshape(B, S, M_HEADS, M_HEAD_DIM).transpose(0, 2, 1, 3)

    q, k = jnp.split(qk, 2, axis=-1)
    q = heads(q) * M_HEAD_DIM ** -0.5
    k = heads(k)
    vh = heads(v)
    g = (gate_pre.astype(f32) + b_gates.astype(f32)).reshape(B, S, 4, M_HEADS).transpose(2, 0, 3, 1)
    i_f, f_f, i_b, f_b = g[0], g[1], g[2], g[3]
    flip = lambda a: jnp.flip(a, axis=2)
    h = mlstm_scan(q, k, vh, i_f, f_f) + flip(mlstm_scan(flip(q), flip(k), flip(vh), flip(i_b), flip(f_b)))
    mu = h.mean(-1, keepdims=True)
    var = jnp.square(h - mu).mean(-1, keepdims=True)
    hn = ((h - mu) * lax.rsqrt(var + LN_EPS)).transpose(0, 2, 1, 3).reshape(B, S, M_WIDTH)
    return (jax.nn.sigmoid(o_pre.astype(f32)) * hn * gn_w.astype(f32)).astype(v.dtype)


def banded_attention(q, k, v, slopes, dil, half):
    B, H, G, n, dh = q.shape
    nb = -(-n // half)
    n_pad = nb * half
    pad0 = ((0, 0),) * 3
    qb = jnp.pad(q, pad0 + ((0, n_pad - n), (0, 0))).reshape(B, H, G, nb, half, dh)

    def kblocks(a):
        ap = jnp.pad(a, pad0 + ((half, n_pad - n + half), (0, 0))).reshape(B, H, G, nb + 2, half, dh)
        return jnp.concatenate([ap[:, :, :, :-2], ap[:, :, :, 1:-1], ap[:, :, :, 2:]], axis=-2)

    kb, vb = kblocks(k), kblocks(v)
    qi = jnp.arange(n_pad).reshape(nb, half)
    kj = (jnp.arange(nb) * half - half)[:, None] + jnp.arange(3 * half)[None, :]
    dist = jnp.abs(qi[:, :, None] - kj[:, None, :])
    valid = (dist <= half) & (kj[:, None, :] >= 0) & (kj[:, None, :] < n)
    bias = -slopes[:, None, None, None] * (dil * dist).astype(jnp.float32)
    s = jnp.einsum('bhgnqd,bhgnkd->bhgnqk', qb, kb) * ATT_HEAD_DIM ** -0.5 + bias[None, :, None]
    s = jnp.where(valid, s, NEG)
    lse = jax.nn.logsumexp(s, axis=-1)
    o = jnp.einsum('bhgnqk,bhgnkd->bhgnqd', jnp.exp(s - lse[..., None]), vb)
    return o.reshape(B, H, G, n_pad, dh)[:, :, :, :n], lse.reshape(B, H, G, n_pad)[:, :, :, :n]


def dilated_attention(qkv, slopes):
    B, S, _ = qkv.shape
    qkv = qkv.astype(jnp.float32).reshape(B, S, 3, N_PATTERNS, ATT_SLOTS, ATT_HEAD_DIM)
    outs, lses = [], []
    for g, (win, dil) in enumerate(ATT_PATTERNS):
        n = S // dil

        def to_res(a):
            return a.reshape(B, n, dil, ATT_SLOTS, ATT_HEAD_DIM).transpose(0, 3, 2, 1, 4)

        o, lse = banded_attention(to_res(qkv[:, :, 0, g]), to_res(qkv[:, :, 1, g]),
                                  to_res(qkv[:, :, 2, g]), slopes[g], dil, win // (2 * dil))
        outs.append(o.transpose(0, 3, 2, 1, 4).reshape(B, S, ATT_SLOTS, ATT_HEAD_DIM))
        lses.append(lse.transpose(0, 3, 2, 1).reshape(B, S, ATT_SLOTS))
    w = jax.nn.softmax(jnp.stack(lses, axis=0), axis=0)
    o = (w[..., None] * jnp.stack(outs, axis=0)).sum(0)
    return o.reshape(B, S, ATT_OUT)


def mixer(u, w_in, b_gates, conv_w, conv_b, gn_w, w_a, w_b, w_out, slopes):
    z = u @ w_in
    idx = np.cumsum(IN_SECTIONS)[:-1].tolist()
    qk_m, v_m, o_m, gate_m, qkv_a, gates_br = jnp.split(z, idx, axis=-1)
    y_a = mlstm_branch(qk_m, v_m, o_m, gate_m, b_gates, conv_w, conv_b, gn_w) @ w_a
    y_b = dilated_attention(qkv_a, slopes).astype(u.dtype) @ w_b
    g_a, g_b = jnp.split(gates_br, 2, axis=-1)
    return (jax.nn.sigmoid(g_a) * y_a + jax.nn.sigmoid(g_b) * y_b) @ w_out


def swiglu(u, w1, w3, w2):
    return (jax.nn.silu(u @ w1) * (u @ w3)) @ w2


def setup_inputs(seed: int = 0) -> dict:
    key = jax.random.key(seed)
    ks = jax.random.split(key, 24)

    def nrm(k, shape, scale):
        return jax.random.normal(k, shape, jnp.float32) * scale

    f_bias = jnp.linspace(3.0, 6.0, M_HEADS, dtype=jnp.float32)
    b_gates = jnp.concatenate([
        nrm(ks[3], (DEPTH, M_HEADS), 0.1),
        f_bias + nrm(ks[4], (DEPTH, M_HEADS), 0.1),
        nrm(ks[5], (DEPTH, M_HEADS), 0.1),
        f_bias + nrm(ks[6], (DEPTH, M_HEADS), 0.1)], axis=-1)
    return {
        "x": nrm(ks[0], (BATCH, SEQ, D_MODEL), 1.0),
        "c": nrm(ks[1], (BATCH, D_MODEL), 1.0),
        "w_in": nrm(ks[2], (DEPTH, D_MODEL, N_IN), D_MODEL ** -0.5),
        "b_gates": b_gates,
        "conv_w": nrm(ks[7], (DEPTH, CONV_WIDTH, 2 * M_WIDTH), CONV_WIDTH ** -0.5),
        "conv_b": nrm(ks[8], (DEPTH, 2 * M_WIDTH), 0.02),
        "gn_w": 1.0 + nrm(ks[9], (DEPTH, M_WIDTH), 0.02),
        "w_a": nrm(ks[10], (DEPTH, M_WIDTH, D_MODEL), M_WIDTH ** -0.5),
        "w_b": nrm(ks[11], (DEPTH, ATT_OUT, D_MODEL), ATT_OUT ** -0.5),
        "w_out": nrm(ks[12], (DEPTH, D_MODEL, D_MODEL), DN_BETA * D_MODEL ** -0.5),
        "w_ada": nrm(ks[13], (DEPTH, D_MODEL, 6 * D_MODEL), 0.5 * D_MODEL ** -0.5),
        "b_ada": nrm(ks[14], (DEPTH, 6 * D_MODEL), 0.02),
        "ln1_g": 1.0 + nrm(ks[15], (DEPTH, D_MODEL), 0.02),
        "ln1_b": nrm(ks[16], (DEPTH, D_MODEL), 0.02),
        "w1": nrm(ks[17], (DEPTH, D_MODEL, D_FF), D_MODEL ** -0.5),
        "w3": nrm(ks[18], (DEPTH, D_MODEL, D_FF), D_MODEL ** -0.5),
        "w2": nrm(ks[19], (DEPTH, D_FF, D_MODEL), DN_BETA * D_FF ** -0.5),
        "ln2_g": 1.0 + nrm(ks[20], (DEPTH, D_MODEL), 0.02),
        "ln2_b": nrm(ks[21], (DEPTH, D_MODEL), 0.02),
    }


def reference(x, c, w_in, b_gates, conv_w, conv_b, gn_w, w_a, w_b, w_out, w_ada, b_ada,
              ln1_g, ln1_b, w1, w3, w2, ln2_g, ln2_b):
    slopes = alibi_slopes()
    c_act = jax.nn.silu(c)
    h = layer_norm(x)
    for l in range(DEPTH):
        ada = (c_act @ w_ada[l] + b_ada[l])[:, None, :]
        sh1, sc1, g1, sh2, sc2, g2 = jnp.split(ada, 6, axis=-1)
        u = h * (1.0 + sc1) + sh1
        y = mixer(u, w_in[l], b_gates[l], conv_w[l], conv_b[l], gn_w[l], w_a[l], w_b[l], w_out[l], slopes)
        h = layer_norm(DN_ALPHA * h + g1 * y, ln1_g[l], ln1_b[l])
        u = h * (1.0 + sc2) + sh2
        y = swiglu(u, w1[l], w3[l], w2[l])
        h = layer_norm(DN_ALPHA * h + g2 * y, ln2_g[l], ln2_b[l])
    return h
```

```python
import functools

import jax
import jax.numpy as jnp
import numpy as np
from jax import lax
from jax.experimental import pallas as pl
from jax.experimental.pallas import tpu as pltpu

F32 = jnp.float32
BF16 = jnp.bfloat16

D_MODEL = 1024
DEPTH = 2
M_HEADS = 4
M_HEAD_DIM = 256
M_WIDTH = M_HEADS * M_HEAD_DIM
M_CHUNK = 128
CONV_WIDTH = 5
ATT_SLOTS = 8
ATT_HEAD_DIM = 64
ATT_PATTERNS = ((128, 1), (512, 4), (2048, 16))
N_PATTERNS = 3
ATT_HEADS = ATT_SLOTS * N_PATTERNS
ATT_WIDTH = ATT_HEADS * ATT_HEAD_DIM
ATT_OUT = ATT_SLOTS * ATT_HEAD_DIM
ATT_HALF = 64
D_FF = -(-8 * D_MODEL // (3 * 256)) * 256
DN_ALPHA = (2 * DEPTH) ** 0.25
LN_EPS = 1e-5
NEG = -1e30
IN_SECTIONS = (2 * M_WIDTH, M_WIDTH, M_WIDTH, 4 * M_HEADS, 3 * ATT_WIDTH, 2 * D_MODEL)

VMEM_LIMIT_BYTES = 56 * 1024 * 1024
LANES = 128
SUBLANES = 8

Z32_WIDTH = 2 * M_WIDTH + M_WIDTH + 2 * D_MODEL
ZB_WIDTH = M_WIDTH + 3 * ATT_WIDTH
FF_CHUNK = 256


def _params(*sem):
    return pltpu.CompilerParams(dimension_semantics=sem, vmem_limit_bytes=VMEM_LIMIT_BYTES)


def _layer_norm_rows(x):
    mu = jnp.mean(x, axis=-1, keepdims=True)
    xc = x - mu
    var = jnp.mean(xc * xc, axis=-1, keepdims=True)
    return xc * lax.rsqrt(var + LN_EPS)


def _ada_kernel(c_ref, w_ref, b_ref, o_ref):
    c = c_ref[...]
    ca = c * jax.nn.sigmoid(c)
    o_ref[...] = jnp.dot(ca, w_ref[...], precision=lax.Precision.HIGHEST,
                         preferred_element_type=F32) + b_ref[...]


def _ada_call(c_pad, w_ada, b_ada):
    depth = w_ada.shape[0]
    return pl.pallas_call(
        _ada_kernel,
        grid=(depth, 6),
        in_specs=[pl.BlockSpec((SUBLANES, D_MODEL), lambda l, k: (0, 0)),
                  pl.BlockSpec((None, D_MODEL, D_MODEL), lambda l, k: (l, 0, k)),
                  pl.BlockSpec((None, 1, D_MODEL), lambda l, k: (l, 0, k))],
        out_specs=pl.BlockSpec((None, None, SUBLANES, D_MODEL), lambda l, k: (l, k, 0, 0)),
        out_shape=jax.ShapeDtypeStruct((depth, 6, SUBLANES, D_MODEL), F32),
        compiler_params=_params("arbitrary", "arbitrary"),
        name="ada",
    )(c_pad, w_ada, b_ada.reshape(depth, 1, 6 * D_MODEL))


def _ln0_kernel(x_ref, ada_ref, h_ref, u_ref):
    y = _layer_norm_rows(x_ref[...])
    h_ref[...] = y
    u_ref[...] = (y * (1.0 + ada_ref[1:2, :]) + ada_ref[0:1, :]).astype(BF16)


def _ln0_call(x, ada0, ts=512):
    B, S, D = x.shape
    row = pl.BlockSpec((None, ts, D), lambda b, i: (b, i, 0))
    return pl.pallas_call(
        _ln0_kernel,
        grid=(B, S // ts),
        in_specs=[row, pl.BlockSpec((None, SUBLANES, D), lambda b, i: (b, 0, 0))],
        out_specs=[row, row],
        out_shape=[jax.ShapeDtypeStruct((B, S, D), F32), jax.ShapeDtypeStruct((B, S, D), BF16)],
        compiler_params=_params("arbitrary", "arbitrary"),
        name="ln0",
    )(x, ada0)


def _proj_kernel(u_ref, w_ref, o_ref):
    o_ref[...] = jnp.dot(u_ref[...], w_ref[...], preferred_element_type=F32).astype(o_ref.dtype)


def _proj_call(u2d, w, out_dtype, tm, tn, name):
    T, D = u2d.shape
    N = w.shape[1]
    return pl.pallas_call(
        _proj_kernel,
        grid=(T // tm, N // tn),
        in_specs=[pl.BlockSpec((tm, D), lambda i, j: (i, 0)),
                  pl.BlockSpec((D, tn), lambda i, j: (0, j))],
        out_specs=pl.BlockSpec((tm, tn), lambda i, j: (i, j)),
        out_shape=jax.ShapeDtypeStruct((T, N), out_dtype),
        compiler_params=_params("arbitrary", "arbitrary"),
        name=name,
    )(u2d, w)


def _gate_kernel(u_ref, w_ref, b_ref, o_ref):
    g = jnp.dot(u_ref[...], w_ref[...], preferred_element_type=F32) + b_ref[...]
    lf = jnp.minimum(g, 0.0) - jnp.log1p(jnp.exp(-jnp.abs(g)))
    row = lax.broadcasted_iota(jnp.int32, (M_CHUNK, LANES), 0)
    lane = lax.broadcasted_iota(jnp.int32, (M_CHUNK, LANES), 1)
    H = M_HEADS
    for c in range(u_ref.shape[0] // M_CHUNK):
        sl = slice(c * M_CHUNK, (c + 1) * M_CHUNK)
        gc, pre, suf = g[sl], lf[sl], lf[sl]
        d = 1
        while d < M_CHUNK:
            pre = pre + jnp.where(row >= d, pltpu.roll(pre, d, 0), 0.0)
            suf = suf + jnp.where(row < M_CHUNK - d, pltpu.roll(suf, M_CHUNK - d, 0), 0.0)
            d *= 2
        a_f = gc - pltpu.roll(pre, LANES - H, 1)
        a_b = gc - pltpu.roll(suf, LANES - H, 1)
        o_ref[sl, :] = jnp.where(lane < H, a_f, jnp.where(lane < 2 * H, pre,
                                 jnp.where(lane < 3 * H, a_b, suf)))


def _gate_call(u2d, wg, bg, tm=512):
    T, D = u2d.shape
    return pl.pallas_call(
        _gate_kernel,
        grid=(T // tm,),
        in_specs=[pl.BlockSpec((tm, D), lambda i: (i, 0)),
                  pl.BlockSpec((D, LANES), lambda i: (0, 0)),
                  pl.BlockSpec((1, LANES), lambda i: (0, 0))],
        out_specs=pl.BlockSpec((tm, LANES), lambda i: (i, 0)),
        out_shape=jax.ShapeDtypeStruct((T, LANES), F32),
        compiler_params=_params("arbitrary"),
        name="gates",
    )(u2d, wg, bg)


def _conv_kernel(x_ref, p_ref, n_ref, w_ref, b_ref, *out_refs, scale, transpose_out):
    i = pl.program_id(1)
    ts = x_ref.shape[0]
    x = x_ref[...]
    prev = jnp.where(i > 0, p_ref[...], 0.0)
    nxt = jnp.where(i < pl.num_programs(1) - 1, n_ref[...], 0.0)
    row8 = lax.broadcasted_iota(jnp.int32, (SUBLANES, x.shape[1]), 0)

    def shifted(k):
        if k == 0:
            return x
        r = pltpu.roll(x, (-k) % ts, 0)
        if k < 0:
            edge = jnp.where(row8 < -k, pltpu.roll(prev, -k, 0), r[:SUBLANES])
            return jnp.concatenate([edge, r[SUBLANES:]], axis=0)
        edge = jnp.where(row8 >= SUBLANES - k, pltpu.roll(nxt, SUBLANES - k, 0), r[ts - SUBLANES:])
        return jnp.concatenate([r[:ts - SUBLANES], edge], axis=0)

    y = b_ref[...] + w_ref[0:1, :] * shifted(-2)
    for j in range(1, CONV_WIDTH):
        y = y + w_ref[j:j + 1, :] * shifted(j - CONV_WIDTH // 2)
    y = y * jax.nn.sigmoid(y)
    if scale != 1.0:
        y = y * scale
    out_refs[0][...] = y.astype(BF16)
    if transpose_out:
        out_refs[1][...] = y.T.astype(BF16)


def _conv_call(z32, conv_w8, conv_b, col0, scale, transpose_out, ts=512, tc=512):
    B, S, _ = z32.shape
    c0 = col0 // tc
    nsb = S // SUBLANES
    rb = ts // SUBLANES
    in_specs = [
        pl.BlockSpec((None, ts, tc), lambda b, i, c: (b, i, c0 + c)),
        pl.BlockSpec((None, SUBLANES, tc), lambda b, i, c: (b, jnp.maximum(i * rb - 1, 0), c0 + c)),
        pl.BlockSpec((None, SUBLANES, tc), lambda b, i, c: (b, jnp.minimum((i + 1) * rb, nsb - 1), c0 + c)),
        pl.BlockSpec((SUBLANES, tc), lambda b, i, c: (0, c0 + c)),
        pl.BlockSpec((1, tc), lambda b, i, c: (0, c0 + c)),
    ]
    out_specs = [pl.BlockSpec((None, ts, tc), lambda b, i, c: (b, i, c))]
    out_shape = [jax.ShapeDtypeStruct((B, S, M_WIDTH), BF16)]
    if transpose_out:
        out_specs.append(pl.BlockSpec((None, tc, ts), lambda b, i, c: (b, c, i)))
        out_shape.append(jax.ShapeDtypeStruct((B, M_WIDTH, S), BF16))
    return pl.pallas_call(
        functools.partial(_conv_kernel, scale=scale, transpose_out=transpose_out),
        grid=(B, S // ts, M_WIDTH // tc),
        in_specs=in_specs, out_specs=out_specs, out_shape=out_shape,
        compiler_params=_params("arbitrary", "arbitrary", "arbitrary"),
        name="conv_k" if transpose_out else "conv_q",
    )(z32, z32, z32, conv_w8, conv_b)


def _mlstm_kernel(q_ref, k_ref, kt_ref, v_ref, gc_ref, gr_ref, o_ref, gnw_ref, y_ref,
                  hs_ref, cf_ref, cb_ref):
    L = M_CHUNK
    S = q_ref.shape[0]
    nc = S // L
    row = lax.broadcasted_iota(jnp.int32, (L, L), 0)
    col = lax.broadcasted_iota(jnp.int32, (L, L), 1)
    tri = {True: row >= col, False: row <= col}
    cf_ref[...] = jnp.zeros_like(cf_ref)
    cb_ref[...] = jnp.zeros_like(cb_ref)

    def chunk(c, fwd, c_ref, n, m):
        r0 = pl.multiple_of(c * L, L)
        qc = q_ref[pl.ds(r0, L), :]
        kc = k_ref[pl.ds(r0, L), :]
        vc = v_ref[pl.ds(r0, L), :]
        ktc = kt_ref[:, pl.ds(r0, L)]
        g = gc_ref[pl.ds(r0, L), :]
        j = 0 if fwd else 2
        a_col, b_col = g[:, j:j + 1], g[:, j + 1:j + 2]
        a_row = gr_ref[j:j + 1, pl.ds(r0, L)]
        b_last = b_col[L - 1:L, :] if fwd else b_col[0:1, :]

        log_d = jnp.where(tri[fwd], b_col + a_row, NEG)
        m_inter = b_col + m
        m_t = jnp.maximum(jnp.max(log_d, axis=-1, keepdims=True), m_inter)
        w_inter = jnp.exp(m_inter - m_t)
        s = jnp.dot(qc, ktc, preferred_element_type=F32) * jnp.exp(log_d - m_t)
        num = (jnp.dot(s.astype(BF16), vc, preferred_element_type=F32)
               + w_inter * jnp.dot(qc, c_ref[...].astype(BF16), preferred_element_type=F32))
        qn = jnp.sum(qc.astype(F32) * n, axis=-1, keepdims=True)
        den = jnp.sum(s, axis=-1, keepdims=True) + w_inter * qn
        h = num * (1.0 / jnp.maximum(jnp.abs(den), jnp.exp(-m_t)))

        m_new = jnp.maximum(b_last + m, jnp.max(b_last + a_col, axis=0, keepdims=True))
        ws_col = jnp.exp(b_last + a_col - m_new)
        ws_row = jnp.exp(b_last + a_row - m_new)
        decay = jnp.exp(b_last + m - m_new)
        kwt = (ktc.astype(F32) * ws_row).astype(BF16)
        c_ref[...] = decay * c_ref[...] + jnp.dot(kwt, vc, preferred_element_type=F32)
        n_new = decay * n + jnp.sum(kc.astype(F32) * ws_col, axis=0, keepdims=True)
        return h, n_new, m_new

    def park(c, h):
        hs_ref[pl.ds(pl.multiple_of(c * L, L), L), :] = h

    def finish(c, h):
        r = pl.ds(pl.multiple_of(c * L, L), L)
        hn = _layer_norm_rows(h + hs_ref[r, :])
        y_ref[r, :] = (jax.nn.sigmoid(o_ref[r, :]) * hn * gnw_ref[...]).astype(BF16)

    def make_body(emit):
        def body(i, carry):
            nf, mf, nb, mb = carry
            hf, nf, mf = chunk(i, True, cf_ref, nf, mf)
            emit(i, hf)
            jb = nc - 1 - i
            hb, nb, mb = chunk(jb, False, cb_ref, nb, mb)
            emit(jb, hb)
            return nf, mf, nb, mb
        return body

    n0 = jnp.zeros((1, M_HEAD_DIM), F32)
    m0 = jnp.full((1, 1), NEG, F32)
    carry = lax.fori_loop(0, nc // 2, make_body(park), (n0, m0, n0, m0))
    lax.fori_loop(nc // 2, nc, make_body(finish), carry)


def _mlstm_call(q, k, kt, zb, gcol, grow, z32, gnw):
    B, S, _ = q.shape
    assert S % (2 * M_CHUNK) == 0
    dh = M_HEAD_DIM
    o_blk0 = 2 * M_WIDTH // dh
    seq = lambda off: pl.BlockSpec((None, S, dh), lambda b, h: (b, 0, off + h))
    return pl.pallas_call(
        _mlstm_kernel,
        grid=(B, M_HEADS),
        in_specs=[seq(0), seq(0),
                  pl.BlockSpec((None, dh, S), lambda b, h: (b, h, 0)),
                  seq(0),
                  pl.BlockSpec((None, None, S, 4), lambda b, h: (b, h, 0, 0)),
                  pl.BlockSpec((None, None, SUBLANES, S), lambda b, h: (b, h, 0, 0)),
                  seq(o_blk0),
                  pl.BlockSpec((1, dh), lambda b, h: (0, h))],
        out_specs=seq(0),
        out_shape=jax.ShapeDtypeStruct((B, S, M_WIDTH), BF16),
        scratch_shapes=[pltpu.VMEM((S, dh), F32), pltpu.VMEM((dh, dh), F32), pltpu.VMEM((dh, dh), F32)],
        compiler_params=_params("arbitrary", "arbitrary"),
        name="mlstm",
    )(q, k, kt, zb, gcol, grow, z32, gnw)


def _attn_kernel(q_ref, k_ref, kp_ref, kn_ref, v_ref, vp_ref, vn_ref, bm_ref, o_ref, lse_ref,
                 kpad, vpad, *, n):
    hb = ATT_HALF
    tq = q_ref.shape[0]
    i = pl.program_id(2)
    kpad[0:hb, :] = kp_ref[...]
    kpad[hb:hb + tq, :] = k_ref[...]
    kpad[hb + tq:, :] = kn_ref[...]
    vpad[0:hb, :] = vp_ref[...]
    vpad[hb:hb + tq, :] = v_ref[...]
    vpad[hb + tq:, :] = vn_ref[...]
    even = lax.broadcasted_iota(jnp.int32, (hb, LANES), 1) < ATT_HEAD_DIM
    colw = lax.broadcasted_iota(jnp.int32, (2 * hb, 3 * hb), 1)

    def body(qb, carry):
        r0 = pl.multiple_of(qb * hb, hb)
        key0 = i * tq + r0 - hb
        oob = (colw + key0 < 0) | (colw + key0 >= n)
        for p in range(ATT_SLOTS // 2):
            ls = slice(p * LANES, (p + 1) * LANES)
            q2 = q_ref[pl.ds(r0, hb), ls] * ATT_HEAD_DIM ** -0.5
            qs = jnp.concatenate([jnp.where(even, q2, 0), jnp.where(even, 0, q2)], axis=0)
            kw = kpad[pl.ds(r0, 3 * hb), ls]
            vw = vpad[pl.ds(r0, 3 * hb), ls]
            s = lax.dot_general(qs, kw, (((1,), (1,)), ((), ())), preferred_element_type=F32)
            s = jnp.where(oob, NEG, s + bm_ref[p])
            m = jnp.max(s, axis=-1, keepdims=True)
            e = jnp.exp(s - m)
            l = jnp.sum(e, axis=-1, keepdims=True)
            o = jnp.dot(e.astype(BF16), vw, preferred_element_type=F32) * (1.0 / l)
            lse = m + jnp.log(l)
            o_ref[pl.ds(r0, hb), ls] = jnp.where(even, o[:hb], o[hb:])
            lse_ref[pl.ds(r0, hb), ls] = jnp.where(even, lse[:hb], lse[hb:])
        return carry

    lax.fori_loop(0, tq // hb, body, 0)


def _attn_bias(g):
    h = np.arange(1, ATT_HEADS + 1, dtype=np.float32)
    slopes = np.exp2(-8.0 * h / ATT_HEADS).astype(np.float32).reshape(N_PATTERNS, ATT_SLOTS)[g]
    dil = ATT_PATTERNS[g][1]
    qi = np.arange(ATT_HALF)[:, None] + ATT_HALF
    kj = np.arange(3 * ATT_HALF)[None, :]
    dist = np.abs(qi - kj)
    out = np.empty((ATT_SLOTS // 2, 2 * ATT_HALF, 3 * ATT_HALF), np.float32)
    for p in range(ATT_SLOTS // 2):
        for e in range(2):
            bias = -slopes[2 * p + e] * (dil * dist).astype(np.float32)
            out[p, e * ATT_HALF:(e + 1) * ATT_HALF] = np.where(dist <= ATT_HALF, bias, np.float32(NEG))
    return jnp.asarray(out)


def _attn_call(zb, g):
    B, S, W = zb.shape
    dil = ATT_PATTERNS[g][1]
    n = S // dil
    hb = ATT_HALF
    tq = min(n, 512)
    cw = ATT_OUT
    zr = zb.reshape(B, n, dil * W)
    wb = W // cw
    qo, ko, vo = (M_WIDTH + t * ATT_WIDTH + g * cw for t in range(3))
    rb, nrb = tq // hb, n // hb

    def main(off):
        return pl.BlockSpec((None, tq, cw), lambda b, r, i: (b, i, r * wb + off // cw))

    def prev(off):
        return pl.BlockSpec((None, hb, cw), lambda b, r, i: (b, jnp.maximum(i * rb - 1, 0), r * wb + off // cw))

    def nxt(off):
        return pl.BlockSpec((None, hb, cw), lambda b, r, i: (b, jnp.minimum((i + 1) * rb, nrb - 1), r * wb + off // cw))

    out_spec = pl.BlockSpec((None, tq, cw), lambda b, r, i: (b, i, r))
    o, lse = pl.pallas_call(
        functools.partial(_attn_kernel, n=n),
        grid=(B, dil, n // tq),
        in_specs=[main(qo), main(ko), prev(ko), nxt(ko), main(vo), prev(vo), nxt(vo),
                  pl.BlockSpec((ATT_SLOTS // 2, 2 * hb, 3 * hb), lambda b, r, i: (0, 0, 0))],
        out_specs=[out_spec, out_spec],
        out_shape=[jax.ShapeDtypeStruct((B, n, dil * cw), F32)] * 2,
        scratch_shapes=[pltpu.VMEM((tq + 2 * hb, cw), BF16), pltpu.VMEM((tq + 2 * hb, cw), BF16)],
        compiler_params=_params("arbitrary", "arbitrary", "arbitrary"),
        name=f"attn{g}",
    )(zr, zr, zr, zr, zr, zr, zr, _attn_bias(g))
    return o.reshape(B, S, cw), lse.reshape(B, S, cw)


def _mix_kernel(ym_ref, o0_ref, o1_ref, o2_ref, l0_ref, l1_ref, l2_ref, ga_ref, gb_ref, h_ref,
                wa_ref, wb_ref, wo_ref, ada_ref, lng_ref, lnb_ref, hn_ref, u_ref):
    l0, l1, l2 = l0_ref[...], l1_ref[...], l2_ref[...]
    lm = jnp.maximum(jnp.maximum(l0, l1), l2)
    e0, e1, e2 = jnp.exp(l0 - lm), jnp.exp(l1 - lm), jnp.exp(l2 - lm)
    att = (e0 * o0_ref[...] + e1 * o1_ref[...] + e2 * o2_ref[...]) * (1.0 / (e0 + e1 + e2))
    y_a = jnp.dot(ym_ref[...], wa_ref[...], preferred_element_type=F32)
    y_b = jnp.dot(att.astype(BF16), wb_ref[...], preferred_element_type=F32)
    mix = jax.nn.sigmoid(ga_ref[...]) * y_a + jax.nn.sigmoid(gb_ref[...]) * y_b
    y = jnp.dot(mix.astype(BF16), wo_ref[...], preferred_element_type=F32)
    hn = _layer_norm_rows(DN_ALPHA * h_ref[...] + ada_ref[2:3, :] * y) * lng_ref[...] + lnb_ref[...]
    hn_ref[...] = hn
    u_ref[...] = (hn * (1.0 + ada_ref[4:5, :]) + ada_ref[3:4, :]).astype(BF16)


def _mix_call(ym, att, z32, h, wa, wb, wo, ada_l, lng, lnb, tm=256):
    B, S, D = h.shape
    gcol0 = (2 * M_WIDTH + M_WIDTH) // D
    row = lambda w, c=0: pl.BlockSpec((None, tm, w), lambda b, i: (b, i, c))
    full = lambda a: pl.BlockSpec(a.shape, lambda b, i: (0,) * a.ndim)
    (o0, l0), (o1, l1), (o2, l2) = att
    return pl.pallas_call(
        _mix_kernel,
        grid=(B, S // tm),
        in_specs=[row(D)] + [row(ATT_OUT)] * 6 + [row(D, gcol0), row(D, gcol0 + 1), row(D),
                  full(wa), full(wb), full(wo),
                  pl.BlockSpec((None, SUBLANES, D), lambda b, i: (b, 0, 0)), full(lng), full(lnb)],
        out_specs=[row(D), row(D)],
        out_shape=[jax.ShapeDtypeStruct((B, S, D), F32), jax.ShapeDtypeStruct((B, S, D), BF16)],
        compiler_params=_params("arbitrary", "arbitrary"),
        name="mix",
    )(ym, o0, o1, o2, l0, l1, l2, z32, z32, h, wa, wb, wo, ada_l, lng, lnb)


def _ffn_kernel(u_ref, h_ref, w1_ref, w3_ref, w2_ref, ada_ref, lng_ref, lnb_ref, adan_ref,
                hn_ref, *rest, emit_u):
    acc_ref = rest[-1]
    u = u_ref[...]
    for f in range(w1_ref.shape[1] // FF_CHUNK):
        fs = slice(f * FF_CHUNK, (f + 1) * FF_CHUNK)
        a = jnp.dot(u, w1_ref[:, fs], preferred_element_type=F32)
        b = jnp.dot(u, w3_ref[:, fs], preferred_element_type=F32)
        t = (a * jax.nn.sigmoid(a) * b).astype(BF16)
        part = jnp.dot(t, w2_ref[fs, :], preferred_element_type=F32)
        if f == 0:
            acc_ref[...] = part
        else:
            acc_ref[...] += part
    hn = (_layer_norm_rows(DN_ALPHA * h_ref[...] + ada_ref[5:6, :] * acc_ref[...])
          * lng_ref[...] + lnb_ref[...])
    hn_ref[...] = hn
    if emit_u:
        rest[0][...] = (hn * (1.0 + adan_ref[1:2, :]) + adan_ref[0:1, :]).astype(BF16)


def _ffn_call(u, h, w1, w3, w2, ada_l, lng, lnb, ada_next, emit_u, tm=512):
    B, S, D = h.shape
    row = pl.BlockSpec((None, tm, D), lambda b, i: (b, i, 0))
    full = lambda a: pl.BlockSpec(a.shape, lambda b, i: (0,) * a.ndim)
    ada_spec = pl.BlockSpec((None, SUBLANES, D), lambda b, i: (b, 0, 0))
    out_specs = [row, row] if emit_u else [row]
    out_shape = [jax.ShapeDtypeStruct((B, S, D), F32)]
    if emit_u:
        out_shape.append(jax.ShapeDtypeStruct((B, S, D), BF16))
    return pl.pallas_call(
        functools.partial(_ffn_kernel, emit_u=emit_u),
        grid=(B, S // tm),
        in_specs=[row, row, full(w1), full(w3), full(w2), ada_spec, full(lng), full(lnb), ada_spec],
        out_specs=out_specs, out_shape=out_shape,
        scratch_shapes=[pltpu.VMEM((tm, D), F32)],
        compiler_params=_params("arbitrary", "arbitrary"),
        name="ffn",
    )(u, h, w1, w3, w2, ada_l, lng, lnb, ada_next)


def kernel(x, c, w_in, b_gates, conv_w, conv_b, gn_w, w_a, w_b, w_out, w_ada, b_ada, ln1_g, ln1_b,
           w1, w3, w2, ln2_g, ln2_b):
    B, S, D = x.shape
    depth = w_in.shape[0]
    T = B * S
    H = M_HEADS

    c_pad = jnp.zeros((SUBLANES, D), F32).at[:B].set(c)
    ada = _ada_call(c_pad, w_ada, b_ada)
    ada = jnp.pad(ada[:, :, :B].transpose(0, 2, 1, 3), ((0, 0), (0, 0), (0, 2), (0, 0)))

    s0, s1, s2, s3, s4, s5 = np.cumsum((0,) + IN_SECTIONS)[:6].tolist()
    h, u = _ln0_call(x, ada[0])
    for l in range(depth):
        wl = w_in[l]
        w32 = jnp.concatenate([wl[:, s0:s1], wl[:, s2:s3], wl[:, s5:]], axis=1).astype(BF16)
        wzb = jnp.concatenate([wl[:, s1:s2], wl[:, s4:s5]], axis=1).astype(BF16)
        wg = jnp.pad(wl[:, s3:s4], ((0, 0), (0, LANES - 4 * H))).astype(BF16)
        bg = jnp.pad(b_gates[l], (0, LANES - 4 * H)).reshape(1, LANES)

        u2d = u.reshape(T, D)
        z32 = _proj_call(u2d, w32, F32, 1024, 1024, "proj32").reshape(B, S, Z32_WIDTH)
        zb = _proj_call(u2d, wzb, BF16, 1024, 512, "projb").reshape(B, S, ZB_WIDTH)
        gates = _gate_call(u2d, wg, bg)[:, :4 * H].reshape(B, S, 4, H)
        gcol = gates.transpose(0, 3, 1, 2)
        grow = jnp.pad(gates.transpose(0, 3, 2, 1), ((0, 0), (0, 0), (0, SUBLANES - 4), (0, 0)))

        cw8 = jnp.pad(conv_w[l], ((0, SUBLANES - CONV_WIDTH), (0, 0)))
        cb = conv_b[l].reshape(1, 2 * M_WIDTH)
        (q,) = _conv_call(z32, cw8, cb, 0, M_HEAD_DIM ** -0.5, False)
        k, kt = _conv_call(z32, cw8, cb, M_WIDTH, 1.0, True)
        ym = _mlstm_call(q, k, kt, zb, gcol, grow, z32, gn_w[l].reshape(1, M_WIDTH))

        att = [_attn_call(zb, g) for g in range(N_PATTERNS)]
        h, u = _mix_call(ym, att, z32, h, w_a[l].astype(BF16), w_b[l].astype(BF16),
                         w_out[l].astype(BF16), ada[l], ln1_g[l].reshape(1, D), ln1_b[l].reshape(1, D))
        last = l == depth - 1
        outs = _ffn_call(u, h, w1[l].astype(BF16), w3[l].astype(BF16), w2[l].astype(BF16), ada[l],
                         ln2_g[l].reshape(1, D), ln2_b[l].reshape(1, D),
                         ada[l] if last else ada[l + 1], not last)
        h = outs[0]
        if not last:
            u = outs[1]
    return h
```

```python
import functools

import jax
import jax.numpy as jnp
import numpy as np
from jax import lax
from jax.experimental import pallas as pl
from jax.experimental.pallas import tpu as pltpu

F32 = jnp.float32
BF16 = jnp.bfloat16

D_MODEL = 1024
DEPTH = 2
M_HEADS = 4
M_HEAD_DIM = 256
M_WIDTH = M_HEADS * M_HEAD_DIM
M_CHUNK = 128
CONV_WIDTH = 5
ATT_SLOTS = 8
ATT_HEAD_DIM = 64
ATT_PATTERNS = ((128, 1), (512, 4), (2048, 16))
N_PATTERNS = 3
ATT_HEADS = ATT_SLOTS * N_PATTERNS
ATT_WIDTH = ATT_HEADS * ATT_HEAD_DIM
ATT_OUT = ATT_SLOTS * ATT_HEAD_DIM
ATT_HALF = 64
D_FF = -(-8 * D_MODEL // (3 * 256)) * 256
DN_ALPHA = (2 * DEPTH) ** 0.25
LN_EPS = 1e-5
NEG = -1e30
IN_SECTIONS = (2 * M_WIDTH, M_WIDTH, M_WIDTH, 4 * M_HEADS, 3 * ATT_WIDTH, 2 * D_MODEL)

VMEM_LIMIT_BYTES = 56 * 1024 * 1024
LANES = 128
SUBLANES = 8

Z32_WIDTH = 2 * M_WIDTH + M_WIDTH + 2 * D_MODEL
ZB0_WIDTH = M_WIDTH + 3 * ATT_OUT
FF_CHUNK = 256


def _params(*sem):
    return pltpu.CompilerParams(dimension_semantics=sem, vmem_limit_bytes=VMEM_LIMIT_BYTES)


def _layer_norm_rows(x):
    mu = jnp.mean(x, axis=-1, keepdims=True)
    xc = x - mu
    var = jnp.mean(xc * xc, axis=-1, keepdims=True)
    return xc * lax.rsqrt(var + LN_EPS)


def _ada_kernel(c_ref, w_ref, b_ref, o_ref):
    c = c_ref[...]
    ca = c * jax.nn.sigmoid(c)
    o_ref[...] = jnp.dot(ca, w_ref[...], precision=lax.Precision.HIGHEST,
                         preferred_element_type=F32) + b_ref[...]


def _ada_call(c_pad, w_ada, b_ada):
    depth = w_ada.shape[0]
    return pl.pallas_call(
        _ada_kernel,
        grid=(depth, 6),
        in_specs=[pl.BlockSpec((SUBLANES, D_MODEL), lambda l, k: (0, 0)),
                  pl.BlockSpec((None, D_MODEL, D_MODEL), lambda l, k: (l, 0, k)),
                  pl.BlockSpec((None, 1, D_MODEL), lambda l, k: (l, 0, k))],
        out_specs=pl.BlockSpec((None, None, SUBLANES, D_MODEL), lambda l, k: (l, k, 0, 0)),
        out_shape=jax.ShapeDtypeStruct((depth, 6, SUBLANES, D_MODEL), F32),
        compiler_params=_params("arbitrary", "arbitrary"),
        name="ada",
    )(c_pad, w_ada, b_ada.reshape(depth, 1, 6 * D_MODEL))


def _ln0_kernel(x_ref, ada_ref, h_ref, u_ref):
    y = _layer_norm_rows(x_ref[...])
    h_ref[...] = y
    u_ref[...] = (y * (1.0 + ada_ref[1:2, :]) + ada_ref[0:1, :]).astype(BF16)


def _ln0_call(x, ada0, ts=512):
    B, S, D = x.shape
    row = pl.BlockSpec((None, ts, D), lambda b, i: (b, i, 0))
    return pl.pallas_call(
        _ln0_kernel,
        grid=(B, S // ts),
        in_specs=[row, pl.BlockSpec((None, SUBLANES, D), lambda b, i: (b, 0, 0))],
        out_specs=[row, row],
        out_shape=[jax.ShapeDtypeStruct((B, S, D), F32), jax.ShapeDtypeStruct((B, S, D), BF16)],
        compiler_params=_params("arbitrary", "arbitrary"),
        name="ln0",
    )(x, ada0)


def _proj_kernel(u_ref, w_ref, o_ref):
    o_ref[...] = jnp.dot(u_ref[...], w_ref[...], preferred_element_type=F32).astype(o_ref.dtype)


def _proj_call(u2d, w, out_dtype, tm, tn, name):
    T, D = u2d.shape
    N = w.shape[1]
    return pl.pallas_call(
        _proj_kernel,
        grid=(T // tm, N // tn),
        in_specs=[pl.BlockSpec((tm, D), lambda i, j: (i, 0)),
                  pl.BlockSpec((D, tn), lambda i, j: (0, j))],
        out_specs=pl.BlockSpec((tm, tn), lambda i, j: (i, j)),
        out_shape=jax.ShapeDtypeStruct((T, N), out_dtype),
        compiler_params=_params("arbitrary", "arbitrary"),
        name=name,
    )(u2d, w)


def _proj_dil_kernel(u_ref, w_ref, o_ref, acc_ref, *, dil):
    acc = jnp.dot(u_ref[...], w_ref[...], preferred_element_type=F32)
    rows = u_ref.shape[0] // dil
    for c in range(acc_ref.shape[0]):
        acc_ref[c] = acc[:, c * LANES:(c + 1) * LANES]
    for r in range(dil):
        for c in range(acc_ref.shape[0]):
            o_ref[r, :, c * LANES:(c + 1) * LANES] = (
                acc_ref[c, pl.ds(r, rows, stride=dil), :].astype(o_ref.dtype))


def _proj_dil_call(u, w, dil, tm, tn, name):
    B, S, D = u.shape
    N = w.shape[1]
    return pl.pallas_call(
        functools.partial(_proj_dil_kernel, dil=dil),
        grid=(B, S // tm, N // tn),
        in_specs=[pl.BlockSpec((None, tm, D), lambda b, i, j: (b, i, 0)),
                  pl.BlockSpec((D, tn), lambda b, i, j: (0, j))],
        out_specs=pl.BlockSpec((None, dil, tm // dil, tn), lambda b, i, j: (b, 0, i, j)),
        out_shape=jax.ShapeDtypeStruct((B, dil, S // dil, N), BF16),
        scratch_shapes=[pltpu.VMEM((tn // LANES, tm, LANES), F32)],
        compiler_params=_params("arbitrary", "arbitrary", "arbitrary"),
        name=name,
    )(u, w)


def _gate_kernel(u_ref, w_ref, b_ref, o_ref):
    g = jnp.dot(u_ref[...], w_ref[...], preferred_element_type=F32) + b_ref[...]
    lf = jnp.minimum(g, 0.0) - jnp.log1p(jnp.exp(-jnp.abs(g)))
    row = lax.broadcasted_iota(jnp.int32, (M_CHUNK, LANES), 0)
    lane = lax.broadcasted_iota(jnp.int32, (M_CHUNK, LANES), 1)
    H = M_HEADS
    for c in range(u_ref.shape[0] // M_CHUNK):
        sl = slice(c * M_CHUNK, (c + 1) * M_CHUNK)
        gc, pre, suf = g[sl], lf[sl], lf[sl]
        d = 1
        while d < M_CHUNK:
            pre = pre + jnp.where(row >= d, pltpu.roll(pre, d, 0), 0.0)
            suf = suf + jnp.where(row < M_CHUNK - d, pltpu.roll(suf, M_CHUNK - d, 0), 0.0)
            d *= 2
        a_f = gc - pltpu.roll(pre, LANES - H, 1)
        a_b = gc - pltpu.roll(suf, LANES - H, 1)
        o_ref[sl, :] = jnp.where(lane < H, a_f, jnp.where(lane < 2 * H, pre,
                                 jnp.where(lane < 3 * H, a_b, suf)))


def _gate_call(u2d, wg, bg, tm=512):
    T, D = u2d.shape
    return pl.pallas_call(
        _gate_kernel,
        grid=(T // tm,),
        in_specs=[pl.BlockSpec((tm, D), lambda i: (i, 0)),
                  pl.BlockSpec((D, LANES), lambda i: (0, 0)),
                  pl.BlockSpec((1, LANES), lambda i: (0, 0))],
        out_specs=pl.BlockSpec((tm, LANES), lambda i: (i, 0)),
        out_shape=jax.ShapeDtypeStruct((T, LANES), F32),
        compiler_params=_params("arbitrary"),
        name="gates",
    )(u2d, wg, bg)


def _conv_kernel(x_ref, p_ref, n_ref, w_ref, b_ref, *out_refs, scale, transpose_out):
    i = pl.program_id(1)
    ts = x_ref.shape[0]
    x = x_ref[...]
    prev = jnp.where(i > 0, p_ref[...], 0.0)
    nxt = jnp.where(i < pl.num_programs(1) - 1, n_ref[...], 0.0)
    row8 = lax.broadcasted_iota(jnp.int32, (SUBLANES, x.shape[1]), 0)

    def shifted(k):
        if k == 0:
            return x
        r = pltpu.roll(x, (-k) % ts, 0)
        if k < 0:
            edge = jnp.where(row8 < -k, pltpu.roll(prev, -k, 0), r[:SUBLANES])
            return jnp.concatenate([edge, r[SUBLANES:]], axis=0)
        edge = jnp.where(row8 >= SUBLANES - k, pltpu.roll(nxt, SUBLANES - k, 0), r[ts - SUBLANES:])
        return jnp.concatenate([r[:ts - SUBLANES], edge], axis=0)

    y = b_ref[...] + w_ref[0:1, :] * shifted(-2)
    for j in range(1, CONV_WIDTH):
        y = y + w_ref[j:j + 1, :] * shifted(j - CONV_WIDTH // 2)
    y = y * jax.nn.sigmoid(y)
    if scale != 1.0:
        y = y * scale
    out_refs[0][...] = y.astype(BF16)
    if transpose_out:
        out_refs[1][...] = y.T.astype(BF16)


def _conv_call(z32, conv_w8, conv_b, col0, scale, transpose_out, ts=512, tc=512):
    B, S, _ = z32.shape
    c0 = col0 // tc
    nsb = S // SUBLANES
    rb = ts // SUBLANES
    in_specs = [
        pl.BlockSpec((None, ts, tc), lambda b, i, c: (b, i, c0 + c)),
        pl.BlockSpec((None, SUBLANES, tc), lambda b, i, c: (b, jnp.maximum(i * rb - 1, 0), c0 + c)),
        pl.BlockSpec((None, SUBLANES, tc), lambda b, i, c: (b, jnp.minimum((i + 1) * rb, nsb - 1), c0 + c)),
        pl.BlockSpec((SUBLANES, tc), lambda b, i, c: (0, c0 + c)),
        pl.BlockSpec((1, tc), lambda b, i, c: (0, c0 + c)),
    ]
    out_specs = [pl.BlockSpec((None, ts, tc), lambda b, i, c: (b, i, c))]
    out_shape = [jax.ShapeDtypeStruct((B, S, M_WIDTH), BF16)]
    if transpose_out:
        out_specs.append(pl.BlockSpec((None, tc, ts), lambda b, i, c: (b, c, i)))
        out_shape.append(jax.ShapeDtypeStruct((B, M_WIDTH, S), BF16))
    return pl.pallas_call(
        functools.partial(_conv_kernel, scale=scale, transpose_out=transpose_out),
        grid=(B, S // ts, M_WIDTH // tc),
        in_specs=in_specs, out_specs=out_specs, out_shape=out_shape,
        compiler_params=_params("arbitrary", "arbitrary", "arbitrary"),
        name="conv_k" if transpose_out else "conv_q",
    )(z32, z32, z32, conv_w8, conv_b)


def _mlstm_kernel(q_ref, k_ref, kt_ref, v_ref, gc_ref, gr_ref, o_ref, gnw_ref, y_ref,
                  hs_ref, cf_ref, cb_ref):
    L = M_CHUNK
    S = q_ref.shape[0]
    nc = S // L
    row = lax.broadcasted_iota(jnp.int32, (L, L), 0)
    col = lax.broadcasted_iota(jnp.int32, (L, L), 1)
    tri = {True: row >= col, False: row <= col}
    cf_ref[...] = jnp.zeros_like(cf_ref)
    cb_ref[...] = jnp.zeros_like(cb_ref)

    def chunk(c, fwd, c_ref, n, m):
        r0 = pl.multiple_of(c * L, L)
        qc = q_ref[pl.ds(r0, L), :]
        kc = k_ref[pl.ds(r0, L), :]
        vc = v_ref[pl.ds(r0, L), :]
        ktc = kt_ref[:, pl.ds(r0, L)]
        g = gc_ref[pl.ds(r0, L), :]
        j = 0 if fwd else 2
        a_col, b_col = g[:, j:j + 1], g[:, j + 1:j + 2]
        a_row = gr_ref[j:j + 1, pl.ds(r0, L)]
        b_last = b_col[L - 1:L, :] if fwd else b_col[0:1, :]

        log_d = jnp.where(tri[fwd], b_col + a_row, NEG)
        m_inter = b_col + m
        m_t = jnp.maximum(jnp.max(log_d, axis=-1, keepdims=True), m_inter)
        w_inter = jnp.exp(m_inter - m_t)
        s = jnp.dot(qc, ktc, preferred_element_type=F32) * jnp.exp(log_d - m_t)
        num = (jnp.dot(s.astype(BF16), vc, preferred_element_type=F32)
               + w_inter * jnp.dot(qc, c_ref[...].astype(BF16), preferred_element_type=F32))
        qn = jnp.sum(qc.astype(F32) * n, axis=-1, keepdims=True)
        den = jnp.sum(s, axis=-1, keepdims=True) + w_inter * qn
        h = num * (1.0 / jnp.maximum(jnp.abs(den), jnp.exp(-m_t)))

        m_new = jnp.maximum(b_last + m, jnp.max(b_last + a_col, axis=0, keepdims=True))
        ws_col = jnp.exp(b_last + a_col - m_new)
        ws_row = jnp.exp(b_last + a_row - m_new)
        decay = jnp.exp(b_last + m - m_new)
        kwt = (ktc.astype(F32) * ws_row).astype(BF16)
        c_ref[...] = decay * c_ref[...] + jnp.dot(kwt, vc, preferred_element_type=F32)
        n_new = decay * n + jnp.sum(kc.astype(F32) * ws_col, axis=0, keepdims=True)
        return h, n_new, m_new

    def park(c, h):
        hs_ref[pl.ds(pl.multiple_of(c * L, L), L), :] = h

    def finish(c, h):
        r = pl.ds(pl.multiple_of(c * L, L), L)
        hn = _layer_norm_rows(h + hs_ref[r, :])
        y_ref[r, :] = (jax.nn.sigmoid(o_ref[r, :]) * hn * gnw_ref[...]).astype(BF16)

    def make_body(emit):
        def body(i, carry):
            nf, mf, nb, mb = carry
            hf, nf, mf = chunk(i, True, cf_ref, nf, mf)
            emit(i, hf)
            jb = nc - 1 - i
            hb, nb, mb = chunk(jb, False, cb_ref, nb, mb)
            emit(jb, hb)
            return nf, mf, nb, mb
        return body

    n0 = jnp.zeros((1, M_HEAD_DIM), F32)
    m0 = jnp.full((1, 1), NEG, F32)
    carry = lax.fori_loop(0, nc // 2, make_body(park), (n0, m0, n0, m0))
    lax.fori_loop(nc // 2, nc, make_body(finish), carry)


def _mlstm_call(q, k, kt, zb, gcol, grow, z32, gnw):
    B, S, _ = q.shape
    assert S % (2 * M_CHUNK) == 0
    dh = M_HEAD_DIM
    o_blk0 = 2 * M_WIDTH // dh
    seq = lambda off: pl.BlockSpec((None, S, dh), lambda b, h: (b, 0, off + h))
    return pl.pallas_call(
        _mlstm_kernel,
        grid=(B, M_HEADS),
        in_specs=[seq(0), seq(0),
                  pl.BlockSpec((None, dh, S), lambda b, h: (b, h, 0)),
                  seq(0),
                  pl.BlockSpec((None, None, S, 4), lambda b, h: (b, h, 0, 0)),
                  pl.BlockSpec((None, None, SUBLANES, S), lambda b, h: (b, h, 0, 0)),
                  seq(o_blk0),
                  pl.BlockSpec((1, dh), lambda b, h: (0, h))],
        out_specs=seq(0),
        out_shape=jax.ShapeDtypeStruct((B, S, M_WIDTH), BF16),
        scratch_shapes=[pltpu.VMEM((S, dh), F32), pltpu.VMEM((dh, dh), F32), pltpu.VMEM((dh, dh), F32)],
        compiler_params=_params("arbitrary", "arbitrary"),
        name="mlstm",
    )(q, k, kt, zb, gcol, grow, z32, gnw)


ATT_UNROLL = 4


def _attn_kernel(q_ref, k_ref, kp_ref, kn_ref, v_ref, vp_ref, vn_ref, bm_ref, o_ref, lse_ref,
                 kpad, vpad, *stage, dil):
    hb = ATT_HALF
    tq = q_ref.shape[1]
    nqb = tq // hb
    i = pl.program_id(1)
    ni = pl.num_programs(1)
    kpad[:, 0:hb, :] = kp_ref[...]
    kpad[:, hb:hb + tq, :] = k_ref[...]
    kpad[:, hb + tq:, :] = kn_ref[...]
    vpad[:, 0:hb, :] = vp_ref[...]
    vpad[:, hb:hb + tq, :] = v_ref[...]
    vpad[:, hb + tq:, :] = vn_ref[...]
    even = lax.broadcasted_iota(jnp.int32, (hb, LANES), 1) < ATT_HEAD_DIM
    ones = jnp.ones((3 * hb, LANES), BF16)
    if dil > 1:
        os_ref, ls_ref = stage

    npair = ATT_SLOTS // 2
    total = dil * nqb
    step = min(ATT_UNROLL, total)

    def body(it, carry):
        where, scores = [], []
        for j in range(step):
            idx = it * step + j
            r, qb = idx // nqb, idx % nqb
            r0 = pl.multiple_of(qb * hb, hb)
            case = (jnp.logical_and(i == 0, qb == 0).astype(jnp.int32)
                    + 2 * jnp.logical_and(i == ni - 1, qb == nqb - 1).astype(jnp.int32))
            for p in range(npair):
                ls = slice(p * LANES, (p + 1) * LANES)
                q2 = q_ref[r, pl.ds(r0, hb), ls] * ATT_HEAD_DIM ** -0.5
                qs = jnp.concatenate([jnp.where(even, q2, 0), jnp.where(even, 0, q2)], axis=0)
                kw = kpad[r, pl.ds(r0, 3 * hb), ls]
                s = lax.dot_general(qs, kw, (((1,), (1,)), ((), ())), preferred_element_type=F32)
                scores.append(s + bm_ref[case, p])
                where.append((r, r0, p, ls))
        s = jnp.concatenate(scores, axis=0)
        m = jnp.max(s, axis=-1, keepdims=True)
        e = jnp.exp(s - m).astype(BF16)
        outs = []
        for u, (r, r0, p, ls) in enumerate(where):
            vw = jnp.concatenate([vpad[r, pl.ds(r0, 3 * hb), ls], ones], axis=1)
            outs.append(jnp.dot(e[u * 2 * hb:(u + 1) * 2 * hb], vw, preferred_element_type=F32))
        ol = jnp.concatenate(outs, axis=0)
        l = ol[:, LANES:]
        o = ol[:, :LANES] * (1.0 / l)
        lse = m + jnp.log(l)
        for u, (r, r0, p, ls) in enumerate(where):
            top, bot = slice(u * 2 * hb, u * 2 * hb + hb), slice(u * 2 * hb + hb, (u + 1) * 2 * hb)
            o2 = jnp.where(even, o[top], o[bot])
            l2 = jnp.where(even, lse[top], lse[bot])
            if dil > 1:
                os_ref[r, pl.ds(r0, hb), ls] = o2
                ls_ref[r, pl.ds(r0, hb), ls] = l2
            else:
                o_ref[p, pl.ds(r0, hb), :] = o2
                lse_ref[p, pl.ds(r0, hb), :] = l2
        return carry

    lax.fori_loop(0, total // step, body, 0)
    if dil > 1:
        for r in range(dil):
            for p in range(ATT_SLOTS // 2):
                ls = slice(p * LANES, (p + 1) * LANES)
                o_ref[p, pl.ds(r, tq, stride=dil), :] = os_ref[r, :, ls]
                lse_ref[p, pl.ds(r, tq, stride=dil), :] = ls_ref[r, :, ls]


def _attn_bias(g):
    h = np.arange(1, ATT_HEADS + 1, dtype=np.float32)
    slopes = np.exp2(-8.0 * h / ATT_HEADS).astype(np.float32).reshape(N_PATTERNS, ATT_SLOTS)[g]
    dil = ATT_PATTERNS[g][1]
    hb = ATT_HALF
    qi = np.arange(hb)[:, None] + hb
    kj = np.arange(3 * hb)[None, :]
    dist = np.abs(qi - kj)
    out = np.empty((4, ATT_SLOTS // 2, 2 * hb, 3 * hb), np.float32)
    for case in range(4):
        valid = dist <= hb
        if case & 1:
            valid = valid & (kj >= hb)
        if case & 2:
            valid = valid & (kj < 2 * hb)
        for p in range(ATT_SLOTS // 2):
            for e in range(2):
                bias = -slopes[2 * p + e] * (dil * dist).astype(np.float32)
                out[case, p, e * hb:(e + 1) * hb] = np.where(valid, bias, np.float32(NEG))
    return jnp.asarray(out)


def _attn_call(zq, g, col0):
    B, dil, n, W = zq.shape
    S = n * dil
    hb = ATT_HALF
    tq = min(n, max(512 // dil, hb))
    cw = ATT_OUT
    qc, kc, vc = (col0 // cw + t for t in range(3))
    rb, nrb = tq // hb, n // hb

    def main(c):
        return pl.BlockSpec((None, dil, tq, cw), lambda b, i: (b, 0, i, c))

    def prev(c):
        return pl.BlockSpec((None, dil, hb, cw), lambda b, i: (b, 0, jnp.maximum(i * rb - 1, 0), c))

    def nxt(c):
        return pl.BlockSpec((None, dil, hb, cw), lambda b, i: (b, 0, jnp.minimum((i + 1) * rb, nrb - 1), c))

    npair = ATT_SLOTS // 2
    out_spec = pl.BlockSpec((None, npair, tq * dil, LANES), lambda b, i: (b, 0, i, 0))
    scratch = [pltpu.VMEM((dil, tq + 2 * hb, cw), BF16), pltpu.VMEM((dil, tq + 2 * hb, cw), BF16)]
    if dil > 1:
        scratch += [pltpu.VMEM((dil, tq, cw), F32), pltpu.VMEM((dil, tq, cw), F32)]
    return pl.pallas_call(
        functools.partial(_attn_kernel, dil=dil),
        grid=(B, n // tq),
        in_specs=[main(qc), main(kc), prev(kc), nxt(kc), main(vc), prev(vc), nxt(vc),
                  pl.BlockSpec((4, ATT_SLOTS // 2, 2 * hb, 3 * hb), lambda b, i: (0, 0, 0, 0))],
        out_specs=[out_spec, out_spec],
        out_shape=[jax.ShapeDtypeStruct((B, npair, S, LANES), F32)] * 2,
        scratch_shapes=scratch,
        compiler_params=_params("arbitrary", "arbitrary"),
        name=f"attn{g}",
    )(zq, zq, zq, zq, zq, zq, zq, _attn_bias(g))


def _mix_kernel(ym_ref, o0_ref, o1_ref, o2_ref, l0_ref, l1_ref, l2_ref, ga_ref, gb_ref, h_ref,
                wa_ref, wb_ref, wo_ref, ada_ref, lng_ref, lnb_ref, hn_ref, u_ref):
    pairs = []
    for p in range(ATT_SLOTS // 2):
        l0, l1, l2 = l0_ref[p], l1_ref[p], l2_ref[p]
        lm = jnp.maximum(jnp.maximum(l0, l1), l2)
        e0, e1, e2 = jnp.exp(l0 - lm), jnp.exp(l1 - lm), jnp.exp(l2 - lm)
        pairs.append((e0 * o0_ref[p] + e1 * o1_ref[p] + e2 * o2_ref[p]) * (1.0 / (e0 + e1 + e2)))
    att = jnp.concatenate(pairs, axis=-1)
    y_a = jnp.dot(ym_ref[...], wa_ref[...], preferred_element_type=F32)
    y_b = jnp.dot(att.astype(BF16), wb_ref[...], preferred_element_type=F32)
    mix = jax.nn.sigmoid(ga_ref[...]) * y_a + jax.nn.sigmoid(gb_ref[...]) * y_b
    y = jnp.dot(mix.astype(BF16), wo_ref[...], preferred_element_type=F32)
    hn = _layer_norm_rows(DN_ALPHA * h_ref[...] + ada_ref[2:3, :] * y) * lng_ref[...] + lnb_ref[...]
    hn_ref[...] = hn
    u_ref[...] = (hn * (1.0 + ada_ref[4:5, :]) + ada_ref[3:4, :]).astype(BF16)


def _mix_call(ym, att, z32, h, wa, wb, wo, ada_l, lng, lnb, tm=256):
    B, S, D = h.shape
    gcol0 = (2 * M_WIDTH + M_WIDTH) // D
    row = lambda w, c=0: pl.BlockSpec((None, tm, w), lambda b, i: (b, i, c))
    full = lambda a: pl.BlockSpec(a.shape, lambda b, i: (0,) * a.ndim)
    (o0, l0), (o1, l1), (o2, l2) = att
    return pl.pallas_call(
        _mix_kernel,
        grid=(B, S // tm),
        in_specs=[row(D)] + [pl.BlockSpec((None, ATT_SLOTS // 2, tm, LANES), lambda b, i: (b, 0, i, 0))] * 6
                 + [row(D, gcol0), row(D, gcol0 + 1), row(D),
                  full(wa), full(wb), full(wo),
                  pl.BlockSpec((None, SUBLANES, D), lambda b, i: (b, 0, 0)), full(lng), full(lnb)],
        out_specs=[row(D), row(D)],
        out_shape=[jax.ShapeDtypeStruct((B, S, D), F32), jax.ShapeDtypeStruct((B, S, D), BF16)],
        compiler_params=_params("arbitrary", "arbitrary"),
        name="mix",
    )(ym, o0, o1, o2, l0, l1, l2, z32, z32, h, wa, wb, wo, ada_l, lng, lnb)


def _ffn_kernel(u_ref, h_ref, w1_ref, w3_ref, w2_ref, ada_ref, lng_ref, lnb_ref, adan_ref,
                hn_ref, *rest, emit_u):
    acc_ref = rest[-1]
    u = u_ref[...]
    for f in range(w1_ref.shape[1] // FF_CHUNK):
        fs = slice(f * FF_CHUNK, (f + 1) * FF_CHUNK)
        a = jnp.dot(u, w1_ref[:, fs], preferred_element_type=F32)
        b = jnp.dot(u, w3_ref[:, fs], preferred_element_type=F32)
        t = (a * jax.nn.sigmoid(a) * b).astype(BF16)
        part = jnp.dot(t, w2_ref[fs, :], preferred_element_type=F32)
        if f == 0:
            acc_ref[...] = part
        else:
            acc_ref[...] += part
    hn = (_layer_norm_rows(DN_ALPHA * h_ref[...] + ada_ref[5:6, :] * acc_ref[...])
          * lng_ref[...] + lnb_ref[...])
    hn_ref[...] = hn
    if emit_u:
        rest[0][...] = (hn * (1.0 + adan_ref[1:2, :]) + adan_ref[0:1, :]).astype(BF16)


def _ffn_call(u, h, w1, w3, w2, ada_l, lng, lnb, ada_next, emit_u, tm=512):
    B, S, D = h.shape
    row = pl.BlockSpec((None, tm, D), lambda b, i: (b, i, 0))
    full = lambda a: pl.BlockSpec(a.shape, lambda b, i: (0,) * a.ndim)
    ada_spec = pl.BlockSpec((None, SUBLANES, D), lambda b, i: (b, 0, 0))
    out_specs = [row, row] if emit_u else [row]
    out_shape = [jax.ShapeDtypeStruct((B, S, D), F32)]
    if emit_u:
        out_shape.append(jax.ShapeDtypeStruct((B, S, D), BF16))
    return pl.pallas_call(
        functools.partial(_ffn_kernel, emit_u=emit_u),
        grid=(B, S // tm),
        in_specs=[row, row, full(w1), full(w3), full(w2), ada_spec, full(lng), full(lnb), ada_spec],
        out_specs=out_specs, out_shape=out_shape,
        scratch_shapes=[pltpu.VMEM((tm, D), F32)],
        compiler_params=_params("arbitrary", "arbitrary"),
        name="ffn",
    )(u, h, w1, w3, w2, ada_l, lng, lnb, ada_next)


def kernel(x, c, w_in, b_gates, conv_w, conv_b, gn_w, w_a, w_b, w_out, w_ada, b_ada, ln1_g, ln1_b,
           w1, w3, w2, ln2_g, ln2_b):
    B, S, D = x.shape
    depth = w_in.shape[0]
    T = B * S
    H = M_HEADS

    c_pad = jnp.zeros((SUBLANES, D), F32).at[:B].set(c)
    ada = _ada_call(c_pad, w_ada, b_ada)
    ada = jnp.pad(ada[:, :, :B].transpose(0, 2, 1, 3), ((0, 0), (0, 0), (0, 2), (0, 0)))

    s0, s1, s2, s3, s4, s5 = np.cumsum((0,) + IN_SECTIONS)[:6].tolist()
    h, u = _ln0_call(x, ada[0])
    for l in range(depth):
        wl = w_in[l]
        w32 = jnp.concatenate([wl[:, s0:s1], wl[:, s2:s3], wl[:, s5:]], axis=1).astype(BF16)
        att_w = [jnp.concatenate([wl[:, s4 + t * ATT_WIDTH + g * ATT_OUT:s4 + t * ATT_WIDTH + (g + 1) * ATT_OUT]
                                  for t in range(3)], axis=1).astype(BF16) for g in range(N_PATTERNS)]
        wzb = jnp.concatenate([wl[:, s1:s2].astype(BF16), att_w[0]], axis=1)
        wg = jnp.pad(wl[:, s3:s4], ((0, 0), (0, LANES - 4 * H))).astype(BF16)
        bg = jnp.pad(b_gates[l], (0, LANES - 4 * H)).reshape(1, LANES)

        u2d = u.reshape(T, D)
        z32 = _proj_call(u2d, w32, F32, 1024, 1024, "proj32").reshape(B, S, Z32_WIDTH)
        zb = _proj_call(u2d, wzb, BF16, 1024, 512, "projb0").reshape(B, S, ZB0_WIDTH)
        zq = [zb.reshape(B, 1, S, ZB0_WIDTH)] + [
            _proj_dil_call(u, att_w[g], ATT_PATTERNS[g][1], 1024, 512, f"projb{g}") for g in range(1, N_PATTERNS)]
        gates = _gate_call(u2d, wg, bg)[:, :4 * H].reshape(B, S, 4, H)
        gcol = gates.transpose(0, 3, 1, 2)
        grow = jnp.pad(gates.transpose(0, 3, 2, 1), ((0, 0), (0, 0), (0, SUBLANES - 4), (0, 0)))

        cw8 = jnp.pad(conv_w[l], ((0, SUBLANES - CONV_WIDTH), (0, 0)))
        cb = conv_b[l].reshape(1, 2 * M_WIDTH)
        (q,) = _conv_call(z32, cw8, cb, 0, M_HEAD_DIM ** -0.5, False)
        k, kt = _conv_call(z32, cw8, cb, M_WIDTH, 1.0, True)
        ym = _mlstm_call(q, k, kt, zb, gcol, grow, z32, gn_w[l].reshape(1, M_WIDTH))

        att = [_attn_call(zq[g], g, M_WIDTH if g == 0 else 0) for g in range(N_PATTERNS)]
        h, u = _mix_call(ym, att, z32, h, w_a[l].astype(BF16), w_b[l].astype(BF16),
                         w_out[l].astype(BF16), ada[l], ln1_g[l].reshape(1, D), ln1_b[l].reshape(1, D))
        last = l == depth - 1
        outs = _ffn_call(u, h, w1[l].astype(BF16), w3[l].astype(BF16), w2[l].astype(BF16), ada[l],
                         ln2_g[l].reshape(1, D), ln2_b[l].reshape(1, D),
                         ada[l] if last else ada[l + 1], not last)
        h = outs[0]
        if not last:
            u = outs[1]
    return h
```

```python
import functools

import jax
import jax.numpy as jnp
import numpy as np
from jax import lax
from jax.experimental import pallas as pl
from jax.experimental.pallas import tpu as pltpu

F32 = jnp.float32
BF16 = jnp.bfloat16

D_MODEL = 1024
DEPTH = 2
M_HEADS = 4
M_HEAD_DIM = 256
M_WIDTH = M_HEADS * M_HEAD_DIM
M_CHUNK = 128
CONV_WIDTH = 5
ATT_SLOTS = 8
ATT_HEAD_DIM = 64
ATT_PATTERNS = ((128, 1), (512, 4), (2048, 16))
N_PATTERNS = 3
ATT_HEADS = ATT_SLOTS * N_PATTERNS
ATT_WIDTH = ATT_HEADS * ATT_HEAD_DIM
ATT_OUT = ATT_SLOTS * ATT_HEAD_DIM
ATT_HALF = 64
D_FF = -(-8 * D_MODEL // (3 * 256)) * 256
DN_ALPHA = (2 * DEPTH) ** 0.25
LN_EPS = 1e-5
NEG = -1e30
IN_SECTIONS = (2 * M_WIDTH, M_WIDTH, M_WIDTH, 4 * M_HEADS, 3 * ATT_WIDTH, 2 * D_MODEL)

VMEM_LIMIT_BYTES = 56 * 1024 * 1024
LANES = 128
SUBLANES = 8

ZMAIN_WIDTH = 2 * M_WIDTH + M_WIDTH + 2 * D_MODEL
FF_CHUNK = 256


def _params(*sem):
    return pltpu.CompilerParams(dimension_semantics=sem, vmem_limit_bytes=VMEM_LIMIT_BYTES)


def _layer_norm_rows(x):
    mu = jnp.mean(x, axis=-1, keepdims=True)
    xc = x - mu
    var = jnp.mean(xc * xc, axis=-1, keepdims=True)
    return xc * lax.rsqrt(var + LN_EPS)


def _ada_kernel(c_ref, w_ref, b_ref, o_ref):
    c = c_ref[...]
    ca = c * jax.nn.sigmoid(c)
    o_ref[...] = jnp.dot(ca, w_ref[...], precision=lax.Precision.HIGHEST,
                         preferred_element_type=F32) + b_ref[...]


def _ada_call(c_pad, w_ada, b_ada):
    depth = w_ada.shape[0]
    return pl.pallas_call(
        _ada_kernel,
        grid=(depth, 6),
        in_specs=[pl.BlockSpec((SUBLANES, D_MODEL), lambda l, k: (0, 0)),
                  pl.BlockSpec((None, D_MODEL, D_MODEL), lambda l, k: (l, 0, k)),
                  pl.BlockSpec((None, 1, D_MODEL), lambda l, k: (l, 0, k))],
        out_specs=pl.BlockSpec((None, None, SUBLANES, D_MODEL), lambda l, k: (l, k, 0, 0)),
        out_shape=jax.ShapeDtypeStruct((depth, 6, SUBLANES, D_MODEL), F32),
        compiler_params=_params("arbitrary", "arbitrary"),
        name="ada",
    )(c_pad, w_ada, b_ada.reshape(depth, 1, 6 * D_MODEL))


def _ln0_kernel(x_ref, ada_ref, h_ref, u_ref):
    y = _layer_norm_rows(x_ref[...])
    h_ref[...] = y
    u_ref[...] = (y * (1.0 + ada_ref[1:2, :]) + ada_ref[0:1, :]).astype(BF16)


def _ln0_call(x, ada0, ts=512):
    B, S, D = x.shape
    row = pl.BlockSpec((None, ts, D), lambda b, i: (b, i, 0))
    return pl.pallas_call(
        _ln0_kernel,
        grid=(B, S // ts),
        in_specs=[row, pl.BlockSpec((None, SUBLANES, D), lambda b, i: (b, 0, 0))],
        out_specs=[row, row],
        out_shape=[jax.ShapeDtypeStruct((B, S, D), F32), jax.ShapeDtypeStruct((B, S, D), BF16)],
        compiler_params=_params("arbitrary", "arbitrary"),
        name="ln0",
    )(x, ada0)


def _proj_kernel(u_ref, w_ref, o_ref):
    o_ref[...] = jnp.dot(u_ref[...], w_ref[...], preferred_element_type=F32).astype(o_ref.dtype)


def _proj_call(u2d, w, out_dtype, tm, tn, name):
    T, D = u2d.shape
    N = w.shape[1]
    return pl.pallas_call(
        _proj_kernel,
        grid=(T // tm, N // tn),
        in_specs=[pl.BlockSpec((tm, D), lambda i, j: (i, 0)),
                  pl.BlockSpec((D, tn), lambda i, j: (0, j))],
        out_specs=pl.BlockSpec((tm, tn), lambda i, j: (i, j)),
        out_shape=jax.ShapeDtypeStruct((T, N), out_dtype),
        compiler_params=_params("arbitrary", "arbitrary"),
        name=name,
    )(u2d, w)


def _proj_t_kernel(wt_ref, u_ref, o_ref):
    o_ref[...] = lax.dot_general(wt_ref[...], u_ref[...], (((1,), (1,)), ((), ())),
                                 preferred_element_type=F32).astype(o_ref.dtype)


def _proj_t_call(u, wt, tm, name):
    B, S, D = u.shape
    N = wt.shape[0]
    return pl.pallas_call(
        _proj_t_kernel,
        grid=(B, S // tm),
        in_specs=[pl.BlockSpec((N, D), lambda b, i: (0, 0)),
                  pl.BlockSpec((None, tm, D), lambda b, i: (b, i, 0))],
        out_specs=pl.BlockSpec((None, N, tm), lambda b, i: (b, 0, i)),
        out_shape=jax.ShapeDtypeStruct((B, N, S), BF16),
        compiler_params=_params("arbitrary", "arbitrary"),
        name=name,
    )(wt, u)


def _proj_dil_kernel(u_ref, w_ref, o_ref, acc_ref, *, dil):
    u = u_ref[...]
    rows = u_ref.shape[0] // dil
    gw = 2 * LANES
    for j in range(w_ref.shape[1] // gw):
        acc = jnp.dot(u, w_ref[:, j * gw:(j + 1) * gw], preferred_element_type=F32)
        for c in range(gw // LANES):
            acc_ref[j, c] = acc[:, c * LANES:(c + 1) * LANES]
        for r in range(dil):
            for c in range(gw // LANES):
                o_ref[r, :, j * gw + c * LANES:j * gw + (c + 1) * LANES] = (
                    acc_ref[j, c, pl.ds(r, rows, stride=dil), :].astype(o_ref.dtype))


def _proj_dil_call(u, w, dil, tm, name):
    B, S, D = u.shape
    N = w.shape[1]
    return pl.pallas_call(
        functools.partial(_proj_dil_kernel, dil=dil),
        grid=(B, S // tm),
        in_specs=[pl.BlockSpec((None, tm, D), lambda b, i: (b, i, 0)),
                  pl.BlockSpec((D, N), lambda b, i: (0, 0))],
        out_specs=pl.BlockSpec((None, dil, tm // dil, N), lambda b, i: (b, 0, i, 0)),
        out_shape=jax.ShapeDtypeStruct((B, dil, S // dil, N), BF16),
        scratch_shapes=[pltpu.VMEM((N // (2 * LANES), 2, tm, LANES), F32)],
        compiler_params=_params("arbitrary", "arbitrary"),
        name=name,
    )(u, w)


def _gate_kernel(u_ref, w_ref, b_ref, o_ref):
    g = jnp.dot(u_ref[...], w_ref[...], preferred_element_type=F32) + b_ref[...]
    lf = jnp.minimum(g, 0.0) - jnp.log1p(jnp.exp(-jnp.abs(g)))
    row = lax.broadcasted_iota(jnp.int32, (M_CHUNK, LANES), 0)
    lane = lax.broadcasted_iota(jnp.int32, (M_CHUNK, LANES), 1)
    H = M_HEADS
    for c in range(u_ref.shape[0] // M_CHUNK):
        sl = slice(c * M_CHUNK, (c + 1) * M_CHUNK)
        gc, pre, suf = g[sl], lf[sl], lf[sl]
        d = 1
        while d < M_CHUNK:
            pre = pre + jnp.where(row >= d, pltpu.roll(pre, d, 0), 0.0)
            suf = suf + jnp.where(row < M_CHUNK - d, pltpu.roll(suf, M_CHUNK - d, 0), 0.0)
            d *= 2
        a_f = gc - pltpu.roll(pre, LANES - H, 1)
        a_b = gc - pltpu.roll(suf, LANES - H, 1)
        e_f = jnp.broadcast_to(pltpu.roll(pre, 3 * H, 1)[M_CHUNK - 1:M_CHUNK], pre.shape)
        e_b = jnp.broadcast_to(pltpu.roll(suf, 2 * H, 1)[0:1], suf.shape)
        o_ref[sl, :] = jnp.where(lane < H, a_f, jnp.where(lane < 2 * H, pre,
                                 jnp.where(lane < 3 * H, a_b, jnp.where(lane < 4 * H, suf,
                                           jnp.where(lane < 5 * H, e_f, e_b)))))


def _gate_call(u2d, wg, bg, tm=512):
    T, D = u2d.shape
    return pl.pallas_call(
        _gate_kernel,
        grid=(T // tm,),
        in_specs=[pl.BlockSpec((tm, D), lambda i: (i, 0)),
                  pl.BlockSpec((D, LANES), lambda i: (0, 0)),
                  pl.BlockSpec((1, LANES), lambda i: (0, 0))],
        out_specs=pl.BlockSpec((tm, LANES), lambda i: (i, 0)),
        out_shape=jax.ShapeDtypeStruct((T, LANES), F32),
        compiler_params=_params("arbitrary"),
        name="gates",
    )(u2d, wg, bg)


def _conv_kernel(x_ref, p_ref, n_ref, w_ref, b_ref, o_ref, *, scale, transpose_out):
    i = pl.program_id(1)
    ts = x_ref.shape[0]
    x = x_ref[...].astype(F32)
    prev = jnp.where(i > 0, p_ref[...].astype(F32)[SUBLANES:], 0.0)
    nxt = jnp.where(i < pl.num_programs(1) - 1, n_ref[...].astype(F32)[:SUBLANES], 0.0)
    row8 = lax.broadcasted_iota(jnp.int32, (SUBLANES, x.shape[1]), 0)

    def shifted(k):
        if k == 0:
            return x
        r = pltpu.roll(x, (-k) % ts, 0)
        if k < 0:
            edge = jnp.where(row8 < -k, pltpu.roll(prev, -k, 0), r[:SUBLANES])
            return jnp.concatenate([edge, r[SUBLANES:]], axis=0)
        edge = jnp.where(row8 >= SUBLANES - k, pltpu.roll(nxt, SUBLANES - k, 0), r[ts - SUBLANES:])
        return jnp.concatenate([r[:ts - SUBLANES], edge], axis=0)

    y = b_ref[...] + w_ref[0:1, :] * shifted(-2)
    for j in range(1, CONV_WIDTH):
        y = y + w_ref[j:j + 1, :] * shifted(j - CONV_WIDTH // 2)
    y = y * jax.nn.sigmoid(y)
    if scale != 1.0:
        y = y * scale
    o_ref[...] = (y.T if transpose_out else y).astype(BF16)


def _conv_call(z, conv_w8, conv_b, col0, scale, transpose_out, ts=512, tc=512):
    B, S, _ = z.shape
    c0 = col0 // tc
    hr = 2 * SUBLANES
    nsb = S // hr
    rb = ts // hr
    in_specs = [
        pl.BlockSpec((None, ts, tc), lambda b, i, c: (b, i, c0 + c)),
        pl.BlockSpec((None, hr, tc), lambda b, i, c: (b, jnp.maximum(i * rb - 1, 0), c0 + c)),
        pl.BlockSpec((None, hr, tc), lambda b, i, c: (b, jnp.minimum((i + 1) * rb, nsb - 1), c0 + c)),
        pl.BlockSpec((SUBLANES, tc), lambda b, i, c: (0, c0 + c)),
        pl.BlockSpec((1, tc), lambda b, i, c: (0, c0 + c)),
    ]
    if transpose_out:
        out_specs = pl.BlockSpec((None, tc, ts), lambda b, i, c: (b, c, i))
        out_shape = jax.ShapeDtypeStruct((B, M_WIDTH, S), BF16)
    else:
        out_specs = pl.BlockSpec((None, ts, tc), lambda b, i, c: (b, i, c))
        out_shape = jax.ShapeDtypeStruct((B, S, M_WIDTH), BF16)
    return pl.pallas_call(
        functools.partial(_conv_kernel, scale=scale, transpose_out=transpose_out),
        grid=(B, S // ts, M_WIDTH // tc),
        in_specs=in_specs, out_specs=out_specs, out_shape=out_shape,
        compiler_params=_params("arbitrary", "arbitrary", "arbitrary"),
        name="conv_qT" if transpose_out else "conv_k",
    )(z, z, z, conv_w8, conv_b)


def _mlstm_kernel(kf_ref, kb_ref, qtf_ref, qtb_ref, vtf_ref, vtb_ref, gf_ref, gb_ref, of_ref, ob_ref,
                  gnw_ref, yhi_ref, ylo_ref, hs_ref, ct_ref, n_ref, m_ref):
    L, H, dh = M_CHUNK, M_HEADS, M_HEAD_DIM
    i = pl.program_id(1)
    nc = pl.num_programs(1)

    @pl.when(i == 0)
    def _init():
        ct_ref[...] = jnp.zeros_like(ct_ref)
        n_ref[...] = jnp.zeros_like(n_ref)
        m_ref[...] = jnp.full(m_ref.shape, NEG, F32)

    row = lax.broadcasted_iota(jnp.int32, (L, L), 0)
    col = lax.broadcasted_iota(jnp.int32, (L, L), 1)
    mask = (row <= col, row >= col)
    k_refs, qt_refs, vt_refs = (kf_ref, kb_ref), (qtf_ref, qtb_ref), (vtf_ref, vtb_ref)
    g = (gf_ref[...], gb_ref[...])
    gt = (g[0].T, g[1].T)
    units = [(d, h) for d in range(2) for h in range(H)]
    twice = lambda r: jnp.concatenate([r, r], axis=1)

    gate = []
    for u, (d, h) in enumerate(units):
        ja, jb, je = h + 2 * H * d, H + h + 2 * H * d, 4 * H + h + H * d
        a_col, a_row = g[d][:, ja:ja + 1], gt[d][ja:ja + 1, :]
        b_row, e_row = gt[d][jb:jb + 1, :], gt[d][je:je + 1, :]
        m_prev = m_ref[u:u + 1, :]
        log_d = jnp.where(mask[d], a_col + b_row, NEG)
        m_inter = b_row + m_prev
        m_t = jnp.maximum(jnp.max(log_d, axis=0, keepdims=True), m_inter)
        lw = e_row + a_row
        m_new = jnp.maximum(e_row + m_prev, jnp.max(lw, axis=-1, keepdims=True))
        m_ref[u:u + 1, :] = m_new
        gate.append(dict(dmat=jnp.exp(log_d - m_t), w_inter=jnp.exp(m_inter - m_t), emt=jnp.exp(-m_t),
                         ws=jnp.exp(lw - m_new), decay=jnp.exp(e_row + m_prev - m_new)))

    ops, pre = [], []
    for u, (d, h) in enumerate(units):
        hs = slice(h * dh, (h + 1) * dh)
        k_u, qt_u, vt_u = k_refs[d][:, hs], qt_refs[d][hs, :], vt_refs[d][hs, :]
        ops.append((k_u, qt_u, vt_u))
        st = jnp.dot(k_u, qt_u, preferred_element_type=F32)
        cq = jnp.dot(ct_ref[u].astype(BF16), qt_u, preferred_element_type=F32)
        n8 = jnp.broadcast_to(n_ref[u:u + 1, :].astype(BF16), (SUBLANES, dh))
        qn = jnp.dot(n8, qt_u, preferred_element_type=F32)[0:1]
        pre.append((st, cq, qn))

    h_t = []
    for u, (d, h) in enumerate(units):
        (k_u, qt_u, vt_u), (st, cq, qn), gq = ops[u], pre[u], gate[u]
        p = st * gq["dmat"]
        den = jnp.sum(p, axis=0, keepdims=True) + gq["w_inter"] * qn
        r = 1.0 / jnp.maximum(jnp.abs(den), gq["emt"])
        num = jnp.dot(vt_u, p.astype(BF16), preferred_element_type=F32) + gq["w_inter"] * cq
        h_t.append(num * r)

    for u, (d, h) in enumerate(units):
        (k_u, qt_u, vt_u), gq = ops[u], gate[u]
        vw = (vt_u.astype(F32) * gq["ws"]).astype(BF16)
        ct_ref[u] = twice(gq["decay"]) * ct_ref[u] + jnp.dot(vw, k_u, preferred_element_type=F32)
        w8 = jnp.broadcast_to(gq["ws"].astype(BF16), (SUBLANES, L))
        n_ref[u:u + 1, :] = (twice(gq["decay"]) * n_ref[u:u + 1, :]
                             + jnp.dot(w8, k_u, preferred_element_type=F32)[0:1])

    chunk = (i, nc - 1 - i)

    @pl.when(i < nc // 2)
    def _park():
        for u, (d, h) in enumerate(units):
            hs_ref[chunk[d], h] = h_t[u]

    @pl.when(i >= nc // 2)
    def _finish():
        for d, (o_ref, y_ref) in enumerate(((of_ref, yhi_ref), (ob_ref, ylo_ref))):
            for h in range(H):
                hs = slice(h * dh, (h + 1) * dh)
                x = h_t[d * H + h] + hs_ref[chunk[d], h]
                mu = jnp.mean(x, axis=0, keepdims=True)
                xc = x - mu
                var = jnp.mean(xc * xc, axis=0, keepdims=True)
                hn = (xc * lax.rsqrt(var + LN_EPS)).T
                y_ref[:, hs] = (jax.nn.sigmoid(o_ref[:, hs].astype(F32)) * hn * gnw_ref[:, hs]).astype(BF16)


def _mlstm_call(k, qt, vt, gates, zmain, gnw):
    B, S, W = k.shape
    L = M_CHUNK
    nc = S // L
    assert nc % 2 == 0
    half = nc // 2
    o_col = 2 * M_WIDTH // W
    fwd, bwd = (lambda i: i), (lambda i: nc - 1 - i)
    second_f, second_b = (lambda i: jnp.maximum(i, half)), (lambda i: jnp.minimum(nc - 1 - i, half - 1))
    rows = lambda w, ch, c=0: pl.BlockSpec((None, L, w), lambda b, i: (b, ch(i), c))
    cols = lambda ch: pl.BlockSpec((None, W, L), lambda b, i: (b, 0, ch(i)))
    return pl.pallas_call(
        _mlstm_kernel,
        grid=(B, nc),
        in_specs=[rows(W, fwd), rows(W, bwd), cols(fwd), cols(bwd), cols(fwd), cols(bwd),
                  rows(LANES, fwd), rows(LANES, bwd), rows(W, second_f, o_col), rows(W, second_b, o_col),
                  pl.BlockSpec((1, W), lambda b, i: (0, 0))],
        out_specs=[pl.BlockSpec((None, L, W), lambda b, i: (b, jnp.maximum(i - half, 0), 0)),
                   pl.BlockSpec((None, L, W), lambda b, i: (b, jnp.minimum(nc - 1 - i, half - 1), 0))],
        out_shape=[jax.ShapeDtypeStruct((B, S // 2, W), BF16)] * 2,
        scratch_shapes=[pltpu.VMEM((nc, M_HEADS, M_HEAD_DIM, L), F32),
                        pltpu.VMEM((2 * M_HEADS, M_HEAD_DIM, M_HEAD_DIM), F32),
                        pltpu.VMEM((2 * M_HEADS, M_HEAD_DIM), F32),
                        pltpu.VMEM((2 * M_HEADS, L), F32)],
        compiler_params=_params("arbitrary", "arbitrary"),
        name="mlstm",
    )(k, k, qt, qt, vt, vt, gates, gates, zmain, zmain, gnw)


ATT_UNROLL = 4


def _attn_kernel(q_ref, k_ref, kp_ref, kn_ref, v_ref, vp_ref, vn_ref, bm_ref, o_ref, lse_ref,
                 *scratch, dil):
    hb = ATT_HALF
    tq = q_ref.shape[1]
    nqb = tq // hb
    i = pl.program_id(1)
    ni = pl.num_programs(1)
    if nqb == 1:
        stage = scratch

        def make_window(main, before, after):
            return lambda r, r0, ls: jnp.concatenate([before[r, :, ls], main[r, :, ls], after[r, :, ls]], axis=0)
    else:
        pads, stage = scratch[:2], scratch[2:]

        def make_window(main, before, after, pad):
            pad[:, 0:hb, :] = before[...]
            pad[:, hb:hb + tq, :] = main[...]
            pad[:, hb + tq:, :] = after[...]
            return lambda r, r0, ls: pad[r, pl.ds(r0, 3 * hb), ls]
    k_window = make_window(k_ref, kp_ref, kn_ref, *(() if nqb == 1 else (pads[0],)))
    v_window = make_window(v_ref, vp_ref, vn_ref, *(() if nqb == 1 else (pads[1],)))
    even = lax.broadcasted_iota(jnp.int32, (hb, LANES), 1) < ATT_HEAD_DIM
    ones = jnp.ones((3 * hb, LANES), BF16)
    if dil > 1:
        os_ref, ls_ref = stage

    npair = ATT_SLOTS // 2
    total = dil * nqb
    step = min(ATT_UNROLL, total)

    def body(it, carry):
        where, scores = [], []
        for j in range(step):
            idx = it * step + j
            r, qb = idx // nqb, idx % nqb
            r0 = pl.multiple_of(qb * hb, hb)
            case = (jnp.logical_and(i == 0, qb == 0).astype(jnp.int32)
                    + 2 * jnp.logical_and(i == ni - 1, qb == nqb - 1).astype(jnp.int32))
            for p in range(npair):
                ls = slice(p * LANES, (p + 1) * LANES)
                q2 = q_ref[r, pl.ds(r0, hb), ls] * ATT_HEAD_DIM ** -0.5
                qs = jnp.concatenate([jnp.where(even, q2, 0), jnp.where(even, 0, q2)], axis=0)
                kw = k_window(r, r0, ls)
                s = lax.dot_general(qs, kw, (((1,), (1,)), ((), ())), preferred_element_type=F32)
                scores.append(s + bm_ref[case, p])
                where.append((r, r0, p, ls))
        s = jnp.concatenate(scores, axis=0)
        m = jnp.max(s, axis=-1, keepdims=True)
        e = jnp.exp(s - m).astype(BF16)
        outs = []
        for u, (r, r0, p, ls) in enumerate(where):
            vw = jnp.concatenate([v_window(r, r0, ls), ones], axis=1)
            outs.append(jnp.dot(e[u * 2 * hb:(u + 1) * 2 * hb], vw, preferred_element_type=F32))
        ol = jnp.concatenate(outs, axis=0)
        l = ol[:, LANES:]
        o = ol[:, :LANES] * (1.0 / l)
        lse = m + jnp.log(l)
        for u, (r, r0, p, ls) in enumerate(where):
            top, bot = slice(u * 2 * hb, u * 2 * hb + hb), slice(u * 2 * hb + hb, (u + 1) * 2 * hb)
            o2 = jnp.where(even, o[top], o[bot])
            l2 = jnp.where(even, lse[top], lse[bot])
            if dil > 1:
                os_ref[r, pl.ds(r0, hb), ls] = o2
                ls_ref[r, pl.ds(r0, hb), ls] = l2
            else:
                o_ref[p, pl.ds(r0, hb), :] = o2
                lse_ref[p, pl.ds(r0, hb), :] = l2
        return carry

    lax.fori_loop(0, total // step, body, 0)
    if dil > 1:
        for r in range(dil):
            for p in range(ATT_SLOTS // 2):
                ls = slice(p * LANES, (p + 1) * LANES)
                o_ref[p, pl.ds(r, tq, stride=dil), :] = os_ref[r, :, ls]
                lse_ref[p, pl.ds(r, tq, stride=dil), :] = ls_ref[r, :, ls]


def _attn_bias(g):
    h = np.arange(1, ATT_HEADS + 1, dtype=np.float32)
    slopes = np.exp2(-8.0 * h / ATT_HEADS).astype(np.float32).reshape(N_PATTERNS, ATT_SLOTS)[g]
    dil = ATT_PATTERNS[g][1]
    hb = ATT_HALF
    qi = np.arange(hb)[:, None] + hb
    kj = np.arange(3 * hb)[None, :]
    dist = np.abs(qi - kj)
    out = np.empty((4, ATT_SLOTS // 2, 2 * hb, 3 * hb), np.float32)
    for case in range(4):
        valid = dist <= hb
        if case & 1:
            valid = valid & (kj >= hb)
        if case & 2:
            valid = valid & (kj < 2 * hb)
        for p in range(ATT_SLOTS // 2):
            for e in range(2):
                bias = -slopes[2 * p + e] * (dil * dist).astype(np.float32)
                out[case, p, e * hb:(e + 1) * hb] = np.where(valid, bias, np.float32(NEG))
    return jnp.asarray(out)


def _attn_call(zq, g, col0):
    B, dil, n, W = zq.shape
    S = n * dil
    hb = ATT_HALF
    tq = min(n, max(1024 // dil, hb))
    cw = ATT_OUT
    qc, kc, vc = (col0 // cw + t for t in range(3))
    rb, nrb = tq // hb, n // hb

    def main(c):
        return pl.BlockSpec((None, dil, tq, cw), lambda b, i: (b, 0, i, c))

    def prev(c):
        return pl.BlockSpec((None, dil, hb, cw), lambda b, i: (b, 0, jnp.maximum(i * rb - 1, 0), c))

    def nxt(c):
        return pl.BlockSpec((None, dil, hb, cw), lambda b, i: (b, 0, jnp.minimum((i + 1) * rb, nrb - 1), c))

    npair = ATT_SLOTS // 2
    out_spec = pl.BlockSpec((None, npair, tq * dil, LANES), lambda b, i: (b, 0, i, 0))
    scratch = [] if tq == hb else [pltpu.VMEM((dil, tq + 2 * hb, cw), BF16)] * 2
    if dil > 1:
        scratch += [pltpu.VMEM((dil, tq, cw), F32), pltpu.VMEM((dil, tq, cw), F32)]
    return pl.pallas_call(
        functools.partial(_attn_kernel, dil=dil),
        grid=(B, n // tq),
        in_specs=[main(qc), main(kc), prev(kc), nxt(kc), main(vc), prev(vc), nxt(vc),
                  pl.BlockSpec((4, ATT_SLOTS // 2, 2 * hb, 3 * hb), lambda b, i: (0, 0, 0, 0))],
        out_specs=[out_spec, out_spec],
        out_shape=[jax.ShapeDtypeStruct((B, npair, S, LANES), F32)] * 2,
        scratch_shapes=scratch,
        compiler_params=_params("arbitrary", "arbitrary"),
        name=f"attn{g}",
    )(zq, zq, zq, zq, zq, zq, zq, _attn_bias(g))


def _mix_kernel(yhi_ref, ylo_ref, o0_ref, o1_ref, o2_ref, l0_ref, l1_ref, l2_ref, ga_ref, gb_ref,
                h_ref, wa_ref, wb_ref, wo_ref, ada_ref, lng_ref, lnb_ref, hn_ref, u_ref):
    upper = pl.program_id(1) >= pl.num_programs(1) // 2
    ym = jnp.where(upper, yhi_ref[...], ylo_ref[...])
    pairs = []
    for p in range(ATT_SLOTS // 2):
        l0, l1, l2 = l0_ref[p], l1_ref[p], l2_ref[p]
        lm = jnp.maximum(jnp.maximum(l0, l1), l2)
        e0, e1, e2 = jnp.exp(l0 - lm), jnp.exp(l1 - lm), jnp.exp(l2 - lm)
        pairs.append((e0 * o0_ref[p] + e1 * o1_ref[p] + e2 * o2_ref[p]) * (1.0 / (e0 + e1 + e2)))
    att = jnp.concatenate(pairs, axis=-1)
    y_a = jnp.dot(ym, wa_ref[...], preferred_element_type=F32)
    y_b = jnp.dot(att.astype(BF16), wb_ref[...], preferred_element_type=F32)
    mix = (jax.nn.sigmoid(ga_ref[...].astype(F32)) * y_a
           + jax.nn.sigmoid(gb_ref[...].astype(F32)) * y_b)
    y = jnp.dot(mix.astype(BF16), wo_ref[...], preferred_element_type=F32)
    hn = _layer_norm_rows(DN_ALPHA * h_ref[...] + ada_ref[2:3, :] * y) * lng_ref[...] + lnb_ref[...]
    hn_ref[...] = hn
    u_ref[...] = (hn * (1.0 + ada_ref[4:5, :]) + ada_ref[3:4, :]).astype(BF16)


def _mix_call(y_hi, y_lo, att, zmain, h, wa, wb, wo, ada_l, lng, lnb, tm=256):
    B, S, D = h.shape
    gcol0 = (2 * M_WIDTH + M_WIDTH) // D
    half = S // tm // 2
    row = lambda w, c=0: pl.BlockSpec((None, tm, w), lambda b, i: (b, i, c))
    full = lambda a: pl.BlockSpec(a.shape, lambda b, i: (0,) * a.ndim)
    (o0, l0), (o1, l1), (o2, l2) = att
    return pl.pallas_call(
        _mix_kernel,
        grid=(B, S // tm),
        in_specs=[pl.BlockSpec((None, tm, D), lambda b, i: (b, jnp.maximum(i - half, 0), 0)),
                  pl.BlockSpec((None, tm, D), lambda b, i: (b, jnp.minimum(i, half - 1), 0))]
                 + [pl.BlockSpec((None, ATT_SLOTS // 2, tm, LANES), lambda b, i: (b, 0, i, 0))] * 6
                 + [row(D, gcol0), row(D, gcol0 + 1), row(D),
                  full(wa), full(wb), full(wo),
                  pl.BlockSpec((None, SUBLANES, D), lambda b, i: (b, 0, 0)), full(lng), full(lnb)],
        out_specs=[row(D), row(D)],
        out_shape=[jax.ShapeDtypeStruct((B, S, D), F32), jax.ShapeDtypeStruct((B, S, D), BF16)],
        compiler_params=_params("arbitrary", "arbitrary"),
        name="mix",
    )(y_hi, y_lo, o0, o1, o2, l0, l1, l2, zmain, zmain, h, wa, wb, wo, ada_l, lng, lnb)


def _ffn_kernel(u_ref, h_ref, w1_ref, w3_ref, w2_ref, ada_ref, lng_ref, lnb_ref, adan_ref,
                hn_ref, *rest, emit_u):
    acc_ref = rest[-1]
    u = u_ref[...]
    for f in range(w1_ref.shape[1] // FF_CHUNK):
        fs = slice(f * FF_CHUNK, (f + 1) * FF_CHUNK)
        a = jnp.dot(u, w1_ref[:, fs], preferred_element_type=F32)
        b = jnp.dot(u, w3_ref[:, fs], preferred_element_type=F32)
        t = (a * jax.nn.sigmoid(a) * b).astype(BF16)
        part = jnp.dot(t, w2_ref[fs, :], preferred_element_type=F32)
        if f == 0:
            acc_ref[...] = part
        else:
            acc_ref[...] += part
    hn = (_layer_norm_rows(DN_ALPHA * h_ref[...] + ada_ref[5:6, :] * acc_ref[...])
          * lng_ref[...] + lnb_ref[...])
    hn_ref[...] = hn
    if emit_u:
        rest[0][...] = (hn * (1.0 + adan_ref[1:2, :]) + adan_ref[0:1, :]).astype(BF16)


def _ffn_call(u, h, w1, w3, w2, ada_l, lng, lnb, ada_next, emit_u, tm=512):
    B, S, D = h.shape
    row = pl.BlockSpec((None, tm, D), lambda b, i: (b, i, 0))
    full = lambda a: pl.BlockSpec(a.shape, lambda b, i: (0,) * a.ndim)
    ada_spec = pl.BlockSpec((None, SUBLANES, D), lambda b, i: (b, 0, 0))
    out_specs = [row, row] if emit_u else [row]
    out_shape = [jax.ShapeDtypeStruct((B, S, D), F32)]
    if emit_u:
        out_shape.append(jax.ShapeDtypeStruct((B, S, D), BF16))
    return pl.pallas_call(
        functools.partial(_ffn_kernel, emit_u=emit_u),
        grid=(B, S // tm),
        in_specs=[row, row, full(w1), full(w3), full(w2), ada_spec, full(lng), full(lnb), ada_spec],
        out_specs=out_specs, out_shape=out_shape,
        scratch_shapes=[pltpu.VMEM((tm, D), F32)],
        compiler_params=_params("arbitrary", "arbitrary"),
        name="ffn",
    )(u, h, w1, w3, w2, ada_l, lng, lnb, ada_next)


def kernel(x, c, w_in, b_gates, conv_w, conv_b, gn_w, w_a, w_b, w_out, w_ada, b_ada, ln1_g, ln1_b,
           w1, w3, w2, ln2_g, ln2_b):
    B, S, D = x.shape
    depth = w_in.shape[0]
    T = B * S
    H = M_HEADS

    c_pad = jnp.zeros((SUBLANES, D), F32).at[:B].set(c)
    ada = _ada_call(c_pad, w_ada, b_ada)
    ada = jnp.pad(ada[:, :, :B].transpose(0, 2, 1, 3), ((0, 0), (0, 0), (0, 2), (0, 0)))

    s0, s1, s2, s3, s4, s5 = np.cumsum((0,) + IN_SECTIONS)[:6].tolist()
    h, u = _ln0_call(x, ada[0])
    for l in range(depth):
        wl = w_in[l]
        wmain = jnp.concatenate([wl[:, s0:s1], wl[:, s2:s3], wl[:, s5:]], axis=1).astype(BF16)
        att_w = [jnp.concatenate([wl[:, s4 + t * ATT_WIDTH + g * ATT_OUT:s4 + t * ATT_WIDTH + (g + 1) * ATT_OUT]
                                  for t in range(3)], axis=1).astype(BF16) for g in range(N_PATTERNS)]
        wvt = wl[:, s1:s2].T.astype(BF16)
        wg = jnp.pad(wl[:, s3:s4], ((0, 0), (0, LANES - 4 * H))).astype(BF16)
        bg = jnp.pad(b_gates[l], (0, LANES - 4 * H)).reshape(1, LANES)

        u2d = u.reshape(T, D)
        zmain = _proj_call(u2d, wmain, BF16, 1024, 1024, "proj_main").reshape(B, S, ZMAIN_WIDTH)
        vt = _proj_t_call(u, wvt, 512, "proj_vT")
        zq = [_proj_call(u2d, att_w[0], BF16, 1024, 512, "projb0").reshape(B, 1, S, 3 * ATT_OUT)] + [
            _proj_dil_call(u, att_w[g], ATT_PATTERNS[g][1], 1024, f"projb{g}") for g in range(1, N_PATTERNS)]
        gates = _gate_call(u2d, wg, bg).reshape(B, S, LANES)

        cw8 = jnp.pad(conv_w[l], ((0, SUBLANES - CONV_WIDTH), (0, 0)))
        cb = conv_b[l].reshape(1, 2 * M_WIDTH)
        qt = _conv_call(zmain, cw8, cb, 0, M_HEAD_DIM ** -0.5, True)
        k = _conv_call(zmain, cw8, cb, M_WIDTH, 1.0, False)
        y_hi, y_lo = _mlstm_call(k, qt, vt, gates, zmain, gn_w[l].reshape(1, M_WIDTH))

        att = [_attn_call(zq[g], g, 0) for g in range(N_PATTERNS)]
        h, u = _mix_call(y_hi, y_lo, att, zmain, h, w_a[l].astype(BF16), w_b[l].astype(BF16),
                         w_out[l].astype(BF16), ada[l], ln1_g[l].reshape(1, D), ln1_b[l].reshape(1, D))
        last = l == depth - 1
        outs = _ffn_call(u, h, w1[l].astype(BF16), w3[l].astype(BF16), w2[l].astype(BF16), ada[l],
                         ln2_g[l].reshape(1, D), ln2_b[l].reshape(1, D),
                         ada[l] if last else ada[l + 1], not last)
        h = outs[0]
        if not last:
            u = outs[1]
    return h
```

```python
import functools

import jax
import jax.numpy as jnp
import numpy as np
from jax import lax
from jax.experimental import pallas as pl
from jax.experimental.pallas import tpu as pltpu

F32 = jnp.float32
BF16 = jnp.bfloat16

D_MODEL = 1024
DEPTH = 2
M_HEADS = 4
M_HEAD_DIM = 256
M_WIDTH = M_HEADS * M_HEAD_DIM
M_CHUNK = 128
CONV_WIDTH = 5
ATT_SLOTS = 8
ATT_HEAD_DIM = 64
ATT_PATTERNS = ((128, 1), (512, 4), (2048, 16))
N_PATTERNS = 3
ATT_HEADS = ATT_SLOTS * N_PATTERNS
ATT_WIDTH = ATT_HEADS * ATT_HEAD_DIM
ATT_OUT = ATT_SLOTS * ATT_HEAD_DIM
ATT_HALF = 64
D_FF = -(-8 * D_MODEL // (3 * 256)) * 256
DN_ALPHA = (2 * DEPTH) ** 0.25
LN_EPS = 1e-5
NEG = -1e30
IN_SECTIONS = (2 * M_WIDTH, M_WIDTH, M_WIDTH, 4 * M_HEADS, 3 * ATT_WIDTH, 2 * D_MODEL)

VMEM_LIMIT_BYTES = 56 * 1024 * 1024
LANES = 128
SUBLANES = 8

LOG2E = 1.4426950408889634
FF_CHUNK = 256
CONV_ROWS = 128


def _params(*sem):
    return pltpu.CompilerParams(dimension_semantics=sem, vmem_limit_bytes=VMEM_LIMIT_BYTES)


def _layer_norm_rows(x):
    mu = jnp.mean(x, axis=-1, keepdims=True)
    xc = x - mu
    var = jnp.mean(xc * xc, axis=-1, keepdims=True)
    return xc * lax.rsqrt(var + LN_EPS)


def _ada_kernel(c_ref, w_ref, b_ref, o_ref):
    c = c_ref[...]
    ca = c * jax.nn.sigmoid(c)
    o_ref[...] = jnp.dot(ca, w_ref[...], precision=lax.Precision.HIGHEST,
                         preferred_element_type=F32) + b_ref[...]


def _ada_call(c_pad, w_ada, b_ada):
    depth = w_ada.shape[0]
    return pl.pallas_call(
        _ada_kernel,
        grid=(depth, 6),
        in_specs=[pl.BlockSpec((SUBLANES, D_MODEL), lambda l, k: (0, 0)),
                  pl.BlockSpec((None, D_MODEL, D_MODEL), lambda l, k: (l, 0, k)),
                  pl.BlockSpec((None, 1, D_MODEL), lambda l, k: (l, 0, k))],
        out_specs=pl.BlockSpec((None, None, SUBLANES, D_MODEL), lambda l, k: (l, k, 0, 0)),
        out_shape=jax.ShapeDtypeStruct((depth, 6, SUBLANES, D_MODEL), F32),
        compiler_params=_params("arbitrary", "arbitrary"),
        name="ada",
    )(c_pad, w_ada, b_ada.reshape(depth, 1, 6 * D_MODEL))


def _ln0_kernel(x_ref, ada_ref, h_ref, u_ref):
    y = _layer_norm_rows(x_ref[...])
    h_ref[...] = y
    u_ref[...] = (y * (1.0 + ada_ref[1:2, :]) + ada_ref[0:1, :]).astype(BF16)


def _ln0_call(x, ada0, ts=512):
    B, S, D = x.shape
    row = pl.BlockSpec((None, ts, D), lambda b, i: (b, i, 0))
    return pl.pallas_call(
        _ln0_kernel,
        grid=(B, S // ts),
        in_specs=[row, pl.BlockSpec((None, SUBLANES, D), lambda b, i: (b, 0, 0))],
        out_specs=[row, row],
        out_shape=[jax.ShapeDtypeStruct((B, S, D), F32), jax.ShapeDtypeStruct((B, S, D), BF16)],
        compiler_params=_params("arbitrary", "arbitrary"),
        name="ln0",
    )(x, ada0)


def _proj_kernel(u_ref, w_ref, o_ref):
    o_ref[...] = jnp.dot(u_ref[...], w_ref[...], preferred_element_type=F32).astype(o_ref.dtype)


def _proj_call(u2d, w, out_dtype, tm, tn, name):
    T, D = u2d.shape
    N = w.shape[1]
    return pl.pallas_call(
        _proj_kernel,
        grid=(T // tm, N // tn),
        in_specs=[pl.BlockSpec((tm, D), lambda i, j: (i, 0)),
                  pl.BlockSpec((D, tn), lambda i, j: (0, j))],
        out_specs=pl.BlockSpec((tm, tn), lambda i, j: (i, j)),
        out_shape=jax.ShapeDtypeStruct((T, N), out_dtype),
        compiler_params=_params("arbitrary", "arbitrary"),
        name=name,
    )(u2d, w)


def _proj_t_kernel(wt_ref, u_ref, o_ref):
    o_ref[...] = lax.dot_general(wt_ref[...], u_ref[...], (((1,), (1,)), ((), ())),
                                 preferred_element_type=F32).astype(o_ref.dtype)


def _proj_t_call(u, wt, tm, name):
    B, S, D = u.shape
    N = wt.shape[0]
    return pl.pallas_call(
        _proj_t_kernel,
        grid=(B, S // tm),
        in_specs=[pl.BlockSpec((N, D), lambda b, i: (0, 0)),
                  pl.BlockSpec((None, tm, D), lambda b, i: (b, i, 0))],
        out_specs=pl.BlockSpec((None, N, tm), lambda b, i: (b, 0, i)),
        out_shape=jax.ShapeDtypeStruct((B, N, S), BF16),
        compiler_params=_params("arbitrary", "arbitrary"),
        name=name,
    )(wt, u)


def _proj_dil_kernel(u_ref, w_ref, o_ref, acc_ref, *, dil):
    u = u_ref[...]
    rows = u_ref.shape[0] // dil
    gw = 2 * LANES
    for j in range(w_ref.shape[1] // gw):
        acc = jnp.dot(u, w_ref[:, j * gw:(j + 1) * gw], preferred_element_type=F32)
        for c in range(gw // LANES):
            acc_ref[j, c] = acc[:, c * LANES:(c + 1) * LANES]
        for r in range(dil):
            for c in range(gw // LANES):
                o_ref[r, :, j * gw + c * LANES:j * gw + (c + 1) * LANES] = (
                    acc_ref[j, c, pl.ds(r, rows, stride=dil), :].astype(o_ref.dtype))


def _proj_dil_call(u, w, dil, tm, name):
    B, S, D = u.shape
    N = w.shape[1]
    return pl.pallas_call(
        functools.partial(_proj_dil_kernel, dil=dil),
        grid=(B, S // tm),
        in_specs=[pl.BlockSpec((None, tm, D), lambda b, i: (b, i, 0)),
                  pl.BlockSpec((D, N), lambda b, i: (0, 0))],
        out_specs=pl.BlockSpec((None, dil, tm // dil, N), lambda b, i: (b, 0, i, 0)),
        out_shape=jax.ShapeDtypeStruct((B, dil, S // dil, N), BF16),
        scratch_shapes=[pltpu.VMEM((N // (2 * LANES), 2, tm, LANES), F32)],
        compiler_params=_params("arbitrary", "arbitrary"),
        name=name,
    )(u, w)


def _gate_kernel(u_ref, w_ref, b_ref, o_ref):
    g = jnp.dot(u_ref[...], w_ref[...], preferred_element_type=F32) + b_ref[...]
    lf = jnp.minimum(g, 0.0) - jnp.log1p(jnp.exp(-jnp.abs(g)))
    row = lax.broadcasted_iota(jnp.int32, (M_CHUNK, LANES), 0)
    lane = lax.broadcasted_iota(jnp.int32, (M_CHUNK, LANES), 1)
    H = M_HEADS
    for c in range(u_ref.shape[0] // M_CHUNK):
        sl = slice(c * M_CHUNK, (c + 1) * M_CHUNK)
        gc, pre, suf = g[sl], lf[sl], lf[sl]
        d = 1
        while d < M_CHUNK:
            pre = pre + jnp.where(row >= d, pltpu.roll(pre, d, 0), 0.0)
            suf = suf + jnp.where(row < M_CHUNK - d, pltpu.roll(suf, M_CHUNK - d, 0), 0.0)
            d *= 2
        a_f = gc - pltpu.roll(pre, LANES - H, 1)
        a_b = gc - pltpu.roll(suf, LANES - H, 1)
        e_f = jnp.broadcast_to(pltpu.roll(pre, 3 * H, 1)[M_CHUNK - 1:M_CHUNK], pre.shape)
        e_b = jnp.broadcast_to(pltpu.roll(suf, 2 * H, 1)[0:1], suf.shape)
        o_ref[sl, :] = jnp.where(lane < H, a_f, jnp.where(lane < 2 * H, pre,
                                 jnp.where(lane < 3 * H, a_b, jnp.where(lane < 4 * H, suf,
                                           jnp.where(lane < 5 * H, e_f, e_b)))))


def _gate_call(u2d, wg, bg, tm=512):
    T, D = u2d.shape
    return pl.pallas_call(
        _gate_kernel,
        grid=(T // tm,),
        in_specs=[pl.BlockSpec((tm, D), lambda i: (i, 0)),
                  pl.BlockSpec((D, LANES), lambda i: (0, 0)),
                  pl.BlockSpec((1, LANES), lambda i: (0, 0))],
        out_specs=pl.BlockSpec((tm, LANES), lambda i: (i, 0)),
        out_shape=jax.ShapeDtypeStruct((T, LANES), F32),
        compiler_params=_params("arbitrary"),
        name="gates",
    )(u2d, wg, bg)


def _proj_conv_kernel(u_ref, up_ref, un_ref, wc_ref, wo_ref, cw_ref, cb_ref, y_ref, z_ref, *,
                      scale, transpose_out):
    i = pl.program_id(1)
    tm = u_ref.shape[0]
    hr = up_ref.shape[0]
    before = jnp.where(i > 0, up_ref[...], jnp.zeros_like(up_ref))
    after = jnp.where(i < pl.num_programs(1) - 1, un_ref[...], jnp.zeros_like(un_ref))
    lhs = jnp.concatenate([before, u_ref[...], after], axis=0)
    x = jnp.dot(lhs, wc_ref[...], preferred_element_type=F32)
    z_ref[...] = jnp.dot(u_ref[...], wo_ref[...], preferred_element_type=F32).astype(z_ref.dtype)
    rt, ap = CONV_ROWS, SUBLANES
    for c in range(wc_ref.shape[1] // LANES):
        cs = slice(c * LANES, (c + 1) * LANES)
        for rb in range(tm // rt):
            r0 = hr + rb * rt
            blk = x[r0 - ap:r0 + rt + ap, cs]
            y = cb_ref[:, cs]
            for j in range(CONV_WIDTH):
                k = j - CONV_WIDTH // 2
                xs = blk if k == 0 else pltpu.roll(blk, (-k) % (rt + 2 * ap), 0)
                y = y + cw_ref[j:j + 1, cs] * xs[ap:ap + rt]
            y = y * jax.nn.sigmoid(y)
            if scale != 1.0:
                y = y * scale
            if transpose_out:
                y_ref[cs, rb * rt:(rb + 1) * rt] = y.T.astype(BF16)
            else:
                y_ref[rb * rt:(rb + 1) * rt, cs] = y.astype(BF16)


def _proj_conv_call(u, wc, wo, conv_w8, conv_b, col0, scale, transpose_out, name, tm=1024, tc=512):
    B, S, D = u.shape
    nj = M_WIDTH // tc
    to = wo.shape[1] // nj
    hr = 2 * SUBLANES
    rb, nrb = tm // hr, S // hr
    c0 = col0 // tc
    if transpose_out:
        y_spec = pl.BlockSpec((None, tc, tm), lambda b, i, j: (b, j, i))
        y_shape = jax.ShapeDtypeStruct((B, M_WIDTH, S), BF16)
    else:
        y_spec = pl.BlockSpec((None, tm, tc), lambda b, i, j: (b, i, j))
        y_shape = jax.ShapeDtypeStruct((B, S, M_WIDTH), BF16)
    return pl.pallas_call(
        functools.partial(_proj_conv_kernel, scale=scale, transpose_out=transpose_out),
        grid=(B, S // tm, nj),
        in_specs=[pl.BlockSpec((None, tm, D), lambda b, i, j: (b, i, 0)),
                  pl.BlockSpec((None, hr, D), lambda b, i, j: (b, jnp.maximum(i * rb - 1, 0), 0)),
                  pl.BlockSpec((None, hr, D), lambda b, i, j: (b, jnp.minimum((i + 1) * rb, nrb - 1), 0)),
                  pl.BlockSpec((D, tc), lambda b, i, j: (0, j)),
                  pl.BlockSpec((D, to), lambda b, i, j: (0, j)),
                  pl.BlockSpec((SUBLANES, tc), lambda b, i, j: (0, c0 + j)),
                  pl.BlockSpec((1, tc), lambda b, i, j: (0, c0 + j))],
        out_specs=[y_spec, pl.BlockSpec((None, tm, to), lambda b, i, j: (b, i, j))],
        out_shape=[y_shape, jax.ShapeDtypeStruct((B, S, wo.shape[1]), BF16)],
        compiler_params=_params("arbitrary", "arbitrary", "arbitrary"),
        name=name,
    )(u, u, u, wc, wo, conv_w8, conv_b)


def _mlstm_kernel(kf_ref, kb_ref, qtf_ref, qtb_ref, vtf_ref, vtb_ref, gf_ref, gb_ref, of_ref, ob_ref,
                  gnw_ref, yhi_ref, ylo_ref, hs_ref, ct_ref, n_ref, m_ref):
    L, H, dh = M_CHUNK, M_HEADS, M_HEAD_DIM
    i = pl.program_id(1)
    nc = pl.num_programs(1)

    @pl.when(i == 0)
    def _init():
        ct_ref[...] = jnp.zeros_like(ct_ref)
        n_ref[...] = jnp.zeros_like(n_ref)
        m_ref[...] = jnp.full(m_ref.shape, NEG, F32)

    row = lax.broadcasted_iota(jnp.int32, (L, L), 0)
    col = lax.broadcasted_iota(jnp.int32, (L, L), 1)
    mask = (row <= col, row >= col)
    k_refs, qt_refs, vt_refs = (kf_ref, kb_ref), (qtf_ref, qtb_ref), (vtf_ref, vtb_ref)
    g = (gf_ref[...], gb_ref[...])
    gt = (g[0].T, g[1].T)
    units = [(d, h) for d in range(2) for h in range(H)]
    twice = lambda r: jnp.concatenate([r, r], axis=1)

    gate = []
    for u, (d, h) in enumerate(units):
        ja, jb, je = h + 2 * H * d, H + h + 2 * H * d, 4 * H + h + H * d
        a_col, a_row = g[d][:, ja:ja + 1], gt[d][ja:ja + 1, :]
        b_row, e_row = gt[d][jb:jb + 1, :], gt[d][je:je + 1, :]
        m_prev = m_ref[u:u + 1, :]
        log_d = jnp.where(mask[d], a_col + b_row, NEG)
        m_inter = b_row + m_prev
        m_t = jnp.maximum(jnp.max(log_d, axis=0, keepdims=True), m_inter)
        lw = e_row + a_row
        m_new = jnp.maximum(e_row + m_prev, jnp.max(lw, axis=-1, keepdims=True))
        m_ref[u:u + 1, :] = m_new
        gate.append(dict(dmat=jnp.exp(log_d - m_t), w_inter=jnp.exp(m_inter - m_t), emt=jnp.exp(-m_t),
                         ws=jnp.exp(lw - m_new), decay=jnp.exp(e_row + m_prev - m_new)))

    ops, pre = [], []
    for u, (d, h) in enumerate(units):
        hs = slice(h * dh, (h + 1) * dh)
        k_u, qt_u, vt_u = k_refs[d][:, hs], qt_refs[d][hs, :], vt_refs[d][hs, :]
        ops.append((k_u, qt_u, vt_u))
        st = jnp.dot(k_u, qt_u, preferred_element_type=F32)
        cq = jnp.dot(ct_ref[u].astype(BF16), qt_u, preferred_element_type=F32)
        n8 = jnp.broadcast_to(n_ref[u:u + 1, :].astype(BF16), (SUBLANES, dh))
        qn = jnp.dot(n8, qt_u, preferred_element_type=F32)[0:1]
        pre.append((st, cq, qn))

    h_t = []
    for u, (d, h) in enumerate(units):
        (k_u, qt_u, vt_u), (st, cq, qn), gq = ops[u], pre[u], gate[u]
        p = st * gq["dmat"]
        den = jnp.sum(p, axis=0, keepdims=True) + gq["w_inter"] * qn
        r = 1.0 / jnp.maximum(jnp.abs(den), gq["emt"])
        num = jnp.dot(vt_u, p.astype(BF16), preferred_element_type=F32) + gq["w_inter"] * cq
        h_t.append(num * r)

    for u, (d, h) in enumerate(units):
        (k_u, qt_u, vt_u), gq = ops[u], gate[u]
        vw = (vt_u.astype(F32) * gq["ws"]).astype(BF16)
        ct_ref[u] = twice(gq["decay"]) * ct_ref[u] + jnp.dot(vw, k_u, preferred_element_type=F32)
        w8 = jnp.broadcast_to(gq["ws"].astype(BF16), (SUBLANES, L))
        n_ref[u:u + 1, :] = (twice(gq["decay"]) * n_ref[u:u + 1, :]
                             + jnp.dot(w8, k_u, preferred_element_type=F32)[0:1])

    chunk = (i, nc - 1 - i)

    @pl.when(i < nc // 2)
    def _park():
        for u, (d, h) in enumerate(units):
            hs_ref[chunk[d], h] = h_t[u]

    @pl.when(i >= nc // 2)
    def _finish():
        for d, (o_ref, y_ref) in enumerate(((of_ref, yhi_ref), (ob_ref, ylo_ref))):
            for h in range(H):
                hs = slice(h * dh, (h + 1) * dh)
                x = h_t[d * H + h] + hs_ref[chunk[d], h]
                mu = jnp.mean(x, axis=0, keepdims=True)
                xc = x - mu
                var = jnp.mean(xc * xc, axis=0, keepdims=True)
                hn = (xc * lax.rsqrt(var + LN_EPS)).T
                y_ref[:, hs] = (jax.nn.sigmoid(o_ref[:, hs].astype(F32)) * hn * gnw_ref[:, hs]).astype(BF16)


def _mlstm_call(k, qt, vt, gates, og, o_col, gnw):
    B, S, W = k.shape
    L = M_CHUNK
    nc = S // L
    assert nc % 2 == 0
    half = nc // 2
    fwd, bwd = (lambda i: i), (lambda i: nc - 1 - i)
    second_f, second_b = (lambda i: jnp.maximum(i, half)), (lambda i: jnp.minimum(nc - 1 - i, half - 1))
    rows = lambda w, ch, c=0: pl.BlockSpec((None, L, w), lambda b, i: (b, ch(i), c))
    cols = lambda ch: pl.BlockSpec((None, W, L), lambda b, i: (b, 0, ch(i)))
    return pl.pallas_call(
        _mlstm_kernel,
        grid=(B, nc),
        in_specs=[rows(W, fwd), rows(W, bwd), cols(fwd), cols(bwd), cols(fwd), cols(bwd),
                  rows(LANES, fwd), rows(LANES, bwd), rows(W, second_f, o_col), rows(W, second_b, o_col),
                  pl.BlockSpec((1, W), lambda b, i: (0, 0))],
        out_specs=[pl.BlockSpec((None, L, W), lambda b, i: (b, jnp.maximum(i - half, 0), 0)),
                   pl.BlockSpec((None, L, W), lambda b, i: (b, jnp.minimum(nc - 1 - i, half - 1), 0))],
        out_shape=[jax.ShapeDtypeStruct((B, S // 2, W), BF16)] * 2,
        scratch_shapes=[pltpu.VMEM((nc, M_HEADS, M_HEAD_DIM, L), F32),
                        pltpu.VMEM((2 * M_HEADS, M_HEAD_DIM, M_HEAD_DIM), F32),
                        pltpu.VMEM((2 * M_HEADS, M_HEAD_DIM), F32),
                        pltpu.VMEM((2 * M_HEADS, L), F32)],
        compiler_params=_params("arbitrary", "arbitrary"),
        name="mlstm",
    )(k, k, qt, qt, vt, vt, gates, gates, og, og, gnw)


ATT_UNROLL = 4


def _attn_kernel(q_ref, k_ref, kp_ref, kn_ref, v_ref, vp_ref, vn_ref, bm_ref, o_ref, lse_ref,
                 *scratch, dil):
    hb = ATT_HALF
    tq = q_ref.shape[1]
    nqb = tq // hb
    i = pl.program_id(1)
    ni = pl.num_programs(1)
    if nqb == 1:
        stage = scratch

        def make_window(main, before, after):
            return lambda r, r0, ls: jnp.concatenate([before[r, :, ls], main[r, :, ls], after[r, :, ls]], axis=0)
    else:
        pads, stage = scratch[:2], scratch[2:]

        def make_window(main, before, after, pad):
            pad[:, 0:hb, :] = before[...]
            pad[:, hb:hb + tq, :] = main[...]
            pad[:, hb + tq:, :] = after[...]
            return lambda r, r0, ls: pad[r, pl.ds(r0, 3 * hb), ls]
    k_window = make_window(k_ref, kp_ref, kn_ref, *(() if nqb == 1 else (pads[0],)))
    v_window = make_window(v_ref, vp_ref, vn_ref, *(() if nqb == 1 else (pads[1],)))
    even = lax.broadcasted_iota(jnp.int32, (hb, LANES), 1) < ATT_HEAD_DIM
    ones = jnp.ones((3 * hb, LANES), BF16)
    if dil > 1:
        os_ref, ls_ref = stage

    npair = ATT_SLOTS // 2
    total = dil * nqb
    step = min(ATT_UNROLL, total)

    def body(it, carry):
        where, scores = [], []
        for j in range(step):
            idx = it * step + j
            r, qb = idx // nqb, idx % nqb
            r0 = pl.multiple_of(qb * hb, hb)
            case = (jnp.logical_and(i == 0, qb == 0).astype(jnp.int32)
                    + 2 * jnp.logical_and(i == ni - 1, qb == nqb - 1).astype(jnp.int32))
            for p in range(npair):
                ls = slice(p * LANES, (p + 1) * LANES)
                q2 = q_ref[r, pl.ds(r0, hb), ls]
                qs = jnp.concatenate([jnp.where(even, q2, 0), jnp.where(even, 0, q2)], axis=0)
                kw = k_window(r, r0, ls)
                s = lax.dot_general(qs, kw, (((1,), (1,)), ((), ())), preferred_element_type=F32)
                scores.append(s + bm_ref[case, p])
                where.append((r, r0, p, ls))
        s = jnp.concatenate(scores, axis=0)
        m = jnp.max(s, axis=-1, keepdims=True)
        e = jnp.exp2(s - m).astype(BF16)
        outs = []
        for u, (r, r0, p, ls) in enumerate(where):
            vw = jnp.concatenate([v_window(r, r0, ls), ones], axis=1)
            outs.append(jnp.dot(e[u * 2 * hb:(u + 1) * 2 * hb], vw, preferred_element_type=F32))
        ol = jnp.concatenate(outs, axis=0)
        l = ol[:, LANES:]
        o = ol[:, :LANES] * (1.0 / l)
        lse = m + jnp.log2(l)
        for u, (r, r0, p, ls) in enumerate(where):
            top, bot = slice(u * 2 * hb, u * 2 * hb + hb), slice(u * 2 * hb + hb, (u + 1) * 2 * hb)
            o2 = jnp.where(even, o[top], o[bot])
            l2 = jnp.where(even, lse[top], lse[bot])
            if dil > 1:
                os_ref[r, pl.ds(r0, hb), ls] = o2
                ls_ref[r, pl.ds(r0, hb), ls] = l2
            else:
                o_ref[p, pl.ds(r0, hb), :] = o2
                lse_ref[p, pl.ds(r0, hb), :] = l2
        return carry

    lax.fori_loop(0, total // step, body, 0)
    if dil > 1:
        for r in range(dil):
            for p in range(ATT_SLOTS // 2):
                ls = slice(p * LANES, (p + 1) * LANES)
                o_ref[p, pl.ds(r, tq, stride=dil), :] = os_ref[r, :, ls]
                lse_ref[p, pl.ds(r, tq, stride=dil), :] = ls_ref[r, :, ls]


def _attn_bias(g):
    h = np.arange(1, ATT_HEADS + 1, dtype=np.float32)
    slopes = np.exp2(-8.0 * h / ATT_HEADS).astype(np.float32).reshape(N_PATTERNS, ATT_SLOTS)[g]
    dil = ATT_PATTERNS[g][1]
    hb = ATT_HALF
    qi = np.arange(hb)[:, None] + hb
    kj = np.arange(3 * hb)[None, :]
    dist = np.abs(qi - kj)
    out = np.empty((4, ATT_SLOTS // 2, 2 * hb, 3 * hb), np.float32)
    for case in range(4):
        valid = dist <= hb
        if case & 1:
            valid = valid & (kj >= hb)
        if case & 2:
            valid = valid & (kj < 2 * hb)
        for p in range(ATT_SLOTS // 2):
            for e in range(2):
                bias = -slopes[2 * p + e] * (dil * dist).astype(np.float32) * np.float32(LOG2E)
                out[case, p, e * hb:(e + 1) * hb] = np.where(valid, bias, np.float32(NEG))
    return jnp.asarray(out)


def _attn_call(zq, g, col0):
    B, dil, n, W = zq.shape
    S = n * dil
    hb = ATT_HALF
    tq = min(n, max(1024 // dil, hb))
    cw = ATT_OUT
    qc, kc, vc = (col0 // cw + t for t in range(3))
    rb, nrb = tq // hb, n // hb

    def main(c):
        return pl.BlockSpec((None, dil, tq, cw), lambda b, i: (b, 0, i, c))

    def prev(c):
        return pl.BlockSpec((None, dil, hb, cw), lambda b, i: (b, 0, jnp.maximum(i * rb - 1, 0), c))

    def nxt(c):
        return pl.BlockSpec((None, dil, hb, cw), lambda b, i: (b, 0, jnp.minimum((i + 1) * rb, nrb - 1), c))

    npair = ATT_SLOTS // 2
    out_spec = pl.BlockSpec((None, npair, tq * dil, LANES), lambda b, i: (b, 0, i, 0))
    scratch = [] if tq == hb else [pltpu.VMEM((dil, tq + 2 * hb, cw), BF16)] * 2
    if dil > 1:
        scratch += [pltpu.VMEM((dil, tq, cw), F32), pltpu.VMEM((dil, tq, cw), F32)]
    return pl.pallas_call(
        functools.partial(_attn_kernel, dil=dil),
        grid=(B, n // tq),
        in_specs=[main(qc), main(kc), prev(kc), nxt(kc), main(vc), prev(vc), nxt(vc),
                  pl.BlockSpec((4, ATT_SLOTS // 2, 2 * hb, 3 * hb), lambda b, i: (0, 0, 0, 0))],
        out_specs=[out_spec, out_spec],
        out_shape=[jax.ShapeDtypeStruct((B, npair, S, LANES), F32)] * 2,
        scratch_shapes=scratch,
        compiler_params=_params("arbitrary", "arbitrary"),
        name=f"attn{g}",
    )(zq, zq, zq, zq, zq, zq, zq, _attn_bias(g))


def _mix_kernel(yhi_ref, ylo_ref, o0_ref, o1_ref, o2_ref, l0_ref, l1_ref, l2_ref, ga_ref, gb_ref,
                h_ref, wa_ref, wb_ref, wo_ref, ada_ref, lng_ref, lnb_ref, hn_ref, u_ref):
    upper = pl.program_id(1) >= pl.num_programs(1) // 2
    ym = jnp.where(upper, yhi_ref[...], ylo_ref[...])
    pairs = []
    for p in range(ATT_SLOTS // 2):
        l0, l1, l2 = l0_ref[p], l1_ref[p], l2_ref[p]
        lm = jnp.maximum(jnp.maximum(l0, l1), l2)
        e0, e1, e2 = jnp.exp2(l0 - lm), jnp.exp2(l1 - lm), jnp.exp2(l2 - lm)
        pairs.append((e0 * o0_ref[p] + e1 * o1_ref[p] + e2 * o2_ref[p]) * (1.0 / (e0 + e1 + e2)))
    att = jnp.concatenate(pairs, axis=-1)
    y_a = jnp.dot(ym, wa_ref[...], preferred_element_type=F32)
    y_b = jnp.dot(att.astype(BF16), wb_ref[...], preferred_element_type=F32)
    mix = (jax.nn.sigmoid(ga_ref[...].astype(F32)) * y_a
           + jax.nn.sigmoid(gb_ref[...].astype(F32)) * y_b)
    y = jnp.dot(mix.astype(BF16), wo_ref[...], preferred_element_type=F32)
    hn = _layer_norm_rows(DN_ALPHA * h_ref[...] + ada_ref[2:3, :] * y) * lng_ref[...] + lnb_ref[...]
    hn_ref[...] = hn
    u_ref[...] = (hn * (1.0 + ada_ref[4:5, :]) + ada_ref[3:4, :]).astype(BF16)


def _mix_call(y_hi, y_lo, att, ga, gb, gb_col, h, wa, wb, wo, ada_l, lng, lnb, tm=256):
    B, S, D = h.shape
    half = S // tm // 2
    row = lambda w, c=0: pl.BlockSpec((None, tm, w), lambda b, i: (b, i, c))
    full = lambda a: pl.BlockSpec(a.shape, lambda b, i: (0,) * a.ndim)
    (o0, l0), (o1, l1), (o2, l2) = att
    return pl.pallas_call(
        _mix_kernel,
        grid=(B, S // tm),
        in_specs=[pl.BlockSpec((None, tm, D), lambda b, i: (b, jnp.maximum(i - half, 0), 0)),
                  pl.BlockSpec((None, tm, D), lambda b, i: (b, jnp.minimum(i, half - 1), 0))]
                 + [pl.BlockSpec((None, ATT_SLOTS // 2, tm, LANES), lambda b, i: (b, 0, i, 0))] * 6
                 + [row(D), row(D, gb_col), row(D),
                  full(wa), full(wb), full(wo),
                  pl.BlockSpec((None, SUBLANES, D), lambda b, i: (b, 0, 0)), full(lng), full(lnb)],
        out_specs=[row(D), row(D)],
        out_shape=[jax.ShapeDtypeStruct((B, S, D), F32), jax.ShapeDtypeStruct((B, S, D), BF16)],
        compiler_params=_params("arbitrary", "arbitrary"),
        name="mix",
    )(y_hi, y_lo, o0, o1, o2, l0, l1, l2, ga, gb, h, wa, wb, wo, ada_l, lng, lnb)


def _ffn_kernel(u_ref, h_ref, w1_ref, w3_ref, w2_ref, ada_ref, lng_ref, lnb_ref, adan_ref,
                hn_ref, *rest, emit_u):
    acc_ref = rest[-1]
    u = u_ref[...]
    for f in range(w1_ref.shape[1] // FF_CHUNK):
        fs = slice(f * FF_CHUNK, (f + 1) * FF_CHUNK)
        a = jnp.dot(u, w1_ref[:, fs], preferred_element_type=F32)
        b = jnp.dot(u, w3_ref[:, fs], preferred_element_type=F32)
        t = (a * jax.nn.sigmoid(a) * b).astype(BF16)
        part = jnp.dot(t, w2_ref[fs, :], preferred_element_type=F32)
        if f == 0:
            acc_ref[...] = part
        else:
            acc_ref[...] += part
    hn = (_layer_norm_rows(DN_ALPHA * h_ref[...] + ada_ref[5:6, :] * acc_ref[...])
          * lng_ref[...] + lnb_ref[...])
    hn_ref[...] = hn
    if emit_u:
        rest[0][...] = (hn * (1.0 + adan_ref[1:2, :]) + adan_ref[0:1, :]).astype(BF16)


def _ffn_call(u, h, w1, w3, w2, ada_l, lng, lnb, ada_next, emit_u, tm=512):
    B, S, D = h.shape
    row = pl.BlockSpec((None, tm, D), lambda b, i: (b, i, 0))
    full = lambda a: pl.BlockSpec(a.shape, lambda b, i: (0,) * a.ndim)
    ada_spec = pl.BlockSpec((None, SUBLANES, D), lambda b, i: (b, 0, 0))
    out_specs = [row, row] if emit_u else [row]
    out_shape = [jax.ShapeDtypeStruct((B, S, D), F32)]
    if emit_u:
        out_shape.append(jax.ShapeDtypeStruct((B, S, D), BF16))
    return pl.pallas_call(
        functools.partial(_ffn_kernel, emit_u=emit_u),
        grid=(B, S // tm),
        in_specs=[row, row, full(w1), full(w3), full(w2), ada_spec, full(lng), full(lnb), ada_spec],
        out_specs=out_specs, out_shape=out_shape,
        scratch_shapes=[pltpu.VMEM((tm, D), F32)],
        compiler_params=_params("arbitrary", "arbitrary"),
        name="ffn",
    )(u, h, w1, w3, w2, ada_l, lng, lnb, ada_next)


def kernel(x, c, w_in, b_gates, conv_w, conv_b, gn_w, w_a, w_b, w_out, w_ada, b_ada, ln1_g, ln1_b,
           w1, w3, w2, ln2_g, ln2_b):
    B, S, D = x.shape
    depth = w_in.shape[0]
    T = B * S
    H = M_HEADS

    c_pad = jnp.zeros((SUBLANES, D), F32).at[:B].set(c)
    ada = _ada_call(c_pad, w_ada, b_ada)
    ada = jnp.pad(ada[:, :, :B].transpose(0, 2, 1, 3), ((0, 0), (0, 0), (0, 2), (0, 0)))

    s0, s1, s2, s3, s4, s5 = np.cumsum((0,) + IN_SECTIONS)[:6].tolist()
    h, u = _ln0_call(x, ada[0])
    for l in range(depth):
        wl = w_in[l]
        q_scale = ATT_HEAD_DIM ** -0.5 * LOG2E
        att_w = [jnp.concatenate(
            [wl[:, s4 + t * ATT_WIDTH + g * ATT_OUT:s4 + t * ATT_WIDTH + (g + 1) * ATT_OUT] * (q_scale if t == 0 else 1.0)
             for t in range(3)], axis=1).astype(BF16) for g in range(N_PATTERNS)]
        w_q, w_k = wl[:, s0:s0 + M_WIDTH].astype(BF16), wl[:, s0 + M_WIDTH:s1].astype(BF16)
        w_ga = wl[:, s5:s5 + D].astype(BF16)
        w_ogb = jnp.concatenate([wl[:, s2:s3], wl[:, s5 + D:]], axis=1).astype(BF16)
        wvt = wl[:, s1:s2].T.astype(BF16)
        wg = jnp.pad(wl[:, s3:s4], ((0, 0), (0, LANES - 4 * H))).astype(BF16)
        bg = jnp.pad(b_gates[l], (0, LANES - 4 * H)).reshape(1, LANES)

        u2d = u.reshape(T, D)
        vt = _proj_t_call(u, wvt, 512, "proj_vT")
        zq = [_proj_call(u2d, att_w[0], BF16, 1024, 3 * ATT_OUT, "projb0").reshape(B, 1, S, 3 * ATT_OUT)] + [
            _proj_dil_call(u, att_w[g], ATT_PATTERNS[g][1], 1024, f"projb{g}") for g in range(1, N_PATTERNS)]
        gates = _gate_call(u2d, wg, bg).reshape(B, S, LANES)

        cw8 = jnp.pad(conv_w[l], ((0, SUBLANES - CONV_WIDTH), (0, 0)))
        cb = conv_b[l].reshape(1, 2 * M_WIDTH)
        qt, ga = _proj_conv_call(u, w_q, w_ga, cw8, cb, 0, M_HEAD_DIM ** -0.5, True, "proj_conv_q")
        k, ogb = _proj_conv_call(u, w_k, w_ogb, cw8, cb, M_WIDTH, 1.0, False, "proj_conv_k")
        y_hi, y_lo = _mlstm_call(k, qt, vt, gates, ogb, 0, gn_w[l].reshape(1, M_WIDTH))

        att = [_attn_call(zq[g], g, 0) for g in range(N_PATTERNS)]
        h, u = _mix_call(y_hi, y_lo, att, ga, ogb, 1, h, w_a[l].astype(BF16), w_b[l].astype(BF16),
                         w_out[l].astype(BF16), ada[l], ln1_g[l].reshape(1, D), ln1_b[l].reshape(1, D))
        last = l == depth - 1
        outs = _ffn_call(u, h, w1[l].astype(BF16), w3[l].astype(BF16), w2[l].astype(BF16), ada[l],
                         ln2_g[l].reshape(1, D), ln2_b[l].reshape(1, D),
                         ada[l] if last else ada[l + 1], not last)
        h = outs[0]
        if not last:
            u = outs[1]
    return h
```

```python
import functools

import jax
import jax.numpy as jnp
import numpy as np
from jax import lax
from jax.experimental import pallas as pl
from jax.experimental.pallas import tpu as pltpu

F32 = jnp.float32
BF16 = jnp.bfloat16

D_MODEL = 1024
DEPTH = 2
M_HEADS = 4
M_HEAD_DIM = 256
M_WIDTH = M_HEADS * M_HEAD_DIM
M_CHUNK = 128
CONV_WIDTH = 5
ATT_SLOTS = 8
ATT_HEAD_DIM = 64
ATT_PATTERNS = ((128, 1), (512, 4), (2048, 16))
N_PATTERNS = 3
ATT_HEADS = ATT_SLOTS * N_PATTERNS
ATT_WIDTH = ATT_HEADS * ATT_HEAD_DIM
ATT_OUT = ATT_SLOTS * ATT_HEAD_DIM
ATT_HALF = 64
D_FF = -(-8 * D_MODEL // (3 * 256)) * 256
DN_ALPHA = (2 * DEPTH) ** 0.25
LN_EPS = 1e-5
NEG = -1e30
IN_SECTIONS = (2 * M_WIDTH, M_WIDTH, M_WIDTH, 4 * M_HEADS, 3 * ATT_WIDTH, 2 * D_MODEL)

VMEM_LIMIT_BYTES = 56 * 1024 * 1024
LANES = 128
SUBLANES = 8

LOG2E = 1.4426950408889634
FF_CHUNK = 256
CONV_ROWS = 128


def _params(*sem):
    return pltpu.CompilerParams(dimension_semantics=sem, vmem_limit_bytes=VMEM_LIMIT_BYTES)


def _layer_norm_rows(x):
    mu = jnp.mean(x, axis=-1, keepdims=True)
    xc = x - mu
    var = jnp.mean(xc * xc, axis=-1, keepdims=True)
    return xc * lax.rsqrt(var + LN_EPS)


def _ada_kernel(c_ref, w_ref, b_ref, o_ref):
    c = c_ref[...]
    ca = c * jax.nn.sigmoid(c)
    o_ref[...] = jnp.dot(ca, w_ref[...], precision=lax.Precision.HIGHEST,
                         preferred_element_type=F32) + b_ref[...]


def _ada_call(c_pad, w_ada, b_ada):
    depth = w_ada.shape[0]
    return pl.pallas_call(
        _ada_kernel,
        grid=(depth, 6),
        in_specs=[pl.BlockSpec((SUBLANES, D_MODEL), lambda l, k: (0, 0)),
                  pl.BlockSpec((None, D_MODEL, D_MODEL), lambda l, k: (l, 0, k)),
                  pl.BlockSpec((None, 1, D_MODEL), lambda l, k: (l, 0, k))],
        out_specs=pl.BlockSpec((None, None, SUBLANES, D_MODEL), lambda l, k: (l, k, 0, 0)),
        out_shape=jax.ShapeDtypeStruct((depth, 6, SUBLANES, D_MODEL), F32),
        compiler_params=_params("arbitrary", "arbitrary"),
        name="ada",
    )(c_pad, w_ada, b_ada.reshape(depth, 1, 6 * D_MODEL))


def _ln0_kernel(x_ref, ada_ref, h_ref, u_ref):
    y = _layer_norm_rows(x_ref[...])
    h_ref[...] = y
    u_ref[...] = (y * (1.0 + ada_ref[1:2, :]) + ada_ref[0:1, :]).astype(BF16)


def _ln0_call(x, ada0, ts=512):
    B, S, D = x.shape
    row = pl.BlockSpec((None, ts, D), lambda b, i: (b, i, 0))
    return pl.pallas_call(
        _ln0_kernel,
        grid=(B, S // ts),
        in_specs=[row, pl.BlockSpec((None, SUBLANES, D), lambda b, i: (b, 0, 0))],
        out_specs=[row, row],
        out_shape=[jax.ShapeDtypeStruct((B, S, D), F32), jax.ShapeDtypeStruct((B, S, D), BF16)],
        compiler_params=_params("arbitrary", "arbitrary"),
        name="ln0",
    )(x, ada0)


W_QK, W_O, W_GA, W_GB, W_ATT = 0, 2 * M_WIDTH, 3 * M_WIDTH, 3 * M_WIDTH + D_MODEL, 3 * M_WIDTH + 2 * D_MODEL
W_GATE = W_ATT + 3 * ATT_WIDTH
W_ALL = W_GATE + LANES


def _prep_w_in(w_in):
    s0, s1, s2, s3, s4, s5 = np.cumsum((0,) + IN_SECTIONS)[:6].tolist()
    att = w_in[:, :, s4:s5]
    att = jnp.concatenate([att[:, :, :ATT_WIDTH] * (ATT_HEAD_DIM ** -0.5 * LOG2E), att[:, :, ATT_WIDTH:]], axis=-1)
    gate = jnp.pad(w_in[:, :, s3:s4], ((0, 0), (0, 0), (0, LANES - 4 * M_HEADS)))
    return jnp.concatenate([w_in[:, :, s0:s1], w_in[:, :, s2:s3], w_in[:, :, s5:], att, gate],
                           axis=-1).astype(BF16)


def _w_spec(l, rows, width, block_index):
    return pl.BlockSpec((None, rows, width), lambda *g: (l, 0, block_index(*g)))


def _proj_t_kernel(wt_ref, u_ref, o_ref):
    o_ref[...] = lax.dot_general(wt_ref[...], u_ref[...], (((1,), (1,)), ((), ())),
                                 preferred_element_type=F32).astype(o_ref.dtype)


def _proj_t_call(u, wt, tm, name):
    B, S, D = u.shape
    N = wt.shape[0]
    return pl.pallas_call(
        _proj_t_kernel,
        grid=(B, S // tm),
        in_specs=[pl.BlockSpec((N, D), lambda b, i: (0, 0)),
                  pl.BlockSpec((None, tm, D), lambda b, i: (b, i, 0))],
        out_specs=pl.BlockSpec((None, N, tm), lambda b, i: (b, 0, i)),
        out_shape=jax.ShapeDtypeStruct((B, N, S), BF16),
        compiler_params=_params("arbitrary", "arbitrary"),
        name=name,
    )(wt, u)


def _proj_dil_kernel(u_ref, wq_ref, wk_ref, wv_ref, o_ref, *scratch, dil):
    u = u_ref[...]
    rows = u_ref.shape[0] // dil
    gw = 2 * LANES
    j = 0
    for w_ref in (wq_ref, wk_ref, wv_ref):
        for part in range(w_ref.shape[1] // gw):
            acc = jnp.dot(u, w_ref[:, part * gw:(part + 1) * gw], preferred_element_type=F32)
            if dil == 1:
                o_ref[0, :, j * gw:(j + 1) * gw] = acc.astype(o_ref.dtype)
            else:
                acc_ref = scratch[0]
                for c in range(gw // LANES):
                    acc_ref[j, c] = acc[:, c * LANES:(c + 1) * LANES]
                for r in range(dil):
                    for c in range(gw // LANES):
                        o_ref[r, :, j * gw + c * LANES:j * gw + (c + 1) * LANES] = (
                            acc_ref[j, c, pl.ds(r, rows, stride=dil), :].astype(o_ref.dtype))
            j += 1


def _proj_dil_call(u, w_all, l, g, tm, name):
    B, S, D = u.shape
    dil = ATT_PATTERNS[g][1]
    N = 3 * ATT_OUT
    w_specs = [_w_spec(l, D, ATT_OUT, lambda b, i, t=t: (W_ATT + t * ATT_WIDTH) // ATT_OUT + g) for t in range(3)]
    scratch = [] if dil == 1 else [pltpu.VMEM((N // (2 * LANES), 2, tm, LANES), F32)]
    return pl.pallas_call(
        functools.partial(_proj_dil_kernel, dil=dil),
        grid=(B, S // tm),
        in_specs=[pl.BlockSpec((None, tm, D), lambda b, i: (b, i, 0))] + w_specs,
        out_specs=pl.BlockSpec((None, dil, tm // dil, N), lambda b, i: (b, 0, i, 0)),
        out_shape=jax.ShapeDtypeStruct((B, dil, S // dil, N), BF16),
        scratch_shapes=scratch,
        compiler_params=_params("arbitrary", "arbitrary"),
        name=name,
    )(u, w_all, w_all, w_all)


def _gate_kernel(u_ref, w_ref, b_ref, o_ref):
    g = jnp.dot(u_ref[...], w_ref[...], preferred_element_type=F32) + b_ref[...]
    lf = jnp.minimum(g, 0.0) - jnp.log1p(jnp.exp(-jnp.abs(g)))
    row = lax.broadcasted_iota(jnp.int32, (M_CHUNK, LANES), 0)
    lane = lax.broadcasted_iota(jnp.int32, (M_CHUNK, LANES), 1)
    H = M_HEADS
    for c in range(u_ref.shape[0] // M_CHUNK):
        sl = slice(c * M_CHUNK, (c + 1) * M_CHUNK)
        gc, pre, suf = g[sl], lf[sl], lf[sl]
        d = 1
        while d < M_CHUNK:
            pre = pre + jnp.where(row >= d, pltpu.roll(pre, d, 0), 0.0)
            suf = suf + jnp.where(row < M_CHUNK - d, pltpu.roll(suf, M_CHUNK - d, 0), 0.0)
            d *= 2
        a_f = gc - pltpu.roll(pre, LANES - H, 1)
        a_b = gc - pltpu.roll(suf, LANES - H, 1)
        e_f = jnp.broadcast_to(pltpu.roll(pre, 3 * H, 1)[M_CHUNK - 1:M_CHUNK], pre.shape)
        e_b = jnp.broadcast_to(pltpu.roll(suf, 2 * H, 1)[0:1], suf.shape)
        o_ref[sl, :] = jnp.where(lane < H, a_f, jnp.where(lane < 2 * H, pre,
                                 jnp.where(lane < 3 * H, a_b, jnp.where(lane < 4 * H, suf,
                                           jnp.where(lane < 5 * H, e_f, e_b)))))


def _gate_call(u2d, w_all, l, bg, tm=512):
    T, D = u2d.shape
    return pl.pallas_call(
        _gate_kernel,
        grid=(T // tm,),
        in_specs=[pl.BlockSpec((tm, D), lambda i: (i, 0)),
                  _w_spec(l, D, LANES, lambda i: W_GATE // LANES),
                  pl.BlockSpec((1, LANES), lambda i: (0, 0))],
        out_specs=pl.BlockSpec((tm, LANES), lambda i: (i, 0)),
        out_shape=jax.ShapeDtypeStruct((T, LANES), F32),
        compiler_params=_params("arbitrary"),
        name="gates",
    )(u2d, w_all, bg)


def _proj_conv_kernel(u_ref, up_ref, un_ref, wc_ref, wo_ref, cw_ref, cb_ref, y_ref, z_ref, *,
                      scale, transpose_out):
    i = pl.program_id(1)
    tm = u_ref.shape[0]
    hr = up_ref.shape[0]
    before = jnp.where(i > 0, up_ref[...], jnp.zeros_like(up_ref))
    after = jnp.where(i < pl.num_programs(1) - 1, un_ref[...], jnp.zeros_like(un_ref))
    lhs = jnp.concatenate([before, u_ref[...], after], axis=0)
    x = jnp.dot(lhs, wc_ref[...], preferred_element_type=F32)
    z_ref[...] = jnp.dot(u_ref[...], wo_ref[...], preferred_element_type=F32).astype(z_ref.dtype)
    rt, ap = CONV_ROWS, SUBLANES
    for c in range(wc_ref.shape[1] // LANES):
        cs = slice(c * LANES, (c + 1) * LANES)
        for rb in range(tm // rt):
            r0 = hr + rb * rt
            blk = x[r0 - ap:r0 + rt + ap, cs]
            y = cb_ref[:, cs]
            for j in range(CONV_WIDTH):
                k = j - CONV_WIDTH // 2
                xs = blk if k == 0 else pltpu.roll(blk, (-k) % (rt + 2 * ap), 0)
                y = y + cw_ref[j:j + 1, cs] * xs[ap:ap + rt]
            y = y * jax.nn.sigmoid(y)
            if scale != 1.0:
                y = y * scale
            if transpose_out:
                y_ref[cs, rb * rt:(rb + 1) * rt] = y.T.astype(BF16)
            else:
                y_ref[rb * rt:(rb + 1) * rt, cs] = y.astype(BF16)


def _proj_conv_call(u, w_all, l, col0, wo_cols, to, conv_w8, conv_b, scale, transpose_out, name,
                    tm=1024, tc=512):
    B, S, D = u.shape
    nj = M_WIDTH // tc
    assert len(wo_cols) == nj and all(c % to == 0 for c in wo_cols)
    wo_index = lambda j: sum(jnp.where(j == jj, c // to, 0) for jj, c in enumerate(wo_cols))
    hr = 2 * SUBLANES
    rb, nrb = tm // hr, S // hr
    c0 = col0 // tc
    if transpose_out:
        y_spec = pl.BlockSpec((None, tc, tm), lambda b, i, j: (b, j, i))
        y_shape = jax.ShapeDtypeStruct((B, M_WIDTH, S), BF16)
    else:
        y_spec = pl.BlockSpec((None, tm, tc), lambda b, i, j: (b, i, j))
        y_shape = jax.ShapeDtypeStruct((B, S, M_WIDTH), BF16)
    return pl.pallas_call(
        functools.partial(_proj_conv_kernel, scale=scale, transpose_out=transpose_out),
        grid=(B, S // tm, nj),
        in_specs=[pl.BlockSpec((None, tm, D), lambda b, i, j: (b, i, 0)),
                  pl.BlockSpec((None, hr, D), lambda b, i, j: (b, jnp.maximum(i * rb - 1, 0), 0)),
                  pl.BlockSpec((None, hr, D), lambda b, i, j: (b, jnp.minimum((i + 1) * rb, nrb - 1), 0)),
                  _w_spec(l, D, tc, lambda b, i, j: c0 + j),
                  _w_spec(l, D, to, lambda b, i, j: wo_index(j)),
                  pl.BlockSpec((SUBLANES, tc), lambda b, i, j: (0, c0 + j)),
                  pl.BlockSpec((1, tc), lambda b, i, j: (0, c0 + j))],
        out_specs=[y_spec, pl.BlockSpec((None, tm, to), lambda b, i, j: (b, i, j))],
        out_shape=[y_shape, jax.ShapeDtypeStruct((B, S, nj * to), BF16)],
        compiler_params=_params("arbitrary", "arbitrary", "arbitrary"),
        name=name,
    )(u, u, u, w_all, w_all, conv_w8, conv_b)


def _mlstm_kernel(kf_ref, kb_ref, qtf_ref, qtb_ref, vtf_ref, vtb_ref, gf_ref, gb_ref, of_ref, ob_ref,
                  gnw_ref, yhi_ref, ylo_ref, hs_ref, ct_ref, n_ref, m_ref):
    L, H, dh = M_CHUNK, M_HEADS, M_HEAD_DIM
    i = pl.program_id(1)
    nc = pl.num_programs(1)

    @pl.when(i == 0)
    def _init():
        ct_ref[...] = jnp.zeros_like(ct_ref)
        n_ref[...] = jnp.zeros_like(n_ref)
        m_ref[...] = jnp.full(m_ref.shape, NEG, F32)

    row = lax.broadcasted_iota(jnp.int32, (L, L), 0)
    col = lax.broadcasted_iota(jnp.int32, (L, L), 1)
    mask = (row <= col, row >= col)
    k_refs, qt_refs, vt_refs = (kf_ref, kb_ref), (qtf_ref, qtb_ref), (vtf_ref, vtb_ref)
    g = (gf_ref[...], gb_ref[...])
    gt = (g[0].T, g[1].T)
    units = [(d, h) for d in range(2) for h in range(H)]
    twice = lambda r: jnp.concatenate([r, r], axis=1)

    gate = []
    for u, (d, h) in enumerate(units):
        ja, jb, je = h + 2 * H * d, H + h + 2 * H * d, 4 * H + h + H * d
        a_col, a_row = g[d][:, ja:ja + 1], gt[d][ja:ja + 1, :]
        b_row, e_row = gt[d][jb:jb + 1, :], gt[d][je:je + 1, :]
        m_prev = m_ref[u:u + 1, :]
        log_d = jnp.where(mask[d], a_col + b_row, NEG)
        m_inter = b_row + m_prev
        m_t = jnp.maximum(jnp.max(log_d, axis=0, keepdims=True), m_inter)
        lw = e_row + a_row
        m_new = jnp.maximum(e_row + m_prev, jnp.max(lw, axis=-1, keepdims=True))
        m_ref[u:u + 1, :] = m_new
        gate.append(dict(dmat=jnp.exp(log_d - m_t), w_inter=jnp.exp(m_inter - m_t), emt=jnp.exp(-m_t),
                         ws=jnp.exp(lw - m_new), decay=jnp.exp(e_row + m_prev - m_new)))

    ops, pre = [], []
    for u, (d, h) in enumerate(units):
        hs = slice(h * dh, (h + 1) * dh)
        k_u, qt_u, vt_u = k_refs[d][:, hs], qt_refs[d][hs, :], vt_refs[d][hs, :]
        ops.append((k_u, qt_u, vt_u))
        st = jnp.dot(k_u, qt_u, preferred_element_type=F32)
        cq = jnp.dot(ct_ref[u].astype(BF16), qt_u, preferred_element_type=F32)
        n8 = jnp.broadcast_to(n_ref[u:u + 1, :].astype(BF16), (SUBLANES, dh))
        qn = jnp.dot(n8, qt_u, preferred_element_type=F32)[0:1]
        pre.append((st, cq, qn))

    h_t = []
    for u, (d, h) in enumerate(units):
        (k_u, qt_u, vt_u), (st, cq, qn), gq = ops[u], pre[u], gate[u]
        p = st * gq["dmat"]
        den = jnp.sum(p, axis=0, keepdims=True) + gq["w_inter"] * qn
        r = 1.0 / jnp.maximum(jnp.abs(den), gq["emt"])
        num = jnp.dot(vt_u, p.astype(BF16), preferred_element_type=F32) + gq["w_inter"] * cq
        h_t.append(num * r)

    for u, (d, h) in enumerate(units):
        (k_u, qt_u, vt_u), gq = ops[u], gate[u]
        vw = (vt_u.astype(F32) * gq["ws"]).astype(BF16)
        ct_ref[u] = twice(gq["decay"]) * ct_ref[u] + jnp.dot(vw, k_u, preferred_element_type=F32)
        w8 = jnp.broadcast_to(gq["ws"].astype(BF16), (SUBLANES, L))
        n_ref[u:u + 1, :] = (twice(gq["decay"]) * n_ref[u:u + 1, :]
                             + jnp.dot(w8, k_u, preferred_element_type=F32)[0:1])

    chunk = (i, nc - 1 - i)

    @pl.when(i < nc // 2)
    def _park():
        for u, (d, h) in enumerate(units):
            hs_ref[chunk[d], h] = h_t[u]

    @pl.when(i >= nc // 2)
    def _finish():
        for d, (o_ref, y_ref) in enumerate(((of_ref, yhi_ref), (ob_ref, ylo_ref))):
            for h in range(H):
                hs = slice(h * dh, (h + 1) * dh)
                x = h_t[d * H + h] + hs_ref[chunk[d], h]
                mu = jnp.mean(x, axis=0, keepdims=True)
                xc = x - mu
                var = jnp.mean(xc * xc, axis=0, keepdims=True)
                hn = (xc * lax.rsqrt(var + LN_EPS)).T
                y_ref[:, hs] = (jax.nn.sigmoid(o_ref[:, hs].astype(F32)) * hn * gnw_ref[:, hs]).astype(BF16)


def _mlstm_call(k, qt, vt, gates, og, o_col, gnw):
    B, S, W = k.shape
    L = M_CHUNK
    nc = S // L
    assert nc % 2 == 0
    half = nc // 2
    fwd, bwd = (lambda i: i), (lambda i: nc - 1 - i)
    second_f, second_b = (lambda i: jnp.maximum(i, half)), (lambda i: jnp.minimum(nc - 1 - i, half - 1))
    rows = lambda w, ch, c=0: pl.BlockSpec((None, L, w), lambda b, i: (b, ch(i), c))
    cols = lambda ch: pl.BlockSpec((None, W, L), lambda b, i: (b, 0, ch(i)))
    return pl.pallas_call(
        _mlstm_kernel,
        grid=(B, nc),
        in_specs=[rows(W, fwd), rows(W, bwd), cols(fwd), cols(bwd), cols(fwd), cols(bwd),
                  rows(LANES, fwd), rows(LANES, bwd), rows(W, second_f, o_col), rows(W, second_b, o_col),
                  pl.BlockSpec((1, W), lambda b, i: (0, 0))],
        out_specs=[pl.BlockSpec((None, L, W), lambda b, i: (b, jnp.maximum(i - half, 0), 0)),
                   pl.BlockSpec((None, L, W), lambda b, i: (b, jnp.minimum(nc - 1 - i, half - 1), 0))],
        out_shape=[jax.ShapeDtypeStruct((B, S // 2, W), BF16)] * 2,
        scratch_shapes=[pltpu.VMEM((nc, M_HEADS, M_HEAD_DIM, L), F32),
                        pltpu.VMEM((2 * M_HEADS, M_HEAD_DIM, M_HEAD_DIM), F32),
                        pltpu.VMEM((2 * M_HEADS, M_HEAD_DIM), F32),
                        pltpu.VMEM((2 * M_HEADS, L), F32)],
        compiler_params=_params("arbitrary", "arbitrary"),
        name="mlstm",
    )(k, k, qt, qt, vt, vt, gates, gates, og, og, gnw)


ATT_UNROLL = 8


def _attn_kernel(q_ref, k_ref, kp_ref, kn_ref, v_ref, vp_ref, vn_ref, bm_ref, o_ref, lse_ref,
                 *scratch, dil):
    hb = ATT_HALF
    tq = q_ref.shape[1]
    nqb = tq // hb
    i = pl.program_id(1)
    ni = pl.num_programs(1)
    if nqb == 1:
        stage = scratch

        def make_window(main, before, after):
            return lambda r, r0, ls: jnp.concatenate([before[r, :, ls], main[r, :, ls], after[r, :, ls]], axis=0)
    else:
        pads, stage = scratch[:2], scratch[2:]

        def make_window(main, before, after, pad):
            pad[:, 0:hb, :] = before[...]
            pad[:, hb:hb + tq, :] = main[...]
            pad[:, hb + tq:, :] = after[...]
            return lambda r, r0, ls: pad[r, pl.ds(r0, 3 * hb), ls]
    k_window = make_window(k_ref, kp_ref, kn_ref, *(() if nqb == 1 else (pads[0],)))
    v_window = make_window(v_ref, vp_ref, vn_ref, *(() if nqb == 1 else (pads[1],)))
    even = lax.broadcasted_iota(jnp.int32, (hb, LANES), 1) < ATT_HEAD_DIM
    ones = jnp.ones((3 * hb, LANES), BF16)
    if dil > 1:
        os_ref, ls_ref = stage

    npair = ATT_SLOTS // 2
    total = dil * nqb
    step = min(ATT_UNROLL, total)

    def body(it, carry):
        where, scores = [], []
        for j in range(step):
            idx = it * step + j
            r, qb = idx // nqb, idx % nqb
            r0 = pl.multiple_of(qb * hb, hb)
            case = (jnp.logical_and(i == 0, qb == 0).astype(jnp.int32)
                    + 2 * jnp.logical_and(i == ni - 1, qb == nqb - 1).astype(jnp.int32))
            for p in range(npair):
                ls = slice(p * LANES, (p + 1) * LANES)
                q2 = q_ref[r, pl.ds(r0, hb), ls]
                qs = jnp.concatenate([jnp.where(even, q2, 0), jnp.where(even, 0, q2)], axis=0)
                kw = k_window(r, r0, ls)
                s = lax.dot_general(qs, kw, (((1,), (1,)), ((), ())), preferred_element_type=F32)
                scores.append(s + bm_ref[case, p])
                where.append((r, r0, p, ls))
        s = jnp.concatenate(scores, axis=0)
        m = jnp.max(s, axis=-1, keepdims=True)
        e = jnp.exp2(s - m).astype(BF16)
        outs = []
        for u, (r, r0, p, ls) in enumerate(where):
            vw = jnp.concatenate([v_window(r, r0, ls), ones], axis=1)
            outs.append(jnp.dot(e[u * 2 * hb:(u + 1) * 2 * hb], vw, preferred_element_type=F32))
        ol = jnp.concatenate(outs, axis=0)
        l = ol[:, LANES:]
        o = ol[:, :LANES] * (1.0 / l)
        lse = m + jnp.log2(l)
        for u, (r, r0, p, ls) in enumerate(where):
            top, bot = slice(u * 2 * hb, u * 2 * hb + hb), slice(u * 2 * hb + hb, (u + 1) * 2 * hb)
            o2 = jnp.where(even, o[top], o[bot])
            l2 = jnp.where(even, lse[top], lse[bot])
            if dil > 1:
                os_ref[r, pl.ds(r0, hb), ls] = o2
                ls_ref[r, pl.ds(r0, hb), ls] = l2
            else:
                o_ref[p, pl.ds(r0, hb), :] = o2
                lse_ref[p, pl.ds(r0, hb), :] = l2
        return carry

    lax.fori_loop(0, total // step, body, 0)
    if dil > 1:
        for r in range(dil):
            for p in range(ATT_SLOTS // 2):
                ls = slice(p * LANES, (p + 1) * LANES)
                o_ref[p, pl.ds(r, tq, stride=dil), :] = os_ref[r, :, ls]
                lse_ref[p, pl.ds(r, tq, stride=dil), :] = ls_ref[r, :, ls]


def _attn_bias(g):
    h = np.arange(1, ATT_HEADS + 1, dtype=np.float32)
    slopes = np.exp2(-8.0 * h / ATT_HEADS).astype(np.float32).reshape(N_PATTERNS, ATT_SLOTS)[g]
    dil = ATT_PATTERNS[g][1]
    hb = ATT_HALF
    qi = np.arange(hb)[:, None] + hb
    kj = np.arange(3 * hb)[None, :]
    dist = np.abs(qi - kj)
    out = np.empty((4, ATT_SLOTS // 2, 2 * hb, 3 * hb), np.float32)
    for case in range(4):
        valid = dist <= hb
        if case & 1:
            valid = valid & (kj >= hb)
        if case & 2:
            valid = valid & (kj < 2 * hb)
        for p in range(ATT_SLOTS // 2):
            for e in range(2):
                bias = -slopes[2 * p + e] * (dil * dist).astype(np.float32) * np.float32(LOG2E)
                out[case, p, e * hb:(e + 1) * hb] = np.where(valid, bias, np.float32(NEG))
    return jnp.asarray(out)


def _attn_call(zq, g, col0):
    B, dil, n, W = zq.shape
    S = n * dil
    hb = ATT_HALF
    tq = min(n, max(1024 // dil, hb))
    cw = ATT_OUT
    qc, kc, vc = (col0 // cw + t for t in range(3))
    rb, nrb = tq // hb, n // hb

    def main(c):
        return pl.BlockSpec((None, dil, tq, cw), lambda b, i: (b, 0, i, c))

    def prev(c):
        return pl.BlockSpec((None, dil, hb, cw), lambda b, i: (b, 0, jnp.maximum(i * rb - 1, 0), c))

    def nxt(c):
        return pl.BlockSpec((None, dil, hb, cw), lambda b, i: (b, 0, jnp.minimum((i + 1) * rb, nrb - 1), c))

    npair = ATT_SLOTS // 2
    out_spec = pl.BlockSpec((None, npair, tq * dil, LANES), lambda b, i: (b, 0, i, 0))
    scratch = [] if tq == hb else [pltpu.VMEM((dil, tq + 2 * hb, cw), BF16)] * 2
    if dil > 1:
        scratch += [pltpu.VMEM((dil, tq, cw), F32), pltpu.VMEM((dil, tq, cw), F32)]
    return pl.pallas_call(
        functools.partial(_attn_kernel, dil=dil),
        grid=(B, n // tq),
        in_specs=[main(qc), main(kc), prev(kc), nxt(kc), main(vc), prev(vc), nxt(vc),
                  pl.BlockSpec((4, ATT_SLOTS // 2, 2 * hb, 3 * hb), lambda b, i: (0, 0, 0, 0))],
        out_specs=[out_spec, out_spec],
        out_shape=[jax.ShapeDtypeStruct((B, npair, S, LANES), F32)] * 2,
        scratch_shapes=scratch,
        compiler_params=_params("arbitrary", "arbitrary"),
        name=f"attn{g}",
    )(zq, zq, zq, zq, zq, zq, zq, _attn_bias(g))


def _mix_kernel(yhi_ref, ylo_ref, o0_ref, o1_ref, o2_ref, l0_ref, l1_ref, l2_ref, ga_ref, gb_ref,
                h_ref, wa_ref, wb_ref, wo_ref, ada_ref, lng_ref, lnb_ref, hn_ref, u_ref):
    upper = pl.program_id(1) >= pl.num_programs(1) // 2
    ym = jnp.where(upper, yhi_ref[...], ylo_ref[...])
    pairs = []
    for p in range(ATT_SLOTS // 2):
        l0, l1, l2 = l0_ref[p], l1_ref[p], l2_ref[p]
        lm = jnp.maximum(jnp.maximum(l0, l1), l2)
        e0, e1, e2 = jnp.exp2(l0 - lm), jnp.exp2(l1 - lm), jnp.exp2(l2 - lm)
        pairs.append((e0 * o0_ref[p] + e1 * o1_ref[p] + e2 * o2_ref[p]) * (1.0 / (e0 + e1 + e2)))
    att = jnp.concatenate(pairs, axis=-1)
    y_a = jnp.dot(ym, wa_ref[...], preferred_element_type=F32)
    y_b = jnp.dot(att.astype(BF16), wb_ref[...], preferred_element_type=F32)
    mix = (jax.nn.sigmoid(ga_ref[...].astype(F32)) * y_a
           + jax.nn.sigmoid(gb_ref[...].astype(F32)) * y_b)
    y = jnp.dot(mix.astype(BF16), wo_ref[...], preferred_element_type=F32)
    hn = _layer_norm_rows(DN_ALPHA * h_ref[...] + ada_ref[2:3, :] * y) * lng_ref[...] + lnb_ref[...]
    hn_ref[...] = hn
    u_ref[...] = (hn * (1.0 + ada_ref[4:5, :]) + ada_ref[3:4, :]).astype(BF16)


def _mix_call(y_hi, y_lo, att, ga, gb, gb_col, h, wa, wb, wo, ada_l, lng, lnb, tm=256):
    B, S, D = h.shape
    half = S // tm // 2
    row = lambda w, c=0: pl.BlockSpec((None, tm, w), lambda b, i: (b, i, c))
    full = lambda a: pl.BlockSpec(a.shape, lambda b, i: (0,) * a.ndim)
    (o0, l0), (o1, l1), (o2, l2) = att
    return pl.pallas_call(
        _mix_kernel,
        grid=(B, S // tm),
        in_specs=[pl.BlockSpec((None, tm, D), lambda b, i: (b, jnp.maximum(i - half, 0), 0)),
                  pl.BlockSpec((None, tm, D), lambda b, i: (b, jnp.minimum(i, half - 1), 0))]
                 + [pl.BlockSpec((None, ATT_SLOTS // 2, tm, LANES), lambda b, i: (b, 0, i, 0))] * 6
                 + [row(D), row(D, gb_col), row(D),
                  full(wa), full(wb), full(wo),
                  pl.BlockSpec((None, SUBLANES, D), lambda b, i: (b, 0, 0)), full(lng), full(lnb)],
        out_specs=[row(D), row(D)],
        out_shape=[jax.ShapeDtypeStruct((B, S, D), F32), jax.ShapeDtypeStruct((B, S, D), BF16)],
        compiler_params=_params("arbitrary", "arbitrary"),
        name="mix",
    )(y_hi, y_lo, o0, o1, o2, l0, l1, l2, ga, gb, h, wa, wb, wo, ada_l, lng, lnb)


def _ffn_kernel(u_ref, h_ref, w1_ref, w3_ref, w2_ref, ada_ref, lng_ref, lnb_ref, adan_ref,
                hn_ref, *rest, emit_u):
    acc_ref = rest[-1]
    u = u_ref[...]
    for f in range(w1_ref.shape[1] // FF_CHUNK):
        fs = slice(f * FF_CHUNK, (f + 1) * FF_CHUNK)
        a = jnp.dot(u, w1_ref[:, fs], preferred_element_type=F32)
        b = jnp.dot(u, w3_ref[:, fs], preferred_element_type=F32)
        t = (a * jax.nn.sigmoid(a) * b).astype(BF16)
        part = jnp.dot(t, w2_ref[fs, :], preferred_element_type=F32)
        if f == 0:
            acc_ref[...] = part
        else:
            acc_ref[...] += part
    hn = (_layer_norm_rows(DN_ALPHA * h_ref[...] + ada_ref[5:6, :] * acc_ref[...])
          * lng_ref[...] + lnb_ref[...])
    hn_ref[...] = hn
    if emit_u:
        rest[0][...] = (hn * (1.0 + adan_ref[1:2, :]) + adan_ref[0:1, :]).astype(BF16)


def _ffn_call(u, h, w1, w3, w2, ada_l, lng, lnb, ada_next, emit_u, tm=512):
    B, S, D = h.shape
    row = pl.BlockSpec((None, tm, D), lambda b, i: (b, i, 0))
    full = lambda a: pl.BlockSpec(a.shape, lambda b, i: (0,) * a.ndim)
    ada_spec = pl.BlockSpec((None, SUBLANES, D), lambda b, i: (b, 0, 0))
    out_specs = [row, row] if emit_u else [row]
    out_shape = [jax.ShapeDtypeStruct((B, S, D), F32)]
    if emit_u:
        out_shape.append(jax.ShapeDtypeStruct((B, S, D), BF16))
    return pl.pallas_call(
        functools.partial(_ffn_kernel, emit_u=emit_u),
        grid=(B, S // tm),
        in_specs=[row, row, full(w1), full(w3), full(w2), ada_spec, full(lng), full(lnb), ada_spec],
        out_specs=out_specs, out_shape=out_shape,
        scratch_shapes=[pltpu.VMEM((tm, D), F32)],
        compiler_params=_params("arbitrary", "arbitrary"),
        name="ffn",
    )(u, h, w1, w3, w2, ada_l, lng, lnb, ada_next)


def kernel(x, c, w_in, b_gates, conv_w, conv_b, gn_w, w_a, w_b, w_out, w_ada, b_ada, ln1_g, ln1_b,
           w1, w3, w2, ln2_g, ln2_b):
    B, S, D = x.shape
    depth = w_in.shape[0]
    T = B * S
    H = M_HEADS

    c_pad = jnp.zeros((SUBLANES, D), F32).at[:B].set(c)
    ada = _ada_call(c_pad, w_ada, b_ada)
    ada = jnp.pad(ada[:, :, :B].transpose(0, 2, 1, 3), ((0, 0), (0, 0), (0, 2), (0, 0)))

    s1, s2 = IN_SECTIONS[0], IN_SECTIONS[0] + IN_SECTIONS[1]
    w_all = _prep_w_in(w_in)
    h, u = _ln0_call(x, ada[0])
    for l in range(depth):
        wvt = w_in[l, :, s1:s2].T.astype(BF16)
        bg = jnp.pad(b_gates[l], (0, LANES - 4 * H)).reshape(1, LANES)

        u2d = u.reshape(T, D)
        vt = _proj_t_call(u, wvt, 512, "proj_vT")
        zq = [_proj_dil_call(u, w_all, l, g, 1024, f"projb{g}") for g in range(N_PATTERNS)]
        gates = _gate_call(u2d, w_all, l, bg).reshape(B, S, LANES)

        cw8 = jnp.pad(conv_w[l], ((0, SUBLANES - CONV_WIDTH), (0, 0)))
        cb = conv_b[l].reshape(1, 2 * M_WIDTH)
        qt, ga = _proj_conv_call(u, w_all, l, W_QK, (W_GA, W_GA + D // 2), D // 2, cw8, cb,
                                 M_HEAD_DIM ** -0.5, True, "proj_conv_q")
        k, ogb = _proj_conv_call(u, w_all, l, W_QK + M_WIDTH, (W_O, W_GB), D, cw8, cb,
                                 1.0, False, "proj_conv_k")
        y_hi, y_lo = _mlstm_call(k, qt, vt, gates, ogb, 0, gn_w[l].reshape(1, M_WIDTH))

        att = [_attn_call(zq[g], g, 0) for g in range(N_PATTERNS)]
        h, u = _mix_call(y_hi, y_lo, att, ga, ogb, 1, h, w_a[l].astype(BF16), w_b[l].astype(BF16),
                         w_out[l].astype(BF16), ada[l], ln1_g[l].reshape(1, D), ln1_b[l].reshape(1, D))
        last = l == depth - 1
        outs = _ffn_call(u, h, w1[l].astype(BF16), w3[l].astype(BF16), w2[l].astype(BF16), ada[l],
                         ln2_g[l].reshape(1, D), ln2_b[l].reshape(1, D),
                         ada[l] if last else ada[l + 1], not last)
        h = outs[0]
        if not last:
            u = outs[1]
    return h
```

```python
import functools

import jax
import jax.numpy as jnp
import numpy as np
from jax import lax
from jax.experimental import pallas as pl
from jax.experimental.pallas import tpu as pltpu

F32 = jnp.float32
BF16 = jnp.bfloat16

D_MODEL = 1024
DEPTH = 2
M_HEADS = 4
M_HEAD_DIM = 256
M_WIDTH = M_HEADS * M_HEAD_DIM
M_CHUNK = 128
CONV_WIDTH = 5
ATT_SLOTS = 8
ATT_HEAD_DIM = 64
ATT_PATTERNS = ((128, 1), (512, 4), (2048, 16))
N_PATTERNS = 3
ATT_HEADS = ATT_SLOTS * N_PATTERNS
ATT_WIDTH = ATT_HEADS * ATT_HEAD_DIM
ATT_OUT = ATT_SLOTS * ATT_HEAD_DIM
ATT_HALF = 64
D_FF = -(-8 * D_MODEL // (3 * 256)) * 256
DN_ALPHA = (2 * DEPTH) ** 0.25
LN_EPS = 1e-5
NEG = -1e30
IN_SECTIONS = (2 * M_WIDTH, M_WIDTH, M_WIDTH, 4 * M_HEADS, 3 * ATT_WIDTH, 2 * D_MODEL)

VMEM_LIMIT_BYTES = 56 * 1024 * 1024
LANES = 128
SUBLANES = 8

LOG2E = 1.4426950408889634
FF_CHUNK = 256
CONV_ROWS = 128


def _params(*sem):
    return pltpu.CompilerParams(dimension_semantics=sem, vmem_limit_bytes=VMEM_LIMIT_BYTES)


def _layer_norm_rows(x):
    mu = jnp.mean(x, axis=-1, keepdims=True)
    xc = x - mu
    var = jnp.mean(xc * xc, axis=-1, keepdims=True)
    return xc * lax.rsqrt(var + LN_EPS)


def _ada_kernel(c_ref, w_ref, b_ref, o_ref):
    c = c_ref[...]
    ca = c * jax.nn.sigmoid(c)
    o_ref[...] = jnp.dot(ca, w_ref[...], precision=lax.Precision.HIGHEST,
                         preferred_element_type=F32) + b_ref[...]


def _ada_call(c_pad, w_ada, b_ada):
    depth = w_ada.shape[0]
    return pl.pallas_call(
        _ada_kernel,
        grid=(depth, 6),
        in_specs=[pl.BlockSpec((SUBLANES, D_MODEL), lambda l, k: (0, 0)),
                  pl.BlockSpec((None, D_MODEL, D_MODEL), lambda l, k: (l, 0, k)),
                  pl.BlockSpec((None, 1, D_MODEL), lambda l, k: (l, 0, k))],
        out_specs=pl.BlockSpec((None, None, SUBLANES, D_MODEL), lambda l, k: (l, k, 0, 0)),
        out_shape=jax.ShapeDtypeStruct((depth, 6, SUBLANES, D_MODEL), F32),
        compiler_params=_params("arbitrary", "arbitrary"),
        name="ada",
    )(c_pad, w_ada, b_ada.reshape(depth, 1, 6 * D_MODEL))


def _ln0_kernel(x_ref, ada_ref, h_ref, u_ref):
    y = _layer_norm_rows(x_ref[...])
    h_ref[...] = y
    u_ref[...] = (y * (1.0 + ada_ref[1:2, :]) + ada_ref[0:1, :]).astype(BF16)


def _ln0_call(x, ada0, ts=512):
    B, S, D = x.shape
    row = pl.BlockSpec((None, ts, D), lambda b, i: (b, i, 0))
    return pl.pallas_call(
        _ln0_kernel,
        grid=(B, S // ts),
        in_specs=[row, pl.BlockSpec((None, SUBLANES, D), lambda b, i: (b, 0, 0))],
        out_specs=[row, row],
        out_shape=[jax.ShapeDtypeStruct((B, S, D), F32), jax.ShapeDtypeStruct((B, S, D), BF16)],
        compiler_params=_params("arbitrary", "arbitrary"),
        name="ln0",
    )(x, ada0)


W_QK, W_O, W_GA, W_GB, W_ATT = 0, 2 * M_WIDTH, 3 * M_WIDTH, 3 * M_WIDTH + D_MODEL, 3 * M_WIDTH + 2 * D_MODEL
W_GATE = W_ATT + 3 * ATT_WIDTH
W_BLOCK = 512
W_ALL = W_GATE + W_BLOCK
PREP_LOOK = W_BLOCK // LANES + 1


def _prep_plan():
    s0, s1, s2, s3, s4, s5 = np.cumsum((0,) + IN_SECTIONS)[:6].tolist()
    runs = [(W_QK, s0, 2 * M_WIDTH, 0), (W_O, s2, M_WIDTH, 0), (W_GA, s5, 2 * D_MODEL, 0),
            (W_ATT, s4, ATT_WIDTH, 1), (W_ATT + ATT_WIDTH, s4 + ATT_WIDTH, 2 * ATT_WIDTH, 0),
            (W_GATE, s3, 4 * M_HEADS, 0)]
    plan = np.zeros((4, W_ALL // W_BLOCK), np.int32)
    for dst, src, width, is_q in runs:
        for k in range(-(-width // W_BLOCK)):
            col = src + k * W_BLOCK
            plan[:, dst // W_BLOCK + k] = (col // LANES, col % LANES, is_q, min(W_BLOCK, width - k * W_BLOCK))
    return plan


def _prep_kernel(src_ref, off_ref, isq_ref, valid_ref, *refs):
    o_ref = refs[-1]
    ob = pl.program_id(1)
    scale = jnp.where(isq_ref[ob] == 1, ATT_HEAD_DIM ** -0.5 * LOG2E, 1.0)
    for off in sorted({int(c) for c in _prep_plan()[1]}):
        @pl.when(off_ref[ob] == off)
        def _():
            x = jnp.concatenate([r[...] for r in refs[:-1]], axis=1)
            y = x[:, off:off + W_BLOCK] * scale
            lane = lax.broadcasted_iota(jnp.int32, y.shape, 1)
            o_ref[...] = jnp.where(lane < valid_ref[ob], y, 0.0).astype(BF16)


def _prep_w_in(w_in):
    depth, D, _ = w_in.shape
    plan = _prep_plan()
    src_spec = lambda k: pl.BlockSpec((None, D, LANES), lambda l, ob, src, *_: (l, 0, src[ob] + k))
    return pl.pallas_call(
        _prep_kernel,
        grid_spec=pltpu.PrefetchScalarGridSpec(
            num_scalar_prefetch=4, grid=(depth, W_ALL // W_BLOCK),
            in_specs=[src_spec(k) for k in range(PREP_LOOK)],
            out_specs=pl.BlockSpec((None, D, W_BLOCK), lambda l, ob, *_: (l, 0, ob))),
        out_shape=jax.ShapeDtypeStruct((depth, D, W_ALL), BF16),
        compiler_params=_params("arbitrary", "arbitrary"),
        name="prep_w_in",
    )(*(jnp.asarray(p) for p in plan), *([w_in] * PREP_LOOK))


def _w_spec(l, rows, width, block_index):
    return pl.BlockSpec((None, rows, width), lambda *g: (l, 0, block_index(*g)))


def _proj_t_kernel(wt_ref, u_ref, o_ref):
    o_ref[...] = lax.dot_general(wt_ref[...], u_ref[...], (((1,), (1,)), ((), ())),
                                 preferred_element_type=F32).astype(o_ref.dtype)


def _proj_t_call(u, wt, tm, name):
    B, S, D = u.shape
    N = wt.shape[0]
    return pl.pallas_call(
        _proj_t_kernel,
        grid=(B, S // tm),
        in_specs=[pl.BlockSpec((N, D), lambda b, i: (0, 0)),
                  pl.BlockSpec((None, tm, D), lambda b, i: (b, i, 0))],
        out_specs=pl.BlockSpec((None, N, tm), lambda b, i: (b, 0, i)),
        out_shape=jax.ShapeDtypeStruct((B, N, S), BF16),
        compiler_params=_params("arbitrary", "arbitrary"),
        name=name,
    )(wt, u)


def _proj_dil_kernel(u_ref, wq_ref, wk_ref, wv_ref, o_ref, *scratch, dil):
    u = u_ref[...]
    rows = u_ref.shape[0] // dil
    gw = 2 * LANES
    j = 0
    for w_ref in (wq_ref, wk_ref, wv_ref):
        for part in range(w_ref.shape[1] // gw):
            acc = jnp.dot(u, w_ref[:, part * gw:(part + 1) * gw], preferred_element_type=F32)
            if dil == 1:
                o_ref[0, :, j * gw:(j + 1) * gw] = acc.astype(o_ref.dtype)
            else:
                acc_ref = scratch[0]
                for c in range(gw // LANES):
                    acc_ref[j, c] = acc[:, c * LANES:(c + 1) * LANES]
                for r in range(dil):
                    for c in range(gw // LANES):
                        o_ref[r, :, j * gw + c * LANES:j * gw + (c + 1) * LANES] = (
                            acc_ref[j, c, pl.ds(r, rows, stride=dil), :].astype(o_ref.dtype))
            j += 1


def _proj_dil_call(u, w_all, l, g, tm, name):
    B, S, D = u.shape
    dil = ATT_PATTERNS[g][1]
    N = 3 * ATT_OUT
    w_specs = [_w_spec(l, D, ATT_OUT, lambda b, i, t=t: (W_ATT + t * ATT_WIDTH) // ATT_OUT + g) for t in range(3)]
    scratch = [] if dil == 1 else [pltpu.VMEM((N // (2 * LANES), 2, tm, LANES), F32)]
    return pl.pallas_call(
        functools.partial(_proj_dil_kernel, dil=dil),
        grid=(B, S // tm),
        in_specs=[pl.BlockSpec((None, tm, D), lambda b, i: (b, i, 0))] + w_specs,
        out_specs=pl.BlockSpec((None, dil, tm // dil, N), lambda b, i: (b, 0, i, 0)),
        out_shape=jax.ShapeDtypeStruct((B, dil, S // dil, N), BF16),
        scratch_shapes=scratch,
        compiler_params=_params("arbitrary", "arbitrary"),
        name=name,
    )(u, w_all, w_all, w_all)


def _gate_kernel(u_ref, w_ref, b_ref, o_ref):
    g = jnp.dot(u_ref[...], w_ref[...], preferred_element_type=F32) + b_ref[...]
    lf = jnp.minimum(g, 0.0) - jnp.log1p(jnp.exp(-jnp.abs(g)))
    row = lax.broadcasted_iota(jnp.int32, (M_CHUNK, LANES), 0)
    lane = lax.broadcasted_iota(jnp.int32, (M_CHUNK, LANES), 1)
    H = M_HEADS
    for c in range(u_ref.shape[0] // M_CHUNK):
        sl = slice(c * M_CHUNK, (c + 1) * M_CHUNK)
        gc, pre, suf = g[sl], lf[sl], lf[sl]
        d = 1
        while d < M_CHUNK:
            pre = pre + jnp.where(row >= d, pltpu.roll(pre, d, 0), 0.0)
            suf = suf + jnp.where(row < M_CHUNK - d, pltpu.roll(suf, M_CHUNK - d, 0), 0.0)
            d *= 2
        a_f = gc - pltpu.roll(pre, LANES - H, 1)
        a_b = gc - pltpu.roll(suf, LANES - H, 1)
        e_f = jnp.broadcast_to(pltpu.roll(pre, 3 * H, 1)[M_CHUNK - 1:M_CHUNK], pre.shape)
        e_b = jnp.broadcast_to(pltpu.roll(suf, 2 * H, 1)[0:1], suf.shape)
        o_ref[sl, :] = jnp.where(lane < H, a_f, jnp.where(lane < 2 * H, pre,
                                 jnp.where(lane < 3 * H, a_b, jnp.where(lane < 4 * H, suf,
                                           jnp.where(lane < 5 * H, e_f, e_b)))))


def _gate_call(u2d, w_all, l, bg, tm=512):
    T, D = u2d.shape
    return pl.pallas_call(
        _gate_kernel,
        grid=(T // tm,),
        in_specs=[pl.BlockSpec((tm, D), lambda i: (i, 0)),
                  _w_spec(l, D, LANES, lambda i: W_GATE // LANES),
                  pl.BlockSpec((1, LANES), lambda i: (0, 0))],
        out_specs=pl.BlockSpec((tm, LANES), lambda i: (i, 0)),
        out_shape=jax.ShapeDtypeStruct((T, LANES), F32),
        compiler_params=_params("arbitrary"),
        name="gates",
    )(u2d, w_all, bg)


def _proj_conv_kernel(u_ref, up_ref, un_ref, wc_ref, wo_ref, cw_ref, cb_ref, y_ref, z_ref, *,
                      scale, transpose_out):
    i = pl.program_id(1)
    tm = u_ref.shape[0]
    hr = up_ref.shape[0]
    before = jnp.where(i > 0, up_ref[...], jnp.zeros_like(up_ref))
    after = jnp.where(i < pl.num_programs(1) - 1, un_ref[...], jnp.zeros_like(un_ref))
    lhs = jnp.concatenate([before, u_ref[...], after], axis=0)
    x = jnp.dot(lhs, wc_ref[...], preferred_element_type=F32)
    z_ref[...] = jnp.dot(u_ref[...], wo_ref[...], preferred_element_type=F32).astype(z_ref.dtype)
    rt, ap = CONV_ROWS, SUBLANES
    for c in range(wc_ref.shape[1] // LANES):
        cs = slice(c * LANES, (c + 1) * LANES)
        for rb in range(tm // rt):
            r0 = hr + rb * rt
            blk = x[r0 - ap:r0 + rt + ap, cs]
            y = cb_ref[:, cs]
            for j in range(CONV_WIDTH):
                k = j - CONV_WIDTH // 2
                xs = blk if k == 0 else pltpu.roll(blk, (-k) % (rt + 2 * ap), 0)
                y = y + cw_ref[j:j + 1, cs] * xs[ap:ap + rt]
            y = y * jax.nn.sigmoid(y)
            if scale != 1.0:
                y = y * scale
            if transpose_out:
                y_ref[cs, rb * rt:(rb + 1) * rt] = y.T.astype(BF16)
            else:
                y_ref[rb * rt:(rb + 1) * rt, cs] = y.astype(BF16)


def _proj_conv_call(u, w_all, l, col0, wo_cols, to, conv_w8, conv_b, scale, transpose_out, name,
                    tm=1024, tc=512):
    B, S, D = u.shape
    nj = M_WIDTH // tc
    assert len(wo_cols) == nj and all(c % to == 0 for c in wo_cols)
    wo_index = lambda j: sum(jnp.where(j == jj, c // to, 0) for jj, c in enumerate(wo_cols))
    hr = 2 * SUBLANES
    rb, nrb = tm // hr, S // hr
    c0 = col0 // tc
    if transpose_out:
        y_spec = pl.BlockSpec((None, tc, tm), lambda b, i, j: (b, j, i))
        y_shape = jax.ShapeDtypeStruct((B, M_WIDTH, S), BF16)
    else:
        y_spec = pl.BlockSpec((None, tm, tc), lambda b, i, j: (b, i, j))
        y_shape = jax.ShapeDtypeStruct((B, S, M_WIDTH), BF16)
    return pl.pallas_call(
        functools.partial(_proj_conv_kernel, scale=scale, transpose_out=transpose_out),
        grid=(B, S // tm, nj),
        in_specs=[pl.BlockSpec((None, tm, D), lambda b, i, j: (b, i, 0)),
                  pl.BlockSpec((None, hr, D), lambda b, i, j: (b, jnp.maximum(i * rb - 1, 0), 0)),
                  pl.BlockSpec((None, hr, D), lambda b, i, j: (b, jnp.minimum((i + 1) * rb, nrb - 1), 0)),
                  _w_spec(l, D, tc, lambda b, i, j: c0 + j),
                  _w_spec(l, D, to, lambda b, i, j: wo_index(j)),
                  pl.BlockSpec((SUBLANES, tc), lambda b, i, j: (0, c0 + j)),
                  pl.BlockSpec((1, tc), lambda b, i, j: (0, c0 + j))],
        out_specs=[y_spec, pl.BlockSpec((None, tm, to), lambda b, i, j: (b, i, j))],
        out_shape=[y_shape, jax.ShapeDtypeStruct((B, S, nj * to), BF16)],
        compiler_params=_params("arbitrary", "arbitrary", "arbitrary"),
        name=name,
    )(u, u, u, w_all, w_all, conv_w8, conv_b)


def _mlstm_kernel(kf_ref, kb_ref, qtf_ref, qtb_ref, vtf_ref, vtb_ref, gf_ref, gb_ref, of_ref, ob_ref,
                  gnw_ref, yhi_ref, ylo_ref, hs_ref, ct_ref, n_ref, m_ref):
    L, H, dh = M_CHUNK, M_HEADS, M_HEAD_DIM
    i = pl.program_id(1)
    nc = pl.num_programs(1)

    @pl.when(i == 0)
    def _init():
        ct_ref[...] = jnp.zeros_like(ct_ref)
        n_ref[...] = jnp.zeros_like(n_ref)
        m_ref[...] = jnp.full(m_ref.shape, NEG, F32)

    row = lax.broadcasted_iota(jnp.int32, (L, L), 0)
    col = lax.broadcasted_iota(jnp.int32, (L, L), 1)
    mask = (row <= col, row >= col)
    k_refs, qt_refs, vt_refs = (kf_ref, kb_ref), (qtf_ref, qtb_ref), (vtf_ref, vtb_ref)
    g = (gf_ref[...], gb_ref[...])
    gt = (g[0].T, g[1].T)
    units = [(d, h) for d in range(2) for h in range(H)]
    twice = lambda r: jnp.concatenate([r, r], axis=1)

    gate = []
    for u, (d, h) in enumerate(units):
        ja, jb, je = h + 2 * H * d, H + h + 2 * H * d, 4 * H + h + H * d
        a_col, a_row = g[d][:, ja:ja + 1], gt[d][ja:ja + 1, :]
        b_row, e_row = gt[d][jb:jb + 1, :], gt[d][je:je + 1, :]
        m_prev = m_ref[u:u + 1, :]
        log_d = jnp.where(mask[d], a_col + b_row, NEG)
        m_inter = b_row + m_prev
        m_t = jnp.maximum(jnp.max(log_d, axis=0, keepdims=True), m_inter)
        lw = e_row + a_row
        m_new = jnp.maximum(e_row + m_prev, jnp.max(lw, axis=-1, keepdims=True))
        m_ref[u:u + 1, :] = m_new
        gate.append(dict(dmat=jnp.exp(log_d - m_t), w_inter=jnp.exp(m_inter - m_t), emt=jnp.exp(-m_t),
                         ws=jnp.exp(lw - m_new), decay=jnp.exp(e_row + m_prev - m_new)))

    ops, pre = [], []
    for u, (d, h) in enumerate(units):
        hs = slice(h * dh, (h + 1) * dh)
        k_u, qt_u, vt_u = k_refs[d][:, hs], qt_refs[d][hs, :], vt_refs[d][hs, :]
        ops.append((k_u, qt_u, vt_u))
        st = jnp.dot(k_u, qt_u, preferred_element_type=F32)
        cq = jnp.dot(ct_ref[u].astype(BF16), qt_u, preferred_element_type=F32)
        n8 = jnp.broadcast_to(n_ref[u:u + 1, :].astype(BF16), (SUBLANES, dh))
        qn = jnp.dot(n8, qt_u, preferred_element_type=F32)[0:1]
        pre.append((st, cq, qn))

    h_t = []
    for u, (d, h) in enumerate(units):
        (k_u, qt_u, vt_u), (st, cq, qn), gq = ops[u], pre[u], gate[u]
        p = st * gq["dmat"]
        den = jnp.sum(p, axis=0, keepdims=True) + gq["w_inter"] * qn
        r = 1.0 / jnp.maximum(jnp.abs(den), gq["emt"])
        num = jnp.dot(vt_u, p.astype(BF16), preferred_element_type=F32) + gq["w_inter"] * cq
        h_t.append(num * r)

    for u, (d, h) in enumerate(units):
        (k_u, qt_u, vt_u), gq = ops[u], gate[u]
        vw = (vt_u.astype(F32) * gq["ws"]).astype(BF16)
        ct_ref[u] = twice(gq["decay"]) * ct_ref[u] + jnp.dot(vw, k_u, preferred_element_type=F32)
        w8 = jnp.broadcast_to(gq["ws"].astype(BF16), (SUBLANES, L))
        n_ref[u:u + 1, :] = (twice(gq["decay"]) * n_ref[u:u + 1, :]
                             + jnp.dot(w8, k_u, preferred_element_type=F32)[0:1])

    chunk = (i, nc - 1 - i)

    @pl.when(i < nc // 2)
    def _park():
        for u, (d, h) in enumerate(units):
            hs_ref[chunk[d], h] = h_t[u]

    @pl.when(i >= nc // 2)
    def _finish():
        for d, (o_ref, y_ref) in enumerate(((of_ref, yhi_ref), (ob_ref, ylo_ref))):
            for h in range(H):
                hs = slice(h * dh, (h + 1) * dh)
                x = h_t[d * H + h] + hs_ref[chunk[d], h]
                mu = jnp.mean(x, axis=0, keepdims=True)
                xc = x - mu
                var = jnp.mean(xc * xc, axis=0, keepdims=True)
                hn = (xc * lax.rsqrt(var + LN_EPS)).T
                y_ref[:, hs] = (jax.nn.sigmoid(o_ref[:, hs].astype(F32)) * hn * gnw_ref[:, hs]).astype(BF16)


def _mlstm_call(k, qt, vt, gates, og, o_col, gnw):
    B, S, W = k.shape
    L = M_CHUNK
    nc = S // L
    assert nc % 2 == 0
    half = nc // 2
    fwd, bwd = (lambda i: i), (lambda i: nc - 1 - i)
    second_f, second_b = (lambda i: jnp.maximum(i, half)), (lambda i: jnp.minimum(nc - 1 - i, half - 1))
    rows = lambda w, ch, c=0: pl.BlockSpec((None, L, w), lambda b, i: (b, ch(i), c))
    cols = lambda ch: pl.BlockSpec((None, W, L), lambda b, i: (b, 0, ch(i)))
    return pl.pallas_call(
        _mlstm_kernel,
        grid=(B, nc),
        in_specs=[rows(W, fwd), rows(W, bwd), cols(fwd), cols(bwd), cols(fwd), cols(bwd),
                  rows(LANES, fwd), rows(LANES, bwd), rows(W, second_f, o_col), rows(W, second_b, o_col),
                  pl.BlockSpec((1, W), lambda b, i: (0, 0))],
        out_specs=[pl.BlockSpec((None, L, W), lambda b, i: (b, jnp.maximum(i - half, 0), 0)),
                   pl.BlockSpec((None, L, W), lambda b, i: (b, jnp.minimum(nc - 1 - i, half - 1), 0))],
        out_shape=[jax.ShapeDtypeStruct((B, S // 2, W), BF16)] * 2,
        scratch_shapes=[pltpu.VMEM((nc, M_HEADS, M_HEAD_DIM, L), F32),
                        pltpu.VMEM((2 * M_HEADS, M_HEAD_DIM, M_HEAD_DIM), F32),
                        pltpu.VMEM((2 * M_HEADS, M_HEAD_DIM), F32),
                        pltpu.VMEM((2 * M_HEADS, L), F32)],
        compiler_params=_params("arbitrary", "arbitrary"),
        name="mlstm",
    )(k, k, qt, qt, vt, vt, gates, gates, og, og, gnw)


ATT_UNROLL = 8


def _attn_kernel(q_ref, k_ref, kp_ref, kn_ref, v_ref, vp_ref, vn_ref, bm_ref, o_ref, lse_ref,
                 *scratch, dil):
    hb = ATT_HALF
    tq = q_ref.shape[1]
    nqb = tq // hb
    i = pl.program_id(1)
    ni = pl.num_programs(1)
    if nqb == 1:
        stage = scratch

        def make_window(main, before, after):
            return lambda r, r0, ls: jnp.concatenate([before[r, :, ls], main[r, :, ls], after[r, :, ls]], axis=0)
    else:
        pads, stage = scratch[:2], scratch[2:]

        def make_window(main, before, after, pad):
            pad[:, 0:hb, :] = before[...]
            pad[:, hb:hb + tq, :] = main[...]
            pad[:, hb + tq:, :] = after[...]
            return lambda r, r0, ls: pad[r, pl.ds(r0, 3 * hb), ls]
    k_window = make_window(k_ref, kp_ref, kn_ref, *(() if nqb == 1 else (pads[0],)))
    v_window = make_window(v_ref, vp_ref, vn_ref, *(() if nqb == 1 else (pads[1],)))
    even = lax.broadcasted_iota(jnp.int32, (hb, LANES), 1) < ATT_HEAD_DIM
    ones = jnp.ones((3 * hb, LANES), BF16)
    if dil > 1:
        os_ref, ls_ref = stage

    npair = ATT_SLOTS // 2
    total = dil * nqb
    step = min(ATT_UNROLL, total)

    def body(it, carry):
        where, scores = [], []
        for j in range(step):
            idx = it * step + j
            r, qb = idx // nqb, idx % nqb
            r0 = pl.multiple_of(qb * hb, hb)
            case = (jnp.logical_and(i == 0, qb == 0).astype(jnp.int32)
                    + 2 * jnp.logical_and(i == ni - 1, qb == nqb - 1).astype(jnp.int32))
            for p in range(npair):
                ls = slice(p * LANES, (p + 1) * LANES)
                q2 = q_ref[r, pl.ds(r0, hb), ls]
                qs = jnp.concatenate([jnp.where(even, q2, 0), jnp.where(even, 0, q2)], axis=0)
                kw = k_window(r, r0, ls)
                s = lax.dot_general(qs, kw, (((1,), (1,)), ((), ())), preferred_element_type=F32)
                scores.append(s + bm_ref[case, p])
                where.append((r, r0, p, ls))
        s = jnp.concatenate(scores, axis=0)
        m = jnp.max(s, axis=-1, keepdims=True)
        e = jnp.exp2(s - m).astype(BF16)
        outs = []
        for u, (r, r0, p, ls) in enumerate(where):
            vw = jnp.concatenate([v_window(r, r0, ls), ones], axis=1)
            outs.append(jnp.dot(e[u * 2 * hb:(u + 1) * 2 * hb], vw, preferred_element_type=F32))
        ol = jnp.concatenate(outs, axis=0)
        l = ol[:, LANES:]
        o = ol[:, :LANES] * (1.0 / l)
        lse = m + jnp.log2(l)
        for u, (r, r0, p, ls) in enumerate(where):
            top, bot = slice(u * 2 * hb, u * 2 * hb + hb), slice(u * 2 * hb + hb, (u + 1) * 2 * hb)
            o2 = jnp.where(even, o[top], o[bot])
            l2 = jnp.where(even, lse[top], lse[bot])
            if dil > 1:
                os_ref[r, pl.ds(r0, hb), ls] = o2
                ls_ref[r, pl.ds(r0, hb), ls] = l2
            else:
                o_ref[p, pl.ds(r0, hb), :] = o2
                lse_ref[p, pl.ds(r0, hb), :] = l2
        return carry

    lax.fori_loop(0, total // step, body, 0)
    if dil > 1:
        for r in range(dil):
            for p in range(ATT_SLOTS // 2):
                ls = slice(p * LANES, (p + 1) * LANES)
                o_ref[p, pl.ds(r, tq, stride=dil), :] = os_ref[r, :, ls]
                lse_ref[p, pl.ds(r, tq, stride=dil), :] = ls_ref[r, :, ls]


def _attn_bias(g):
    h = np.arange(1, ATT_HEADS + 1, dtype=np.float32)
    slopes = np.exp2(-8.0 * h / ATT_HEADS).astype(np.float32).reshape(N_PATTERNS, ATT_SLOTS)[g]
    dil = ATT_PATTERNS[g][1]
    hb = ATT_HALF
    qi = np.arange(hb)[:, None] + hb
    kj = np.arange(3 * hb)[None, :]
    dist = np.abs(qi - kj)
    out = np.empty((4, ATT_SLOTS // 2, 2 * hb, 3 * hb), np.float32)
    for case in range(4):
        valid = dist <= hb
        if case & 1:
            valid = valid & (kj >= hb)
        if case & 2:
            valid = valid & (kj < 2 * hb)
        for p in range(ATT_SLOTS // 2):
            for e in range(2):
                bias = -slopes[2 * p + e] * (dil * dist).astype(np.float32) * np.float32(LOG2E)
                out[case, p, e * hb:(e + 1) * hb] = np.where(valid, bias, np.float32(NEG))
    return jnp.asarray(out)


def _attn_call(zq, g, col0):
    B, dil, n, W = zq.shape
    S = n * dil
    hb = ATT_HALF
    tq = min(n, max(1024 // dil, hb))
    cw = ATT_OUT
    qc, kc, vc = (col0 // cw + t for t in range(3))
    rb, nrb = tq // hb, n // hb

    def main(c):
        return pl.BlockSpec((None, dil, tq, cw), lambda b, i: (b, 0, i, c))

    def prev(c):
        return pl.BlockSpec((None, dil, hb, cw), lambda b, i: (b, 0, jnp.maximum(i * rb - 1, 0), c))

    def nxt(c):
        return pl.BlockSpec((None, dil, hb, cw), lambda b, i: (b, 0, jnp.minimum((i + 1) * rb, nrb - 1), c))

    npair = ATT_SLOTS // 2
    out_spec = pl.BlockSpec((None, npair, tq * dil, LANES), lambda b, i: (b, 0, i, 0))
    scratch = [] if tq == hb else [pltpu.VMEM((dil, tq + 2 * hb, cw), BF16)] * 2
    if dil > 1:
        scratch += [pltpu.VMEM((dil, tq, cw), F32), pltpu.VMEM((dil, tq, cw), F32)]
    return pl.pallas_call(
        functools.partial(_attn_kernel, dil=dil),
        grid=(B, n // tq),
        in_specs=[main(qc), main(kc), prev(kc), nxt(kc), main(vc), prev(vc), nxt(vc),
                  pl.BlockSpec((4, ATT_SLOTS // 2, 2 * hb, 3 * hb), lambda b, i: (0, 0, 0, 0))],
        out_specs=[out_spec, out_spec],
        out_shape=[jax.ShapeDtypeStruct((B, npair, S, LANES), F32)] * 2,
        scratch_shapes=scratch,
        compiler_params=_params("arbitrary", "arbitrary"),
        name=f"attn{g}",
    )(zq, zq, zq, zq, zq, zq, zq, _attn_bias(g))


def _mix_kernel(yhi_ref, ylo_ref, o0_ref, o1_ref, o2_ref, l0_ref, l1_ref, l2_ref, ga_ref, gb_ref,
                h_ref, wa_ref, wb_ref, wo_ref, ada_ref, lng_ref, lnb_ref, hn_ref, u_ref):
    upper = pl.program_id(1) >= pl.num_programs(1) // 2
    ym = jnp.where(upper, yhi_ref[...], ylo_ref[...])
    pairs = []
    for p in range(ATT_SLOTS // 2):
        l0, l1, l2 = l0_ref[p], l1_ref[p], l2_ref[p]
        lm = jnp.maximum(jnp.maximum(l0, l1), l2)
        e0, e1, e2 = jnp.exp2(l0 - lm), jnp.exp2(l1 - lm), jnp.exp2(l2 - lm)
        pairs.append((e0 * o0_ref[p] + e1 * o1_ref[p] + e2 * o2_ref[p]) * (1.0 / (e0 + e1 + e2)))
    att = jnp.concatenate(pairs, axis=-1)
    y_a = jnp.dot(ym, wa_ref[...], preferred_element_type=F32)
    y_b = jnp.dot(att.astype(BF16), wb_ref[...], preferred_element_type=F32)
    mix = (jax.nn.sigmoid(ga_ref[...].astype(F32)) * y_a
           + jax.nn.sigmoid(gb_ref[...].astype(F32)) * y_b)
    y = jnp.dot(mix.astype(BF16), wo_ref[...], preferred_element_type=F32)
    hn = _layer_norm_rows(DN_ALPHA * h_ref[...] + ada_ref[2:3, :] * y) * lng_ref[...] + lnb_ref[...]
    hn_ref[...] = hn
    u_ref[...] = (hn * (1.0 + ada_ref[4:5, :]) + ada_ref[3:4, :]).astype(BF16)


def _mix_call(y_hi, y_lo, att, ga, gb, gb_col, h, wa, wb, wo, ada_l, lng, lnb, tm=256):
    B, S, D = h.shape
    half = S // tm // 2
    row = lambda w, c=0: pl.BlockSpec((None, tm, w), lambda b, i: (b, i, c))
    full = lambda a: pl.BlockSpec(a.shape, lambda b, i: (0,) * a.ndim)
    (o0, l0), (o1, l1), (o2, l2) = att
    return pl.pallas_call(
        _mix_kernel,
        grid=(B, S // tm),
        in_specs=[pl.BlockSpec((None, tm, D), lambda b, i: (b, jnp.maximum(i - half, 0), 0)),
                  pl.BlockSpec((None, tm, D), lambda b, i: (b, jnp.minimum(i, half - 1), 0))]
                 + [pl.BlockSpec((None, ATT_SLOTS // 2, tm, LANES), lambda b, i: (b, 0, i, 0))] * 6
                 + [row(D), row(D, gb_col), row(D),
                  full(wa), full(wb), full(wo),
                  pl.BlockSpec((None, SUBLANES, D), lambda b, i: (b, 0, 0)), full(lng), full(lnb)],
        out_specs=[row(D), row(D)],
        out_shape=[jax.ShapeDtypeStruct((B, S, D), F32), jax.ShapeDtypeStruct((B, S, D), BF16)],
        compiler_params=_params("arbitrary", "arbitrary"),
        name="mix",
    )(y_hi, y_lo, o0, o1, o2, l0, l1, l2, ga, gb, h, wa, wb, wo, ada_l, lng, lnb)


def _ffn_kernel(u_ref, h_ref, w1_ref, w3_ref, w2_ref, ada_ref, lng_ref, lnb_ref, adan_ref,
                hn_ref, *rest, emit_u):
    acc_ref = rest[-1]
    u = u_ref[...]
    for f in range(w1_ref.shape[1] // FF_CHUNK):
        fs = slice(f * FF_CHUNK, (f + 1) * FF_CHUNK)
        a = jnp.dot(u, w1_ref[:, fs], preferred_element_type=F32)
        b = jnp.dot(u, w3_ref[:, fs], preferred_element_type=F32)
        t = (a * jax.nn.sigmoid(a) * b).astype(BF16)
        part = jnp.dot(t, w2_ref[fs, :], preferred_element_type=F32)
        if f == 0:
            acc_ref[...] = part
        else:
            acc_ref[...] += part
    hn = (_layer_norm_rows(DN_ALPHA * h_ref[...] + ada_ref[5:6, :] * acc_ref[...])
          * lng_ref[...] + lnb_ref[...])
    hn_ref[...] = hn
    if emit_u:
        rest[0][...] = (hn * (1.0 + adan_ref[1:2, :]) + adan_ref[0:1, :]).astype(BF16)


def _ffn_call(u, h, w1, w3, w2, ada_l, lng, lnb, ada_next, emit_u, tm=512):
    B, S, D = h.shape
    row = pl.BlockSpec((None, tm, D), lambda b, i: (b, i, 0))
    full = lambda a: pl.BlockSpec(a.shape, lambda b, i: (0,) * a.ndim)
    ada_spec = pl.BlockSpec((None, SUBLANES, D), lambda b, i: (b, 0, 0))
    out_specs = [row, row] if emit_u else [row]
    out_shape = [jax.ShapeDtypeStruct((B, S, D), F32)]
    if emit_u:
        out_shape.append(jax.ShapeDtypeStruct((B, S, D), BF16))
    return pl.pallas_call(
        functools.partial(_ffn_kernel, emit_u=emit_u),
        grid=(B, S // tm),
        in_specs=[row, row, full(w1), full(w3), full(w2), ada_spec, full(lng), full(lnb), ada_spec],
        out_specs=out_specs, out_shape=out_shape,
        scratch_shapes=[pltpu.VMEM((tm, D), F32)],
        compiler_params=_params("arbitrary", "arbitrary"),
        name="ffn",
    )(u, h, w1, w3, w2, ada_l, lng, lnb, ada_next)


def kernel(x, c, w_in, b_gates, conv_w, conv_b, gn_w, w_a, w_b, w_out, w_ada, b_ada, ln1_g, ln1_b,
           w1, w3, w2, ln2_g, ln2_b):
    B, S, D = x.shape
    depth = w_in.shape[0]
    T = B * S
    H = M_HEADS

    c_pad = jnp.zeros((SUBLANES, D), F32).at[:B].set(c)
    ada = _ada_call(c_pad, w_ada, b_ada)
    ada = jnp.pad(ada[:, :, :B].transpose(0, 2, 1, 3), ((0, 0), (0, 0), (0, 2), (0, 0)))

    s1, s2 = IN_SECTIONS[0], IN_SECTIONS[0] + IN_SECTIONS[1]
    w_all = _prep_w_in(w_in)
    h, u = _ln0_call(x, ada[0])
    for l in range(depth):
        wvt = w_in[l, :, s1:s2].T.astype(BF16)
        bg = jnp.pad(b_gates[l], (0, LANES - 4 * H)).reshape(1, LANES)

        u2d = u.reshape(T, D)
        vt = _proj_t_call(u, wvt, 512, "proj_vT")
        zq = [_proj_dil_call(u, w_all, l, g, 1024, f"projb{g}") for g in range(N_PATTERNS)]
        gates = _gate_call(u2d, w_all, l, bg).reshape(B, S, LANES)

        cw8 = jnp.pad(conv_w[l], ((0, SUBLANES - CONV_WIDTH), (0, 0)))
        cb = conv_b[l].reshape(1, 2 * M_WIDTH)
        qt, ga = _proj_conv_call(u, w_all, l, W_QK, (W_GA, W_GA + D // 2), D // 2, cw8, cb,
                                 M_HEAD_DIM ** -0.5, True, "proj_conv_q")
        k, ogb = _proj_conv_call(u, w_all, l, W_QK + M_WIDTH, (W_O, W_GB), D, cw8, cb,
                                 1.0, False, "proj_conv_k")
        y_hi, y_lo = _mlstm_call(k, qt, vt, gates, ogb, 0, gn_w[l].reshape(1, M_WIDTH))

        att = [_attn_call(zq[g], g, 0) for g in range(N_PATTERNS)]
        h, u = _mix_call(y_hi, y_lo, att, ga, ogb, 1, h, w_a[l].astype(BF16), w_b[l].astype(BF16),
                         w_out[l].astype(BF16), ada[l], ln1_g[l].reshape(1, D), ln1_b[l].reshape(1, D))
        last = l == depth - 1
        outs = _ffn_call(u, h, w1[l].astype(BF16), w3[l].astype(BF16), w2[l].astype(BF16), ada[l],
                         ln2_g[l].reshape(1, D), ln2_b[l].reshape(1, D),
                         ada[l] if last else ada[l + 1], not last)
        h = outs[0]
        if not last:
            u = outs[1]
    return h
```

```python
import functools

import jax
import jax.numpy as jnp
import numpy as np
from jax import lax
from jax.experimental import pallas as pl
from jax.experimental.pallas import tpu as pltpu

F32 = jnp.float32
BF16 = jnp.bfloat16

D_MODEL = 1024
DEPTH = 2
M_HEADS = 4
M_HEAD_DIM = 256
M_WIDTH = M_HEADS * M_HEAD_DIM
M_CHUNK = 128
CONV_WIDTH = 5
ATT_SLOTS = 8
ATT_HEAD_DIM = 64
ATT_PATTERNS = ((128, 1), (512, 4), (2048, 16))
N_PATTERNS = 3
ATT_HEADS = ATT_SLOTS * N_PATTERNS
ATT_WIDTH = ATT_HEADS * ATT_HEAD_DIM
ATT_OUT = ATT_SLOTS * ATT_HEAD_DIM
ATT_HALF = 64
D_FF = -(-8 * D_MODEL // (3 * 256)) * 256
DN_ALPHA = (2 * DEPTH) ** 0.25
LN_EPS = 1e-5
NEG = -1e30
IN_SECTIONS = (2 * M_WIDTH, M_WIDTH, M_WIDTH, 4 * M_HEADS, 3 * ATT_WIDTH, 2 * D_MODEL)

VMEM_LIMIT_BYTES = 56 * 1024 * 1024
LANES = 128
SUBLANES = 8

LOG2E = 1.4426950408889634
FF_CHUNK = 256
CONV_ROWS = 128


def _params(*sem):
    return pltpu.CompilerParams(dimension_semantics=sem, vmem_limit_bytes=VMEM_LIMIT_BYTES)


def _layer_norm_rows(x):
    mu = jnp.mean(x, axis=-1, keepdims=True)
    xc = x - mu
    var = jnp.mean(xc * xc, axis=-1, keepdims=True)
    return xc * lax.rsqrt(var + LN_EPS)


def _ada_kernel(c_ref, w_ref, b_ref, o_ref):
    c = c_ref[...]
    ca = c * jax.nn.sigmoid(c)
    o_ref[...] = jnp.dot(ca, w_ref[...], precision=lax.Precision.HIGHEST,
                         preferred_element_type=F32) + b_ref[...]


def _ada_call(c_pad, w_ada, b_ada):
    depth = w_ada.shape[0]
    return pl.pallas_call(
        _ada_kernel,
        grid=(depth, 6),
        in_specs=[pl.BlockSpec((SUBLANES, D_MODEL), lambda l, k: (0, 0)),
                  pl.BlockSpec((None, D_MODEL, D_MODEL), lambda l, k: (l, 0, k)),
                  pl.BlockSpec((None, 1, D_MODEL), lambda l, k: (l, 0, k))],
        out_specs=pl.BlockSpec((None, None, SUBLANES, D_MODEL), lambda l, k: (l, k, 0, 0)),
        out_shape=jax.ShapeDtypeStruct((depth, 6, SUBLANES, D_MODEL), F32),
        compiler_params=_params("arbitrary", "arbitrary"),
        name="ada",
    )(c_pad, w_ada, b_ada.reshape(depth, 1, 6 * D_MODEL))


def _ln0_kernel(x_ref, ada_ref, h_ref, u_ref):
    y = _layer_norm_rows(x_ref[...])
    h_ref[...] = y
    u_ref[...] = (y * (1.0 + ada_ref[1:2, :]) + ada_ref[0:1, :]).astype(BF16)


def _ln0_call(x, ada0, ts=512):
    B, S, D = x.shape
    row = pl.BlockSpec((None, ts, D), lambda b, i: (b, i, 0))
    return pl.pallas_call(
        _ln0_kernel,
        grid=(B, S // ts),
        in_specs=[row, pl.BlockSpec((None, SUBLANES, D), lambda b, i: (b, 0, 0))],
        out_specs=[row, row],
        out_shape=[jax.ShapeDtypeStruct((B, S, D), F32), jax.ShapeDtypeStruct((B, S, D), BF16)],
        compiler_params=_params("arbitrary", "arbitrary"),
        name="ln0",
    )(x, ada0)


W_QK, W_O, W_GA, W_GB, W_ATT = 0, 2 * M_WIDTH, 3 * M_WIDTH, 3 * M_WIDTH + D_MODEL, 3 * M_WIDTH + 2 * D_MODEL
W_GATE = W_ATT + 3 * ATT_WIDTH
W_BLOCK = 512
W_ALL = W_GATE + W_BLOCK
PREP_LOOK = W_BLOCK // LANES + 1


def _prep_plan():
    s0, s1, s2, s3, s4, s5 = np.cumsum((0,) + IN_SECTIONS)[:6].tolist()
    runs = [(W_QK, s0, 2 * M_WIDTH, 0), (W_O, s2, M_WIDTH, 0), (W_GA, s5, 2 * D_MODEL, 0),
            (W_ATT, s4, ATT_WIDTH, 1), (W_ATT + ATT_WIDTH, s4 + ATT_WIDTH, 2 * ATT_WIDTH, 0),
            (W_GATE, s3, 4 * M_HEADS, 0)]
    plan = np.zeros((4, W_ALL // W_BLOCK), np.int32)
    for dst, src, width, is_q in runs:
        for k in range(-(-width // W_BLOCK)):
            col = src + k * W_BLOCK
            plan[:, dst // W_BLOCK + k] = (col // LANES, col % LANES, is_q, min(W_BLOCK, width - k * W_BLOCK))
    return plan


def _prep_kernel(src_ref, off_ref, isq_ref, valid_ref, *refs):
    o_ref = refs[-1]
    ob = pl.program_id(1)
    scale = jnp.where(isq_ref[ob] == 1, ATT_HEAD_DIM ** -0.5 * LOG2E, 1.0)
    for off in sorted({int(c) for c in _prep_plan()[1]}):
        @pl.when(off_ref[ob] == off)
        def _():
            x = jnp.concatenate([r[...] for r in refs[:-1]], axis=1)
            y = x[:, off:off + W_BLOCK] * scale
            lane = lax.broadcasted_iota(jnp.int32, y.shape, 1)
            o_ref[...] = jnp.where(lane < valid_ref[ob], y, 0.0).astype(BF16)


def _prep_w_in(w_in):
    depth, D, _ = w_in.shape
    plan = _prep_plan()
    src_spec = lambda k: pl.BlockSpec((None, D, LANES), lambda l, ob, src, *_: (l, 0, src[ob] + k))
    return pl.pallas_call(
        _prep_kernel,
        grid_spec=pltpu.PrefetchScalarGridSpec(
            num_scalar_prefetch=4, grid=(depth, W_ALL // W_BLOCK),
            in_specs=[src_spec(k) for k in range(PREP_LOOK)],
            out_specs=pl.BlockSpec((None, D, W_BLOCK), lambda l, ob, *_: (l, 0, ob))),
        out_shape=jax.ShapeDtypeStruct((depth, D, W_ALL), BF16),
        compiler_params=_params("arbitrary", "arbitrary"),
        name="prep_w_in",
    )(*(jnp.asarray(p) for p in plan), *([w_in] * PREP_LOOK))


def _w_spec(l, rows, width, block_index):
    return pl.BlockSpec((None, rows, width), lambda *g: (l, 0, block_index(*g)))


def _proj_t_kernel(wt_ref, u_ref, o_ref):
    o_ref[...] = lax.dot_general(wt_ref[...], u_ref[...], (((1,), (1,)), ((), ())),
                                 preferred_element_type=F32).astype(o_ref.dtype)


def _proj_t_call(u, wt_all, l, tm, name):
    B, S, D = u.shape
    N = wt_all.shape[1]
    return pl.pallas_call(
        _proj_t_kernel,
        grid=(B, S // tm),
        in_specs=[pl.BlockSpec((None, N, D), lambda b, i: (l, 0, 0)),
                  pl.BlockSpec((None, tm, D), lambda b, i: (b, i, 0))],
        out_specs=pl.BlockSpec((None, N, tm), lambda b, i: (b, 0, i)),
        out_shape=jax.ShapeDtypeStruct((B, N, S), BF16),
        compiler_params=_params("arbitrary", "arbitrary"),
        name=name,
    )(wt_all, u)


def _transpose_cast_kernel(w_ref, o_ref):
    o_ref[...] = w_ref[...].T.astype(o_ref.dtype)


def _prep_w_v_t(w_in, col0, width, tc=512):
    depth, D, _ = w_in.shape
    return pl.pallas_call(
        _transpose_cast_kernel,
        grid=(depth, width // tc),
        in_specs=[pl.BlockSpec((None, D, tc), lambda l, j: (l, 0, col0 // tc + j))],
        out_specs=pl.BlockSpec((None, tc, D), lambda l, j: (l, j, 0)),
        out_shape=jax.ShapeDtypeStruct((depth, width, D), BF16),
        compiler_params=_params("arbitrary", "arbitrary"),
        name="prep_w_vT",
    )(w_in)


def _proj_dil_kernel(u_ref, wq_ref, wk_ref, wv_ref, o_ref, *scratch, dil):
    u = u_ref[...]
    rows = u_ref.shape[0] // dil
    gw = 2 * LANES
    j = 0
    for w_ref in (wq_ref, wk_ref, wv_ref):
        for part in range(w_ref.shape[1] // gw):
            acc = jnp.dot(u, w_ref[:, part * gw:(part + 1) * gw], preferred_element_type=F32)
            if dil == 1:
                o_ref[0, :, j * gw:(j + 1) * gw] = acc.astype(o_ref.dtype)
            else:
                acc_ref = scratch[0]
                for c in range(gw // LANES):
                    acc_ref[j, c] = acc[:, c * LANES:(c + 1) * LANES]
                for r in range(dil):
                    for c in range(gw // LANES):
                        o_ref[r, :, j * gw + c * LANES:j * gw + (c + 1) * LANES] = (
                            acc_ref[j, c, pl.ds(r, rows, stride=dil), :].astype(o_ref.dtype))
            j += 1


def _proj_dil_call(u, w_all, l, g, tm, name):
    B, S, D = u.shape
    dil = ATT_PATTERNS[g][1]
    N = 3 * ATT_OUT
    w_specs = [_w_spec(l, D, ATT_OUT, lambda b, i, t=t: (W_ATT + t * ATT_WIDTH) // ATT_OUT + g) for t in range(3)]
    scratch = [] if dil == 1 else [pltpu.VMEM((N // (2 * LANES), 2, tm, LANES), F32)]
    return pl.pallas_call(
        functools.partial(_proj_dil_kernel, dil=dil),
        grid=(B, S // tm),
        in_specs=[pl.BlockSpec((None, tm, D), lambda b, i: (b, i, 0))] + w_specs,
        out_specs=pl.BlockSpec((None, dil, tm // dil, N), lambda b, i: (b, 0, i, 0)),
        out_shape=jax.ShapeDtypeStruct((B, dil, S // dil, N), BF16),
        scratch_shapes=scratch,
        compiler_params=_params("arbitrary", "arbitrary"),
        name=name,
    )(u, w_all, w_all, w_all)


def _gate_kernel(u_ref, w_ref, b_ref, o_ref):
    g = jnp.dot(u_ref[...], w_ref[...], preferred_element_type=F32) + b_ref[...]
    lf = jnp.minimum(g, 0.0) - jnp.log1p(jnp.exp(-jnp.abs(g)))
    row = lax.broadcasted_iota(jnp.int32, (M_CHUNK, LANES), 0)
    lane = lax.broadcasted_iota(jnp.int32, (M_CHUNK, LANES), 1)
    H = M_HEADS
    for c in range(u_ref.shape[0] // M_CHUNK):
        sl = slice(c * M_CHUNK, (c + 1) * M_CHUNK)
        gc, pre, suf = g[sl], lf[sl], lf[sl]
        d = 1
        while d < M_CHUNK:
            pre = pre + jnp.where(row >= d, pltpu.roll(pre, d, 0), 0.0)
            suf = suf + jnp.where(row < M_CHUNK - d, pltpu.roll(suf, M_CHUNK - d, 0), 0.0)
            d *= 2
        a_f = gc - pltpu.roll(pre, LANES - H, 1)
        a_b = gc - pltpu.roll(suf, LANES - H, 1)
        e_f = jnp.broadcast_to(pltpu.roll(pre, 3 * H, 1)[M_CHUNK - 1:M_CHUNK], pre.shape)
        e_b = jnp.broadcast_to(pltpu.roll(suf, 2 * H, 1)[0:1], suf.shape)
        o_ref[sl, :] = jnp.where(lane < H, a_f, jnp.where(lane < 2 * H, pre,
                                 jnp.where(lane < 3 * H, a_b, jnp.where(lane < 4 * H, suf,
                                           jnp.where(lane < 5 * H, e_f, e_b)))))


def _gate_call(u2d, w_all, l, bg, tm=512):
    T, D = u2d.shape
    return pl.pallas_call(
        _gate_kernel,
        grid=(T // tm,),
        in_specs=[pl.BlockSpec((tm, D), lambda i: (i, 0)),
                  _w_spec(l, D, LANES, lambda i: W_GATE // LANES),
                  pl.BlockSpec((1, LANES), lambda i: (0, 0))],
        out_specs=pl.BlockSpec((tm, LANES), lambda i: (i, 0)),
        out_shape=jax.ShapeDtypeStruct((T, LANES), F32),
        compiler_params=_params("arbitrary"),
        name="gates",
    )(u2d, w_all, bg)


def _proj_conv_kernel(u_ref, up_ref, un_ref, wc_ref, wo_ref, cw_ref, cb_ref, y_ref, z_ref, *,
                      scale, transpose_out):
    i = pl.program_id(1)
    tm = u_ref.shape[0]
    hr = up_ref.shape[0]
    before = jnp.where(i > 0, up_ref[...], jnp.zeros_like(up_ref))
    after = jnp.where(i < pl.num_programs(1) - 1, un_ref[...], jnp.zeros_like(un_ref))
    lhs = jnp.concatenate([before, u_ref[...], after], axis=0)
    x = jnp.dot(lhs, wc_ref[...], preferred_element_type=F32)
    z_ref[...] = jnp.dot(u_ref[...], wo_ref[...], preferred_element_type=F32).astype(z_ref.dtype)
    rt, ap = CONV_ROWS, SUBLANES
    for c in range(wc_ref.shape[1] // LANES):
        cs = slice(c * LANES, (c + 1) * LANES)
        for rb in range(tm // rt):
            r0 = hr + rb * rt
            blk = x[r0 - ap:r0 + rt + ap, cs]
            y = cb_ref[:, cs]
            for j in range(CONV_WIDTH):
                k = j - CONV_WIDTH // 2
                xs = blk if k == 0 else pltpu.roll(blk, (-k) % (rt + 2 * ap), 0)
                y = y + cw_ref[j:j + 1, cs] * xs[ap:ap + rt]
            y = y * jax.nn.sigmoid(y)
            if scale != 1.0:
                y = y * scale
            if transpose_out:
                y_ref[cs, rb * rt:(rb + 1) * rt] = y.T.astype(BF16)
            else:
                y_ref[rb * rt:(rb + 1) * rt, cs] = y.astype(BF16)


def _proj_conv_call(u, w_all, l, col0, wo_cols, to, conv_w8, conv_b, scale, transpose_out, name,
                    tm=1024, tc=512):
    B, S, D = u.shape
    nj = M_WIDTH // tc
    assert len(wo_cols) == nj and all(c % to == 0 for c in wo_cols)
    wo_index = lambda j: sum(jnp.where(j == jj, c // to, 0) for jj, c in enumerate(wo_cols))
    hr = 2 * SUBLANES
    rb, nrb = tm // hr, S // hr
    c0 = col0 // tc
    if transpose_out:
        y_spec = pl.BlockSpec((None, tc, tm), lambda b, i, j: (b, j, i))
        y_shape = jax.ShapeDtypeStruct((B, M_WIDTH, S), BF16)
    else:
        y_spec = pl.BlockSpec((None, tm, tc), lambda b, i, j: (b, i, j))
        y_shape = jax.ShapeDtypeStruct((B, S, M_WIDTH), BF16)
    return pl.pallas_call(
        functools.partial(_proj_conv_kernel, scale=scale, transpose_out=transpose_out),
        grid=(B, S // tm, nj),
        in_specs=[pl.BlockSpec((None, tm, D), lambda b, i, j: (b, i, 0)),
                  pl.BlockSpec((None, hr, D), lambda b, i, j: (b, jnp.maximum(i * rb - 1, 0), 0)),
                  pl.BlockSpec((None, hr, D), lambda b, i, j: (b, jnp.minimum((i + 1) * rb, nrb - 1), 0)),
                  _w_spec(l, D, tc, lambda b, i, j: c0 + j),
                  _w_spec(l, D, to, lambda b, i, j: wo_index(j)),
                  pl.BlockSpec((SUBLANES, tc), lambda b, i, j: (0, c0 + j)),
                  pl.BlockSpec((1, tc), lambda b, i, j: (0, c0 + j))],
        out_specs=[y_spec, pl.BlockSpec((None, tm, to), lambda b, i, j: (b, i, j))],
        out_shape=[y_shape, jax.ShapeDtypeStruct((B, S, nj * to), BF16)],
        compiler_params=_params("arbitrary", "arbitrary", "arbitrary"),
        name=name,
    )(u, u, u, w_all, w_all, conv_w8, conv_b)


def _mlstm_kernel(kf_ref, kb_ref, qtf_ref, qtb_ref, vtf_ref, vtb_ref, gf_ref, gb_ref, of_ref, ob_ref,
                  gnw_ref, yhi_ref, ylo_ref, hs_ref, ct_ref, n_ref, m_ref):
    L, H, dh = M_CHUNK, M_HEADS, M_HEAD_DIM
    i = pl.program_id(1)
    nc = pl.num_programs(1)

    @pl.when(i == 0)
    def _init():
        ct_ref[...] = jnp.zeros_like(ct_ref)
        n_ref[...] = jnp.zeros_like(n_ref)
        m_ref[...] = jnp.full(m_ref.shape, NEG, F32)

    row = lax.broadcasted_iota(jnp.int32, (L, L), 0)
    col = lax.broadcasted_iota(jnp.int32, (L, L), 1)
    mask = (row <= col, row >= col)
    k_refs, qt_refs, vt_refs = (kf_ref, kb_ref), (qtf_ref, qtb_ref), (vtf_ref, vtb_ref)
    g = (gf_ref[...], gb_ref[...])
    gt = (g[0].T, g[1].T)
    units = [(d, h) for d in range(2) for h in range(H)]
    twice = lambda r: jnp.concatenate([r, r], axis=1)

    gate = []
    for u, (d, h) in enumerate(units):
        ja, jb, je = h + 2 * H * d, H + h + 2 * H * d, 4 * H + h + H * d
        a_col, a_row = g[d][:, ja:ja + 1], gt[d][ja:ja + 1, :]
        b_row, e_row = gt[d][jb:jb + 1, :], gt[d][je:je + 1, :]
        m_prev = m_ref[u:u + 1, :]
        log_d = jnp.where(mask[d], a_col + b_row, NEG)
        m_inter = b_row + m_prev
        m_t = jnp.maximum(jnp.max(log_d, axis=0, keepdims=True), m_inter)
        lw = e_row + a_row
        m_new = jnp.maximum(e_row + m_prev, jnp.max(lw, axis=-1, keepdims=True))
        m_ref[u:u + 1, :] = m_new
        gate.append(dict(dmat=jnp.exp(log_d - m_t), w_inter=jnp.exp(m_inter - m_t), emt=jnp.exp(-m_t),
                         ws=jnp.exp(lw - m_new), decay=jnp.exp(e_row + m_prev - m_new)))

    ops, pre = [], []
    for u, (d, h) in enumerate(units):
        hs = slice(h * dh, (h + 1) * dh)
        k_u, qt_u, vt_u = k_refs[d][:, hs], qt_refs[d][hs, :], vt_refs[d][hs, :]
        ops.append((k_u, qt_u, vt_u))
        st = jnp.dot(k_u, qt_u, preferred_element_type=F32)
        cq = jnp.dot(ct_ref[u].astype(BF16), qt_u, preferred_element_type=F32)
        n8 = jnp.broadcast_to(n_ref[u:u + 1, :].astype(BF16), (SUBLANES, dh))
        qn = jnp.dot(n8, qt_u, preferred_element_type=F32)[0:1]
        pre.append((st, cq, qn))

    h_t = []
    for u, (d, h) in enumerate(units):
        (k_u, qt_u, vt_u), (st, cq, qn), gq = ops[u], pre[u], gate[u]
        p = st * gq["dmat"]
        den = jnp.sum(p, axis=0, keepdims=True) + gq["w_inter"] * qn
        r = 1.0 / jnp.maximum(jnp.abs(den), gq["emt"])
        num = jnp.dot(vt_u, p.astype(BF16), preferred_element_type=F32) + gq["w_inter"] * cq
        h_t.append(num * r)

    for u, (d, h) in enumerate(units):
        (k_u, qt_u, vt_u), gq = ops[u], gate[u]
        vw = (vt_u.astype(F32) * gq["ws"]).astype(BF16)
        ct_ref[u] = twice(gq["decay"]) * ct_ref[u] + jnp.dot(vw, k_u, preferred_element_type=F32)
        w8 = jnp.broadcast_to(gq["ws"].astype(BF16), (SUBLANES, L))
        n_ref[u:u + 1, :] = (twice(gq["decay"]) * n_ref[u:u + 1, :]
                             + jnp.dot(w8, k_u, preferred_element_type=F32)[0:1])

    chunk = (i, nc - 1 - i)

    @pl.when(i < nc // 2)
    def _park():
        for u, (d, h) in enumerate(units):
            hs_ref[chunk[d], h] = h_t[u]

    @pl.when(i >= nc // 2)
    def _finish():
        for d, (o_ref, y_ref) in enumerate(((of_ref, yhi_ref), (ob_ref, ylo_ref))):
            for h in range(H):
                hs = slice(h * dh, (h + 1) * dh)
                x = h_t[d * H + h] + hs_ref[chunk[d], h]
                mu = jnp.mean(x, axis=0, keepdims=True)
                xc = x - mu
                var = jnp.mean(xc * xc, axis=0, keepdims=True)
                hn = (xc * lax.rsqrt(var + LN_EPS)).T
                y_ref[:, hs] = (jax.nn.sigmoid(o_ref[:, hs].astype(F32)) * hn * gnw_ref[:, hs]).astype(BF16)


def _mlstm_call(k, qt, vt, gates, og, o_col, gnw):
    B, S, W = k.shape
    L = M_CHUNK
    nc = S // L
    assert nc % 2 == 0
    half = nc // 2
    fwd, bwd = (lambda i: i), (lambda i: nc - 1 - i)
    second_f, second_b = (lambda i: jnp.maximum(i, half)), (lambda i: jnp.minimum(nc - 1 - i, half - 1))
    rows = lambda w, ch, c=0: pl.BlockSpec((None, L, w), lambda b, i: (b, ch(i), c))
    cols = lambda ch: pl.BlockSpec((None, W, L), lambda b, i: (b, 0, ch(i)))
    return pl.pallas_call(
        _mlstm_kernel,
        grid=(B, nc),
        in_specs=[rows(W, fwd), rows(W, bwd), cols(fwd), cols(bwd), cols(fwd), cols(bwd),
                  rows(LANES, fwd), rows(LANES, bwd), rows(W, second_f, o_col), rows(W, second_b, o_col),
                  pl.BlockSpec((1, W), lambda b, i: (0, 0))],
        out_specs=[pl.BlockSpec((None, L, W), lambda b, i: (b, jnp.maximum(i - half, 0), 0)),
                   pl.BlockSpec((None, L, W), lambda b, i: (b, jnp.minimum(nc - 1 - i, half - 1), 0))],
        out_shape=[jax.ShapeDtypeStruct((B, S // 2, W), BF16)] * 2,
        scratch_shapes=[pltpu.VMEM((nc, M_HEADS, M_HEAD_DIM, L), F32),
                        pltpu.VMEM((2 * M_HEADS, M_HEAD_DIM, M_HEAD_DIM), F32),
                        pltpu.VMEM((2 * M_HEADS, M_HEAD_DIM), F32),
                        pltpu.VMEM((2 * M_HEADS, L), F32)],
        compiler_params=_params("arbitrary", "arbitrary"),
        name="mlstm",
    )(k, k, qt, qt, vt, vt, gates, gates, og, og, gnw)


ATT_UNROLL = 8


def _attn_kernel(q_ref, k_ref, kp_ref, kn_ref, v_ref, vp_ref, vn_ref, bm_ref, o_ref, lse_ref,
                 *scratch, dil):
    hb = ATT_HALF
    tq = q_ref.shape[1]
    nqb = tq // hb
    i = pl.program_id(1)
    ni = pl.num_programs(1)
    if nqb == 1:
        stage = scratch

        def make_window(main, before, after):
            return lambda r, r0, ls: jnp.concatenate([before[r, :, ls], main[r, :, ls], after[r, :, ls]], axis=0)
    else:
        pads, stage = scratch[:2], scratch[2:]

        def make_window(main, before, after, pad):
            pad[:, 0:hb, :] = before[...]
            pad[:, hb:hb + tq, :] = main[...]
            pad[:, hb + tq:, :] = after[...]
            return lambda r, r0, ls: pad[r, pl.ds(r0, 3 * hb), ls]
    k_window = make_window(k_ref, kp_ref, kn_ref, *(() if nqb == 1 else (pads[0],)))
    v_window = make_window(v_ref, vp_ref, vn_ref, *(() if nqb == 1 else (pads[1],)))
    even = lax.broadcasted_iota(jnp.int32, (hb, LANES), 1) < ATT_HEAD_DIM
    ones = jnp.ones((3 * hb, LANES), BF16)
    if dil > 1:
        os_ref, ls_ref = stage

    npair = ATT_SLOTS // 2
    total = dil * nqb
    step = min(ATT_UNROLL, total)

    def body(it, carry):
        where, scores = [], []
        for j in range(step):
            idx = it * step + j
            r, qb = idx // nqb, idx % nqb
            r0 = pl.multiple_of(qb * hb, hb)
            case = (jnp.logical_and(i == 0, qb == 0).astype(jnp.int32)
                    + 2 * jnp.logical_and(i == ni - 1, qb == nqb - 1).astype(jnp.int32))
            for p in range(npair):
                ls = slice(p * LANES, (p + 1) * LANES)
                q2 = q_ref[r, pl.ds(r0, hb), ls]
                qs = jnp.concatenate([jnp.where(even, q2, 0), jnp.where(even, 0, q2)], axis=0)
                kw = k_window(r, r0, ls)
                s = lax.dot_general(qs, kw, (((1,), (1,)), ((), ())), preferred_element_type=F32)
                scores.append(s + bm_ref[case, p])
                where.append((r, r0, p, ls))
        s = jnp.concatenate(scores, axis=0)
        m = jnp.max(s, axis=-1, keepdims=True)
        e = jnp.exp2(s - m).astype(BF16)
        outs = []
        for u, (r, r0, p, ls) in enumerate(where):
            vw = jnp.concatenate([v_window(r, r0, ls), ones], axis=1)
            outs.append(jnp.dot(e[u * 2 * hb:(u + 1) * 2 * hb], vw, preferred_element_type=F32))
        ol = jnp.concatenate(outs, axis=0)
        l = ol[:, LANES:]
        o = ol[:, :LANES] * (1.0 / l)
        lse = m + jnp.log2(l)
        for u, (r, r0, p, ls) in enumerate(where):
            top, bot = slice(u * 2 * hb, u * 2 * hb + hb), slice(u * 2 * hb + hb, (u + 1) * 2 * hb)
            o2 = jnp.where(even, o[top], o[bot])
            l2 = jnp.where(even, lse[top], lse[bot])
            if dil > 1:
                os_ref[r, pl.ds(r0, hb), ls] = o2
                ls_ref[r, pl.ds(r0, hb), ls] = l2
            else:
                o_ref[p, pl.ds(r0, hb), :] = o2
                lse_ref[p, pl.ds(r0, hb), :] = l2
        return carry

    lax.fori_loop(0, total // step, body, 0)
    if dil > 1:
        for r in range(dil):
            for p in range(ATT_SLOTS // 2):
                ls = slice(p * LANES, (p + 1) * LANES)
                o_ref[p, pl.ds(r, tq, stride=dil), :] = os_ref[r, :, ls]
                lse_ref[p, pl.ds(r, tq, stride=dil), :] = ls_ref[r, :, ls]


def _attn_bias(g):
    h = np.arange(1, ATT_HEADS + 1, dtype=np.float32)
    slopes = np.exp2(-8.0 * h / ATT_HEADS).astype(np.float32).reshape(N_PATTERNS, ATT_SLOTS)[g]
    dil = ATT_PATTERNS[g][1]
    hb = ATT_HALF
    qi = np.arange(hb)[:, None] + hb
    kj = np.arange(3 * hb)[None, :]
    dist = np.abs(qi - kj)
    out = np.empty((4, ATT_SLOTS // 2, 2 * hb, 3 * hb), np.float32)
    for case in range(4):
        valid = dist <= hb
        if case & 1:
            valid = valid & (kj >= hb)
        if case & 2:
            valid = valid & (kj < 2 * hb)
        for p in range(ATT_SLOTS // 2):
            for e in range(2):
                bias = -slopes[2 * p + e] * (dil * dist).astype(np.float32) * np.float32(LOG2E)
                out[case, p, e * hb:(e + 1) * hb] = np.where(valid, bias, np.float32(NEG))
    return jnp.asarray(out)


def _attn_call(zq, g, col0):
    B, dil, n, W = zq.shape
    S = n * dil
    hb = ATT_HALF
    tq = min(n, max(1024 // dil, hb))
    cw = ATT_OUT
    qc, kc, vc = (col0 // cw + t for t in range(3))
    rb, nrb = tq // hb, n // hb

    def main(c):
        return pl.BlockSpec((None, dil, tq, cw), lambda b, i: (b, 0, i, c))

    def prev(c):
        return pl.BlockSpec((None, dil, hb, cw), lambda b, i: (b, 0, jnp.maximum(i * rb - 1, 0), c))

    def nxt(c):
        return pl.BlockSpec((None, dil, hb, cw), lambda b, i: (b, 0, jnp.minimum((i + 1) * rb, nrb - 1), c))

    npair = ATT_SLOTS // 2
    out_spec = pl.BlockSpec((None, npair, tq * dil, LANES), lambda b, i: (b, 0, i, 0))
    scratch = [] if tq == hb else [pltpu.VMEM((dil, tq + 2 * hb, cw), BF16)] * 2
    if dil > 1:
        scratch += [pltpu.VMEM((dil, tq, cw), F32), pltpu.VMEM((dil, tq, cw), F32)]
    return pl.pallas_call(
        functools.partial(_attn_kernel, dil=dil),
        grid=(B, n // tq),
        in_specs=[main(qc), main(kc), prev(kc), nxt(kc), main(vc), prev(vc), nxt(vc),
                  pl.BlockSpec((4, ATT_SLOTS // 2, 2 * hb, 3 * hb), lambda b, i: (0, 0, 0, 0))],
        out_specs=[out_spec, out_spec],
        out_shape=[jax.ShapeDtypeStruct((B, npair, S, LANES), F32)] * 2,
        scratch_shapes=scratch,
        compiler_params=_params("arbitrary", "arbitrary"),
        name=f"attn{g}",
    )(zq, zq, zq, zq, zq, zq, zq, _attn_bias(g))


def _mix_kernel(yhi_ref, ylo_ref, o0_ref, o1_ref, o2_ref, l0_ref, l1_ref, l2_ref, ga_ref, gb_ref,
                h_ref, wa_ref, wb_ref, wo_ref, ada_ref, lng_ref, lnb_ref, hn_ref, u_ref):
    upper = pl.program_id(1) >= pl.num_programs(1) // 2
    ym = jnp.where(upper, yhi_ref[...], ylo_ref[...])
    pairs = []
    for p in range(ATT_SLOTS // 2):
        l0, l1, l2 = l0_ref[p], l1_ref[p], l2_ref[p]
        lm = jnp.maximum(jnp.maximum(l0, l1), l2)
        e0, e1, e2 = jnp.exp2(l0 - lm), jnp.exp2(l1 - lm), jnp.exp2(l2 - lm)
        pairs.append((e0 * o0_ref[p] + e1 * o1_ref[p] + e2 * o2_ref[p]) * (1.0 / (e0 + e1 + e2)))
    att = jnp.concatenate(pairs, axis=-1)
    y_a = jnp.dot(ym, wa_ref[...], preferred_element_type=F32)
    y_b = jnp.dot(att.astype(BF16), wb_ref[...], preferred_element_type=F32)
    mix = (jax.nn.sigmoid(ga_ref[...].astype(F32)) * y_a
           + jax.nn.sigmoid(gb_ref[...].astype(F32)) * y_b)
    y = jnp.dot(mix.astype(BF16), wo_ref[...], preferred_element_type=F32)
    hn = _layer_norm_rows(DN_ALPHA * h_ref[...] + ada_ref[2:3, :] * y) * lng_ref[...] + lnb_ref[...]
    hn_ref[...] = hn
    u_ref[...] = (hn * (1.0 + ada_ref[4:5, :]) + ada_ref[3:4, :]).astype(BF16)


def _mix_call(y_hi, y_lo, att, ga, gb, gb_col, h, wa, wb, wo, ada_l, lng, lnb, tm=256):
    B, S, D = h.shape
    half = S // tm // 2
    row = lambda w, c=0: pl.BlockSpec((None, tm, w), lambda b, i: (b, i, c))
    full = lambda a: pl.BlockSpec(a.shape, lambda b, i: (0,) * a.ndim)
    (o0, l0), (o1, l1), (o2, l2) = att
    return pl.pallas_call(
        _mix_kernel,
        grid=(B, S // tm),
        in_specs=[pl.BlockSpec((None, tm, D), lambda b, i: (b, jnp.maximum(i - half, 0), 0)),
                  pl.BlockSpec((None, tm, D), lambda b, i: (b, jnp.minimum(i, half - 1), 0))]
                 + [pl.BlockSpec((None, ATT_SLOTS // 2, tm, LANES), lambda b, i: (b, 0, i, 0))] * 6
                 + [row(D), row(D, gb_col), row(D),
                  full(wa), full(wb), full(wo),
                  pl.BlockSpec((None, SUBLANES, D), lambda b, i: (b, 0, 0)), full(lng), full(lnb)],
        out_specs=[row(D), row(D)],
        out_shape=[jax.ShapeDtypeStruct((B, S, D), F32), jax.ShapeDtypeStruct((B, S, D), BF16)],
        compiler_params=_params("arbitrary", "arbitrary"),
        name="mix",
    )(y_hi, y_lo, o0, o1, o2, l0, l1, l2, ga, gb, h, wa, wb, wo, ada_l, lng, lnb)


def _ffn_kernel(u_ref, h_ref, w1_ref, w3_ref, w2_ref, ada_ref, lng_ref, lnb_ref, adan_ref,
                hn_ref, *rest, emit_u):
    acc_ref = rest[-1]
    u = u_ref[...]
    for f in range(w1_ref.shape[1] // FF_CHUNK):
        fs = slice(f * FF_CHUNK, (f + 1) * FF_CHUNK)
        a = jnp.dot(u, w1_ref[:, fs], preferred_element_type=F32)
        b = jnp.dot(u, w3_ref[:, fs], preferred_element_type=F32)
        t = (a * jax.nn.sigmoid(a) * b).astype(BF16)
        part = jnp.dot(t, w2_ref[fs, :], preferred_element_type=F32)
        if f == 0:
            acc_ref[...] = part
        else:
            acc_ref[...] += part
    hn = (_layer_norm_rows(DN_ALPHA * h_ref[...] + ada_ref[5:6, :] * acc_ref[...])
          * lng_ref[...] + lnb_ref[...])
    hn_ref[...] = hn
    if emit_u:
        rest[0][...] = (hn * (1.0 + adan_ref[1:2, :]) + adan_ref[0:1, :]).astype(BF16)


def _ffn_call(u, h, w1, w3, w2, ada_l, lng, lnb, ada_next, emit_u, tm=512):
    B, S, D = h.shape
    row = pl.BlockSpec((None, tm, D), lambda b, i: (b, i, 0))
    full = lambda a: pl.BlockSpec(a.shape, lambda b, i: (0,) * a.ndim)
    ada_spec = pl.BlockSpec((None, SUBLANES, D), lambda b, i: (b, 0, 0))
    out_specs = [row, row] if emit_u else [row]
    out_shape = [jax.ShapeDtypeStruct((B, S, D), F32)]
    if emit_u:
        out_shape.append(jax.ShapeDtypeStruct((B, S, D), BF16))
    return pl.pallas_call(
        functools.partial(_ffn_kernel, emit_u=emit_u),
        grid=(B, S // tm),
        in_specs=[row, row, full(w1), full(w3), full(w2), ada_spec, full(lng), full(lnb), ada_spec],
        out_specs=out_specs, out_shape=out_shape,
        scratch_shapes=[pltpu.VMEM((tm, D), F32)],
        compiler_params=_params("arbitrary", "arbitrary"),
        name="ffn",
    )(u, h, w1, w3, w2, ada_l, lng, lnb, ada_next)


def kernel(x, c, w_in, b_gates, conv_w, conv_b, gn_w, w_a, w_b, w_out, w_ada, b_ada, ln1_g, ln1_b,
           w1, w3, w2, ln2_g, ln2_b):
    B, S, D = x.shape
    depth = w_in.shape[0]
    T = B * S
    H = M_HEADS

    c_pad = jnp.zeros((SUBLANES, D), F32).at[:B].set(c)
    ada = _ada_call(c_pad, w_ada, b_ada)
    ada = jnp.pad(ada[:, :, :B].transpose(0, 2, 1, 3), ((0, 0), (0, 0), (0, 2), (0, 0)))

    w_all = _prep_w_in(w_in)
    wvt_all = _prep_w_v_t(w_in, IN_SECTIONS[0], IN_SECTIONS[1])
    h, u = _ln0_call(x, ada[0])
    for l in range(depth):
        bg = jnp.pad(b_gates[l], (0, LANES - 4 * H)).reshape(1, LANES)

        u2d = u.reshape(T, D)
        vt = _proj_t_call(u, wvt_all, l, 512, "proj_vT")
        zq = [_proj_dil_call(u, w_all, l, g, 1024, f"projb{g}") for g in range(N_PATTERNS)]
        gates = _gate_call(u2d, w_all, l, bg).reshape(B, S, LANES)

        cw8 = jnp.pad(conv_w[l], ((0, SUBLANES - CONV_WIDTH), (0, 0)))
        cb = conv_b[l].reshape(1, 2 * M_WIDTH)
        qt, ga = _proj_conv_call(u, w_all, l, W_QK, (W_GA, W_GA + D // 2), D // 2, cw8, cb,
                                 M_HEAD_DIM ** -0.5, True, "proj_conv_q")
        k, ogb = _proj_conv_call(u, w_all, l, W_QK + M_WIDTH, (W_O, W_GB), D, cw8, cb,
                                 1.0, False, "proj_conv_k")
        y_hi, y_lo = _mlstm_call(k, qt, vt, gates, ogb, 0, gn_w[l].reshape(1, M_WIDTH))

        att = [_attn_call(zq[g], g, 0) for g in range(N_PATTERNS)]
        h, u = _mix_call(y_hi, y_lo, att, ga, ogb, 1, h, w_a[l].astype(BF16), w_b[l].astype(BF16),
                         w_out[l].astype(BF16), ada[l], ln1_g[l].reshape(1, D), ln1_b[l].reshape(1, D))
        last = l == depth - 1
        outs = _ffn_call(u, h, w1[l].astype(BF16), w3[l].astype(BF16), w2[l].astype(BF16), ada[l],
                         ln2_g[l].reshape(1, D), ln2_b[l].reshape(1, D),
                         ada[l] if last else ada[l + 1], not last)
        h = outs[0]
        if not last:
            u = outs[1]
    return h
```

```python
import functools

import jax
import jax.numpy as jnp
import numpy as np
from jax import lax
from jax.experimental import pallas as pl
from jax.experimental.pallas import tpu as pltpu

F32 = jnp.float32
BF16 = jnp.bfloat16

D_MODEL = 1024
DEPTH = 2
M_HEADS = 4
M_HEAD_DIM = 256
M_WIDTH = M_HEADS * M_HEAD_DIM
M_CHUNK = 128
CONV_WIDTH = 5
ATT_SLOTS = 8
ATT_HEAD_DIM = 64
ATT_PATTERNS = ((128, 1), (512, 4), (2048, 16))
N_PATTERNS = 3
ATT_HEADS = ATT_SLOTS * N_PATTERNS
ATT_WIDTH = ATT_HEADS * ATT_HEAD_DIM
ATT_OUT = ATT_SLOTS * ATT_HEAD_DIM
ATT_HALF = 64
D_FF = -(-8 * D_MODEL // (3 * 256)) * 256
DN_ALPHA = (2 * DEPTH) ** 0.25
LN_EPS = 1e-5
NEG = -1e30
IN_SECTIONS = (2 * M_WIDTH, M_WIDTH, M_WIDTH, 4 * M_HEADS, 3 * ATT_WIDTH, 2 * D_MODEL)

VMEM_LIMIT_BYTES = 56 * 1024 * 1024
LANES = 128
SUBLANES = 8

LOG2E = 1.4426950408889634
FF_CHUNK = 256
CONV_ROWS = 128


def _params(*sem):
    return pltpu.CompilerParams(dimension_semantics=sem, vmem_limit_bytes=VMEM_LIMIT_BYTES)


def _layer_norm_rows(x):
    mu = jnp.mean(x, axis=-1, keepdims=True)
    xc = x - mu
    var = jnp.mean(xc * xc, axis=-1, keepdims=True)
    return xc * lax.rsqrt(var + LN_EPS)


def _ada_kernel(c_ref, w_ref, b_ref, o_ref):
    c = c_ref[...]
    ca = c * jax.nn.sigmoid(c)
    o_ref[...] = jnp.dot(ca, w_ref[...], precision=lax.Precision.HIGHEST,
                         preferred_element_type=F32) + b_ref[...]


def _ada_call(c_pad, w_ada, b_ada):
    depth = w_ada.shape[0]
    return pl.pallas_call(
        _ada_kernel,
        grid=(depth, 6),
        in_specs=[pl.BlockSpec((SUBLANES, D_MODEL), lambda l, k: (0, 0)),
                  pl.BlockSpec((None, D_MODEL, D_MODEL), lambda l, k: (l, 0, k)),
                  pl.BlockSpec((None, 1, D_MODEL), lambda l, k: (l, 0, k))],
        out_specs=pl.BlockSpec((None, None, SUBLANES, D_MODEL), lambda l, k: (l, k, 0, 0)),
        out_shape=jax.ShapeDtypeStruct((depth, 6, SUBLANES, D_MODEL), F32),
        compiler_params=_params("arbitrary", "arbitrary"),
        name="ada",
    )(c_pad, w_ada, b_ada.reshape(depth, 1, 6 * D_MODEL))


def _ln0_kernel(x_ref, ada_ref, h_ref, u_ref):
    y = _layer_norm_rows(x_ref[...])
    h_ref[...] = y
    u_ref[...] = (y * (1.0 + ada_ref[1:2, :]) + ada_ref[0:1, :]).astype(BF16)


def _ln0_call(x, ada0, ts=512):
    B, S, D = x.shape
    row = pl.BlockSpec((None, ts, D), lambda b, i: (b, i, 0))
    return pl.pallas_call(
        _ln0_kernel,
        grid=(B, S // ts),
        in_specs=[row, pl.BlockSpec((None, SUBLANES, D), lambda b, i: (b, 0, 0))],
        out_specs=[row, row],
        out_shape=[jax.ShapeDtypeStruct((B, S, D), F32), jax.ShapeDtypeStruct((B, S, D), BF16)],
        compiler_params=_params("arbitrary", "arbitrary"),
        name="ln0",
    )(x, ada0)


W_QK, W_O, W_GA, W_GB, W_ATT = 0, 2 * M_WIDTH, 3 * M_WIDTH, 3 * M_WIDTH + D_MODEL, 3 * M_WIDTH + 2 * D_MODEL
W_GATE = W_ATT + 3 * ATT_WIDTH
W_BLOCK = 512
W_ALL = W_GATE + W_BLOCK
PREP_ALIGN = 4 * M_HEADS


def _prep_plan():
    s0, s1, s2, s3, s4, s5 = np.cumsum((0,) + IN_SECTIONS)[:6].tolist()
    runs = [(W_QK, s0, 2 * M_WIDTH, 0), (W_O, s2, M_WIDTH, 0), (W_GA, s5, 2 * D_MODEL, 0),
            (W_ATT, s4, ATT_WIDTH, 1), (W_ATT + ATT_WIDTH, s4 + ATT_WIDTH, 2 * ATT_WIDTH, 0),
            (W_GATE, s3, 4 * M_HEADS, 0)]
    plan = np.zeros((3, W_ALL // W_BLOCK), np.int32)
    for dst, src, width, is_q in runs:
        for k in range(-(-width // W_BLOCK)):
            assert (src + k * W_BLOCK) % PREP_ALIGN == 0
            plan[:, dst // W_BLOCK + k] = ((src + k * W_BLOCK) // PREP_ALIGN, is_q,
                                           min(W_BLOCK, width - k * W_BLOCK))
    return plan


def _prep_kernel(src_ref, isq_ref, valid_ref, wt_ref, o_ref):
    ob = pl.program_id(1)
    scale = jnp.where(isq_ref[ob] == 1, ATT_HEAD_DIM ** -0.5 * LOG2E, 1.0)
    y = wt_ref[0].T * scale
    lane = lax.broadcasted_iota(jnp.int32, y.shape, 1)
    o_ref[...] = jnp.where(lane < valid_ref[ob], y, 0.0).astype(BF16)


def _prep_w_in(w_in_t):
    depth, _, D = w_in_t.shape
    plan = _prep_plan()
    return pl.pallas_call(
        _prep_kernel,
        grid_spec=pltpu.PrefetchScalarGridSpec(
            num_scalar_prefetch=3, grid=(depth, W_ALL // W_BLOCK),
            in_specs=[pl.BlockSpec((pl.Element(1), pl.Element(W_BLOCK), pl.Element(D)),
                                   lambda l, ob, src, *_: (l, src[ob] * PREP_ALIGN, 0))],
            out_specs=pl.BlockSpec((None, D, W_BLOCK), lambda l, ob, *_: (l, 0, ob))),
        out_shape=jax.ShapeDtypeStruct((depth, D, W_ALL), BF16),
        compiler_params=_params("arbitrary", "arbitrary"),
        name="prep_w_in",
    )(*(jnp.asarray(p) for p in plan), w_in_t)


def _w_spec(l, rows, width, block_index):
    return pl.BlockSpec((None, rows, width), lambda *g: (l, 0, block_index(*g)))


def _proj_t_kernel(wt_ref, u_ref, o_ref):
    o_ref[...] = lax.dot_general(wt_ref[...], u_ref[...], (((1,), (1,)), ((), ())),
                                 preferred_element_type=F32).astype(o_ref.dtype)


def _proj_t_call(u, wt_all, l, tm, name):
    B, S, D = u.shape
    N = wt_all.shape[1]
    return pl.pallas_call(
        _proj_t_kernel,
        grid=(B, S // tm),
        in_specs=[pl.BlockSpec((None, N, D), lambda b, i: (l, 0, 0)),
                  pl.BlockSpec((None, tm, D), lambda b, i: (b, i, 0))],
        out_specs=pl.BlockSpec((None, N, tm), lambda b, i: (b, 0, i)),
        out_shape=jax.ShapeDtypeStruct((B, N, S), BF16),
        compiler_params=_params("arbitrary", "arbitrary"),
        name=name,
    )(wt_all, u)


def _cast_kernel(w_ref, o_ref):
    o_ref[...] = w_ref[...].astype(o_ref.dtype)


def _prep_w_v_t(w_in_t, row0, rows, tr=512):
    depth, _, D = w_in_t.shape
    return pl.pallas_call(
        _cast_kernel,
        grid=(depth, rows // tr),
        in_specs=[pl.BlockSpec((None, tr, D), lambda l, j: (l, row0 // tr + j, 0))],
        out_specs=pl.BlockSpec((None, tr, D), lambda l, j: (l, j, 0)),
        out_shape=jax.ShapeDtypeStruct((depth, rows, D), BF16),
        compiler_params=_params("arbitrary", "arbitrary"),
        name="prep_w_vT",
    )(w_in_t)


def _proj_dil_kernel(u_ref, wq_ref, wk_ref, wv_ref, o_ref, *scratch, dil):
    u = u_ref[...]
    rows = u_ref.shape[0] // dil
    gw = 2 * LANES
    j = 0
    for w_ref in (wq_ref, wk_ref, wv_ref):
        for part in range(w_ref.shape[1] // gw):
            acc = jnp.dot(u, w_ref[:, part * gw:(part + 1) * gw], preferred_element_type=F32)
            if dil == 1:
                o_ref[0, :, j * gw:(j + 1) * gw] = acc.astype(o_ref.dtype)
            else:
                acc_ref = scratch[0]
                for c in range(gw // LANES):
                    acc_ref[j, c] = acc[:, c * LANES:(c + 1) * LANES]
                for r in range(dil):
                    for c in range(gw // LANES):
                        o_ref[r, :, j * gw + c * LANES:j * gw + (c + 1) * LANES] = (
                            acc_ref[j, c, pl.ds(r, rows, stride=dil), :].astype(o_ref.dtype))
            j += 1


def _proj_dil_call(u, w_all, l, g, tm, name):
    B, S, D = u.shape
    dil = ATT_PATTERNS[g][1]
    N = 3 * ATT_OUT
    w_specs = [_w_spec(l, D, ATT_OUT, lambda b, i, t=t: (W_ATT + t * ATT_WIDTH) // ATT_OUT + g) for t in range(3)]
    scratch = [] if dil == 1 else [pltpu.VMEM((N // (2 * LANES), 2, tm, LANES), F32)]
    return pl.pallas_call(
        functools.partial(_proj_dil_kernel, dil=dil),
        grid=(B, S // tm),
        in_specs=[pl.BlockSpec((None, tm, D), lambda b, i: (b, i, 0))] + w_specs,
        out_specs=pl.BlockSpec((None, dil, tm // dil, N), lambda b, i: (b, 0, i, 0)),
        out_shape=jax.ShapeDtypeStruct((B, dil, S // dil, N), BF16),
        scratch_shapes=scratch,
        compiler_params=_params("arbitrary", "arbitrary"),
        name=name,
    )(u, w_all, w_all, w_all)


def _gate_kernel(u_ref, w_ref, b_ref, o_ref):
    g = jnp.dot(u_ref[...], w_ref[...], preferred_element_type=F32) + b_ref[...]
    lf = jnp.minimum(g, 0.0) - jnp.log1p(jnp.exp(-jnp.abs(g)))
    row = lax.broadcasted_iota(jnp.int32, (M_CHUNK, LANES), 0)
    lane = lax.broadcasted_iota(jnp.int32, (M_CHUNK, LANES), 1)
    H = M_HEADS
    for c in range(u_ref.shape[0] // M_CHUNK):
        sl = slice(c * M_CHUNK, (c + 1) * M_CHUNK)
        gc, pre, suf = g[sl], lf[sl], lf[sl]
        d = 1
        while d < M_CHUNK:
            pre = pre + jnp.where(row >= d, pltpu.roll(pre, d, 0), 0.0)
            suf = suf + jnp.where(row < M_CHUNK - d, pltpu.roll(suf, M_CHUNK - d, 0), 0.0)
            d *= 2
        a_f = gc - pltpu.roll(pre, LANES - H, 1)
        a_b = gc - pltpu.roll(suf, LANES - H, 1)
        e_f = jnp.broadcast_to(pltpu.roll(pre, 3 * H, 1)[M_CHUNK - 1:M_CHUNK], pre.shape)
        e_b = jnp.broadcast_to(pltpu.roll(suf, 2 * H, 1)[0:1], suf.shape)
        o_ref[sl, :] = jnp.where(lane < H, a_f, jnp.where(lane < 2 * H, pre,
                                 jnp.where(lane < 3 * H, a_b, jnp.where(lane < 4 * H, suf,
                                           jnp.where(lane < 5 * H, e_f, e_b)))))


def _gate_call(u2d, w_all, l, bg, tm=512):
    T, D = u2d.shape
    return pl.pallas_call(
        _gate_kernel,
        grid=(T // tm,),
        in_specs=[pl.BlockSpec((tm, D), lambda i: (i, 0)),
                  _w_spec(l, D, LANES, lambda i: W_GATE // LANES),
                  pl.BlockSpec((1, LANES), lambda i: (0, 0))],
        out_specs=pl.BlockSpec((tm, LANES), lambda i: (i, 0)),
        out_shape=jax.ShapeDtypeStruct((T, LANES), F32),
        compiler_params=_params("arbitrary"),
        name="gates",
    )(u2d, w_all, bg)


def _proj_conv_kernel(u_ref, up_ref, un_ref, wc_ref, wo_ref, cw_ref, cb_ref, y_ref, z_ref, *,
                      scale, transpose_out):
    i = pl.program_id(1)
    tm = u_ref.shape[0]
    hr = up_ref.shape[0]
    before = jnp.where(i > 0, up_ref[...], jnp.zeros_like(up_ref))
    after = jnp.where(i < pl.num_programs(1) - 1, un_ref[...], jnp.zeros_like(un_ref))
    lhs = jnp.concatenate([before, u_ref[...], after], axis=0)
    x = jnp.dot(lhs, wc_ref[...], preferred_element_type=F32)
    z_ref[...] = jnp.dot(u_ref[...], wo_ref[...], preferred_element_type=F32).astype(z_ref.dtype)
    rt, ap = CONV_ROWS, SUBLANES
    for c in range(wc_ref.shape[1] // LANES):
        cs = slice(c * LANES, (c + 1) * LANES)
        for rb in range(tm // rt):
            r0 = hr + rb * rt
            blk = x[r0 - ap:r0 + rt + ap, cs]
            y = cb_ref[:, cs]
            for j in range(CONV_WIDTH):
                k = j - CONV_WIDTH // 2
                xs = blk if k == 0 else pltpu.roll(blk, (-k) % (rt + 2 * ap), 0)
                y = y + cw_ref[j:j + 1, cs] * xs[ap:ap + rt]
            y = y * jax.nn.sigmoid(y)
            if scale != 1.0:
                y = y * scale
            if transpose_out:
                y_ref[cs, rb * rt:(rb + 1) * rt] = y.T.astype(BF16)
            else:
                y_ref[rb * rt:(rb + 1) * rt, cs] = y.astype(BF16)


def _proj_conv_call(u, w_all, l, col0, wo_cols, to, conv_w8, conv_b, scale, transpose_out, name,
                    tm=1024, tc=512):
    B, S, D = u.shape
    nj = M_WIDTH // tc
    assert len(wo_cols) == nj and all(c % to == 0 for c in wo_cols)
    wo_index = lambda j: sum(jnp.where(j == jj, c // to, 0) for jj, c in enumerate(wo_cols))
    hr = 2 * SUBLANES
    rb, nrb = tm // hr, S // hr
    c0 = col0 // tc
    if transpose_out:
        y_spec = pl.BlockSpec((None, tc, tm), lambda b, i, j: (b, j, i))
        y_shape = jax.ShapeDtypeStruct((B, M_WIDTH, S), BF16)
    else:
        y_spec = pl.BlockSpec((None, tm, tc), lambda b, i, j: (b, i, j))
        y_shape = jax.ShapeDtypeStruct((B, S, M_WIDTH), BF16)
    return pl.pallas_call(
        functools.partial(_proj_conv_kernel, scale=scale, transpose_out=transpose_out),
        grid=(B, S // tm, nj),
        in_specs=[pl.BlockSpec((None, tm, D), lambda b, i, j: (b, i, 0)),
                  pl.BlockSpec((None, hr, D), lambda b, i, j: (b, jnp.maximum(i * rb - 1, 0), 0)),
                  pl.BlockSpec((None, hr, D), lambda b, i, j: (b, jnp.minimum((i + 1) * rb, nrb - 1), 0)),
                  _w_spec(l, D, tc, lambda b, i, j: c0 + j),
                  _w_spec(l, D, to, lambda b, i, j: wo_index(j)),
                  pl.BlockSpec((SUBLANES, tc), lambda b, i, j: (0, c0 + j)),
                  pl.BlockSpec((1, tc), lambda b, i, j: (0, c0 + j))],
        out_specs=[y_spec, pl.BlockSpec((None, tm, to), lambda b, i, j: (b, i, j))],
        out_shape=[y_shape, jax.ShapeDtypeStruct((B, S, nj * to), BF16)],
        compiler_params=_params("arbitrary", "arbitrary", "arbitrary"),
        name=name,
    )(u, u, u, w_all, w_all, conv_w8, conv_b)


def _mlstm_kernel(kf_ref, kb_ref, qtf_ref, qtb_ref, vtf_ref, vtb_ref, gf_ref, gb_ref, of_ref, ob_ref,
                  gnw_ref, yhi_ref, ylo_ref, hs_ref, ct_ref, n_ref, m_ref):
    L, H, dh = M_CHUNK, M_HEADS, M_HEAD_DIM
    i = pl.program_id(1)
    nc = pl.num_programs(1)

    @pl.when(i == 0)
    def _init():
        ct_ref[...] = jnp.zeros_like(ct_ref)
        n_ref[...] = jnp.zeros_like(n_ref)
        m_ref[...] = jnp.full(m_ref.shape, NEG, F32)

    row = lax.broadcasted_iota(jnp.int32, (L, L), 0)
    col = lax.broadcasted_iota(jnp.int32, (L, L), 1)
    mask = (row <= col, row >= col)
    k_refs, qt_refs, vt_refs = (kf_ref, kb_ref), (qtf_ref, qtb_ref), (vtf_ref, vtb_ref)
    g = (gf_ref[...], gb_ref[...])
    gt = (g[0].T, g[1].T)
    units = [(d, h) for d in range(2) for h in range(H)]
    twice = lambda r: jnp.concatenate([r, r], axis=1)

    gate = []
    for u, (d, h) in enumerate(units):
        ja, jb, je = h + 2 * H * d, H + h + 2 * H * d, 4 * H + h + H * d
        a_col, a_row = g[d][:, ja:ja + 1], gt[d][ja:ja + 1, :]
        b_row, e_row = gt[d][jb:jb + 1, :], gt[d][je:je + 1, :]
        m_prev = m_ref[u:u + 1, :]
        log_d = jnp.where(mask[d], a_col + b_row, NEG)
        m_inter = b_row + m_prev
        m_t = jnp.maximum(jnp.max(log_d, axis=0, keepdims=True), m_inter)
        lw = e_row + a_row
        m_new = jnp.maximum(e_row + m_prev, jnp.max(lw, axis=-1, keepdims=True))
        m_ref[u:u + 1, :] = m_new
        gate.append(dict(dmat=jnp.exp(log_d - m_t), w_inter=jnp.exp(m_inter - m_t), emt=jnp.exp(-m_t),
                         ws=jnp.exp(lw - m_new), decay=jnp.exp(e_row + m_prev - m_new)))

    ops, pre = [], []
    for u, (d, h) in enumerate(units):
        hs = slice(h * dh, (h + 1) * dh)
        k_u, qt_u, vt_u = k_refs[d][:, hs], qt_refs[d][hs, :], vt_refs[d][hs, :]
        ops.append((k_u, qt_u, vt_u))
        st = jnp.dot(k_u, qt_u, preferred_element_type=F32)
        cq = jnp.dot(ct_ref[u].astype(BF16), qt_u, preferred_element_type=F32)
        n8 = jnp.broadcast_to(n_ref[u:u + 1, :].astype(BF16), (SUBLANES, dh))
        qn = jnp.dot(n8, qt_u, preferred_element_type=F32)[0:1]
        pre.append((st, cq, qn))

    h_t = []
    for u, (d, h) in enumerate(units):
        (k_u, qt_u, vt_u), (st, cq, qn), gq = ops[u], pre[u], gate[u]
        p = st * gq["dmat"]
        den = jnp.sum(p, axis=0, keepdims=True) + gq["w_inter"] * qn
        r = 1.0 / jnp.maximum(jnp.abs(den), gq["emt"])
        num = jnp.dot(vt_u, p.astype(BF16), preferred_element_type=F32) + gq["w_inter"] * cq
        h_t.append(num * r)

    for u, (d, h) in enumerate(units):
        (k_u, qt_u, vt_u), gq = ops[u], gate[u]
        vw = (vt_u.astype(F32) * gq["ws"]).astype(BF16)
        ct_ref[u] = twice(gq["decay"]) * ct_ref[u] + jnp.dot(vw, k_u, preferred_element_type=F32)
        w8 = jnp.broadcast_to(gq["ws"].astype(BF16), (SUBLANES, L))
        n_ref[u:u + 1, :] = (twice(gq["decay"]) * n_ref[u:u + 1, :]
                             + jnp.dot(w8, k_u, preferred_element_type=F32)[0:1])

    chunk = (i, nc - 1 - i)

    @pl.when(i < nc // 2)
    def _park():
        for u, (d, h) in enumerate(units):
            hs_ref[chunk[d], h] = h_t[u]

    @pl.when(i >= nc // 2)
    def _finish():
        for d, (o_ref, y_ref) in enumerate(((of_ref, yhi_ref), (ob_ref, ylo_ref))):
            for h in range(H):
                hs = slice(h * dh, (h + 1) * dh)
                x = h_t[d * H + h] + hs_ref[chunk[d], h]
                mu = jnp.mean(x, axis=0, keepdims=True)
                xc = x - mu
                var = jnp.mean(xc * xc, axis=0, keepdims=True)
                hn = (xc * lax.rsqrt(var + LN_EPS)).T
                y_ref[:, hs] = (jax.nn.sigmoid(o_ref[:, hs].astype(F32)) * hn * gnw_ref[:, hs]).astype(BF16)


def _mlstm_call(k, qt, vt, gates, og, o_col, gnw):
    B, S, W = k.shape
    L = M_CHUNK
    nc = S // L
    assert nc % 2 == 0
    half = nc // 2
    fwd, bwd = (lambda i: i), (lambda i: nc - 1 - i)
    second_f, second_b = (lambda i: jnp.maximum(i, half)), (lambda i: jnp.minimum(nc - 1 - i, half - 1))
    rows = lambda w, ch, c=0: pl.BlockSpec((None, L, w), lambda b, i: (b, ch(i), c))
    cols = lambda ch: pl.BlockSpec((None, W, L), lambda b, i: (b, 0, ch(i)))
    return pl.pallas_call(
        _mlstm_kernel,
        grid=(B, nc),
        in_specs=[rows(W, fwd), rows(W, bwd), cols(fwd), cols(bwd), cols(fwd), cols(bwd),
                  rows(LANES, fwd), rows(LANES, bwd), rows(W, second_f, o_col), rows(W, second_b, o_col),
                  pl.BlockSpec((1, W), lambda b, i: (0, 0))],
        out_specs=[pl.BlockSpec((None, L, W), lambda b, i: (b, jnp.maximum(i - half, 0), 0)),
                   pl.BlockSpec((None, L, W), lambda b, i: (b, jnp.minimum(nc - 1 - i, half - 1), 0))],
        out_shape=[jax.ShapeDtypeStruct((B, S // 2, W), BF16)] * 2,
        scratch_shapes=[pltpu.VMEM((nc, M_HEADS, M_HEAD_DIM, L), F32),
                        pltpu.VMEM((2 * M_HEADS, M_HEAD_DIM, M_HEAD_DIM), F32),
                        pltpu.VMEM((2 * M_HEADS, M_HEAD_DIM), F32),
                        pltpu.VMEM((2 * M_HEADS, L), F32)],
        compiler_params=_params("arbitrary", "arbitrary"),
        name="mlstm",
    )(k, k, qt, qt, vt, vt, gates, gates, og, og, gnw)


ATT_UNROLL = 8


def _attn_kernel(q_ref, k_ref, kp_ref, kn_ref, v_ref, vp_ref, vn_ref, bm_ref, o_ref, lse_ref,
                 *scratch, dil):
    hb = ATT_HALF
    tq = q_ref.shape[1]
    nqb = tq // hb
    i = pl.program_id(1)
    ni = pl.num_programs(1)
    if nqb == 1:
        stage = scratch

        def make_window(main, before, after):
            return lambda r, r0, ls: jnp.concatenate([before[r, :, ls], main[r, :, ls], after[r, :, ls]], axis=0)
    else:
        pads, stage = scratch[:2], scratch[2:]

        def make_window(main, before, after, pad):
            pad[:, 0:hb, :] = before[...]
            pad[:, hb:hb + tq, :] = main[...]
            pad[:, hb + tq:, :] = after[...]
            return lambda r, r0, ls: pad[r, pl.ds(r0, 3 * hb), ls]
    k_window = make_window(k_ref, kp_ref, kn_ref, *(() if nqb == 1 else (pads[0],)))
    v_window = make_window(v_ref, vp_ref, vn_ref, *(() if nqb == 1 else (pads[1],)))
    even = lax.broadcasted_iota(jnp.int32, (hb, LANES), 1) < ATT_HEAD_DIM
    ones = jnp.ones((3 * hb, LANES), BF16)
    if dil > 1:
        os_ref, ls_ref = stage

    npair = ATT_SLOTS // 2
    total = dil * nqb
    step = min(ATT_UNROLL, total)

    def body(it, carry):
        where, scores = [], []
        for j in range(step):
            idx = it * step + j
            r, qb = idx // nqb, idx % nqb
            r0 = pl.multiple_of(qb * hb, hb)
            case = (jnp.logical_and(i == 0, qb == 0).astype(jnp.int32)
                    + 2 * jnp.logical_and(i == ni - 1, qb == nqb - 1).astype(jnp.int32))
            for p in range(npair):
                ls = slice(p * LANES, (p + 1) * LANES)
                q2 = q_ref[r, pl.ds(r0, hb), ls]
                qs = jnp.concatenate([jnp.where(even, q2, 0), jnp.where(even, 0, q2)], axis=0)
                kw = k_window(r, r0, ls)
                s = lax.dot_general(qs, kw, (((1,), (1,)), ((), ())), preferred_element_type=F32)
                scores.append(s + bm_ref[case, p])
                where.append((r, r0, p, ls))
        s = jnp.concatenate(scores, axis=0)
        m = jnp.max(s, axis=-1, keepdims=True)
        e = jnp.exp2(s - m).astype(BF16)
        outs = []
        for u, (r, r0, p, ls) in enumerate(where):
            vw = jnp.concatenate([v_window(r, r0, ls), ones], axis=1)
            outs.append(jnp.dot(e[u * 2 * hb:(u + 1) * 2 * hb], vw, preferred_element_type=F32))
        ol = jnp.concatenate(outs, axis=0)
        l = ol[:, LANES:]
        o = ol[:, :LANES] * (1.0 / l)
        lse = m + jnp.log2(l)
        for u, (r, r0, p, ls) in enumerate(where):
            top, bot = slice(u * 2 * hb, u * 2 * hb + hb), slice(u * 2 * hb + hb, (u + 1) * 2 * hb)
            o2 = jnp.where(even, o[top], o[bot])
            l2 = jnp.where(even, lse[top], lse[bot])
            if dil > 1:
                os_ref[r, pl.ds(r0, hb), ls] = o2
                ls_ref[r, pl.ds(r0, hb), ls] = l2
            else:
                o_ref[p, pl.ds(r0, hb), :] = o2
                lse_ref[p, pl.ds(r0, hb), :] = l2
        return carry

    lax.fori_loop(0, total // step, body, 0)
    if dil > 1:
        for r in range(dil):
            for p in range(ATT_SLOTS // 2):
                ls = slice(p * LANES, (p + 1) * LANES)
                o_ref[p, pl.ds(r, tq, stride=dil), :] = os_ref[r, :, ls]
                lse_ref[p, pl.ds(r, tq, stride=dil), :] = ls_ref[r, :, ls]


def _attn_bias(g):
    h = np.arange(1, ATT_HEADS + 1, dtype=np.float32)
    slopes = np.exp2(-8.0 * h / ATT_HEADS).astype(np.float32).reshape(N_PATTERNS, ATT_SLOTS)[g]
    dil = ATT_PATTERNS[g][1]
    hb = ATT_HALF
    qi = np.arange(hb)[:, None] + hb
    kj = np.arange(3 * hb)[None, :]
    dist = np.abs(qi - kj)
    out = np.empty((4, ATT_SLOTS // 2, 2 * hb, 3 * hb), np.float32)
    for case in range(4):
        valid = dist <= hb
        if case & 1:
            valid = valid & (kj >= hb)
        if case & 2:
            valid = valid & (kj < 2 * hb)
        for p in range(ATT_SLOTS // 2):
            for e in range(2):
                bias = -slopes[2 * p + e] * (dil * dist).astype(np.float32) * np.float32(LOG2E)
                out[case, p, e * hb:(e + 1) * hb] = np.where(valid, bias, np.float32(NEG))
    return jnp.asarray(out)


def _attn_call(zq, g, col0):
    B, dil, n, W = zq.shape
    S = n * dil
    hb = ATT_HALF
    tq = min(n, max(1024 // dil, hb))
    cw = ATT_OUT
    qc, kc, vc = (col0 // cw + t for t in range(3))
    rb, nrb = tq // hb, n // hb

    def main(c):
        return pl.BlockSpec((None, dil, tq, cw), lambda b, i: (b, 0, i, c))

    def prev(c):
        return pl.BlockSpec((None, dil, hb, cw), lambda b, i: (b, 0, jnp.maximum(i * rb - 1, 0), c))

    def nxt(c):
        return pl.BlockSpec((None, dil, hb, cw), lambda b, i: (b, 0, jnp.minimum((i + 1) * rb, nrb - 1), c))

    npair = ATT_SLOTS // 2
    out_spec = pl.BlockSpec((None, npair, tq * dil, LANES), lambda b, i: (b, 0, i, 0))
    scratch = [] if tq == hb else [pltpu.VMEM((dil, tq + 2 * hb, cw), BF16)] * 2
    if dil > 1:
        scratch += [pltpu.VMEM((dil, tq, cw), F32), pltpu.VMEM((dil, tq, cw), F32)]
    return pl.pallas_call(
        functools.partial(_attn_kernel, dil=dil),
        grid=(B, n // tq),
        in_specs=[main(qc), main(kc), prev(kc), nxt(kc), main(vc), prev(vc), nxt(vc),
                  pl.BlockSpec((4, ATT_SLOTS // 2, 2 * hb, 3 * hb), lambda b, i: (0, 0, 0, 0))],
        out_specs=[out_spec, out_spec],
        out_shape=[jax.ShapeDtypeStruct((B, npair, S, LANES), F32)] * 2,
        scratch_shapes=scratch,
        compiler_params=_params("arbitrary", "arbitrary"),
        name=f"attn{g}",
    )(zq, zq, zq, zq, zq, zq, zq, _attn_bias(g))


def _mix_kernel(yhi_ref, ylo_ref, o0_ref, o1_ref, o2_ref, l0_ref, l1_ref, l2_ref, ga_ref, gb_ref,
                h_ref, wa_ref, wb_ref, wo_ref, ada_ref, lng_ref, lnb_ref, hn_ref, u_ref):
    upper = pl.program_id(1) >= pl.num_programs(1) // 2
    ym = jnp.where(upper, yhi_ref[...], ylo_ref[...])
    pairs = []
    for p in range(ATT_SLOTS // 2):
        l0, l1, l2 = l0_ref[p], l1_ref[p], l2_ref[p]
        lm = jnp.maximum(jnp.maximum(l0, l1), l2)
        e0, e1, e2 = jnp.exp2(l0 - lm), jnp.exp2(l1 - lm), jnp.exp2(l2 - lm)
        pairs.append((e0 * o0_ref[p] + e1 * o1_ref[p] + e2 * o2_ref[p]) * (1.0 / (e0 + e1 + e2)))
    att = jnp.concatenate(pairs, axis=-1)
    y_a = jnp.dot(ym, wa_ref[...], preferred_element_type=F32)
    y_b = jnp.dot(att.astype(BF16), wb_ref[...], preferred_element_type=F32)
    mix = (jax.nn.sigmoid(ga_ref[...].astype(F32)) * y_a
           + jax.nn.sigmoid(gb_ref[...].astype(F32)) * y_b)
    y = jnp.dot(mix.astype(BF16), wo_ref[...], preferred_element_type=F32)
    hn = _layer_norm_rows(DN_ALPHA * h_ref[...] + ada_ref[2:3, :] * y) * lng_ref[...] + lnb_ref[...]
    hn_ref[...] = hn
    u_ref[...] = (hn * (1.0 + ada_ref[4:5, :]) + ada_ref[3:4, :]).astype(BF16)


def _mix_call(y_hi, y_lo, att, ga, gb, gb_col, h, wa, wb, wo, ada_l, lng, lnb, tm=256):
    B, S, D = h.shape
    half = S // tm // 2
    row = lambda w, c=0: pl.BlockSpec((None, tm, w), lambda b, i: (b, i, c))
    full = lambda a: pl.BlockSpec(a.shape, lambda b, i: (0,) * a.ndim)
    (o0, l0), (o1, l1), (o2, l2) = att
    return pl.pallas_call(
        _mix_kernel,
        grid=(B, S // tm),
        in_specs=[pl.BlockSpec((None, tm, D), lambda b, i: (b, jnp.maximum(i - half, 0), 0)),
                  pl.BlockSpec((None, tm, D), lambda b, i: (b, jnp.minimum(i, half - 1), 0))]
                 + [pl.BlockSpec((None, ATT_SLOTS // 2, tm, LANES), lambda b, i: (b, 0, i, 0))] * 6
                 + [row(D), row(D, gb_col), row(D),
                  full(wa), full(wb), full(wo),
                  pl.BlockSpec((None, SUBLANES, D), lambda b, i: (b, 0, 0)), full(lng), full(lnb)],
        out_specs=[row(D), row(D)],
        out_shape=[jax.ShapeDtypeStruct((B, S, D), F32), jax.ShapeDtypeStruct((B, S, D), BF16)],
        compiler_params=_params("arbitrary", "arbitrary"),
        name="mix",
    )(y_hi, y_lo, o0, o1, o2, l0, l1, l2, ga, gb, h, wa, wb, wo, ada_l, lng, lnb)


def _ffn_kernel(u_ref, h_ref, w1_ref, w3_ref, w2_ref, ada_ref, lng_ref, lnb_ref, adan_ref,
                hn_ref, *rest, emit_u):
    acc_ref = rest[-1]
    u = u_ref[...]
    for f in range(w1_ref.shape[1] // FF_CHUNK):
        fs = slice(f * FF_CHUNK, (f + 1) * FF_CHUNK)
        a = jnp.dot(u, w1_ref[:, fs], preferred_element_type=F32)
        b = jnp.dot(u, w3_ref[:, fs], preferred_element_type=F32)
        t = (a * jax.nn.sigmoid(a) * b).astype(BF16)
        part = jnp.dot(t, w2_ref[fs, :], preferred_element_type=F32)
        if f == 0:
            acc_ref[...] = part
        else:
            acc_ref[...] += part
    hn = (_layer_norm_rows(DN_ALPHA * h_ref[...] + ada_ref[5:6, :] * acc_ref[...])
          * lng_ref[...] + lnb_ref[...])
    hn_ref[...] = hn
    if emit_u:
        rest[0][...] = (hn * (1.0 + adan_ref[1:2, :]) + adan_ref[0:1, :]).astype(BF16)


def _ffn_call(u, h, w1, w3, w2, ada_l, lng, lnb, ada_next, emit_u, tm=512):
    B, S, D = h.shape
    row = pl.BlockSpec((None, tm, D), lambda b, i: (b, i, 0))
    full = lambda a: pl.BlockSpec(a.shape, lambda b, i: (0,) * a.ndim)
    ada_spec = pl.BlockSpec((None, SUBLANES, D), lambda b, i: (b, 0, 0))
    out_specs = [row, row] if emit_u else [row]
    out_shape = [jax.ShapeDtypeStruct((B, S, D), F32)]
    if emit_u:
        out_shape.append(jax.ShapeDtypeStruct((B, S, D), BF16))
    return pl.pallas_call(
        functools.partial(_ffn_kernel, emit_u=emit_u),
        grid=(B, S // tm),
        in_specs=[row, row, full(w1), full(w3), full(w2), ada_spec, full(lng), full(lnb), ada_spec],
        out_specs=out_specs, out_shape=out_shape,
        scratch_shapes=[pltpu.VMEM((tm, D), F32)],
        compiler_params=_params("arbitrary", "arbitrary"),
        name="ffn",
    )(u, h, w1, w3, w2, ada_l, lng, lnb, ada_next)


def kernel(x, c, w_in, b_gates, conv_w, conv_b, gn_w, w_a, w_b, w_out, w_ada, b_ada, ln1_g, ln1_b,
           w1, w3, w2, ln2_g, ln2_b):
    B, S, D = x.shape
    depth = w_in.shape[0]
    T = B * S
    H = M_HEADS

    c_pad = jnp.zeros((SUBLANES, D), F32).at[:B].set(c)
    ada = _ada_call(c_pad, w_ada, b_ada)
    ada = jnp.pad(ada[:, :, :B].transpose(0, 2, 1, 3), ((0, 0), (0, 0), (0, 2), (0, 0)))

    w_in_t = jnp.swapaxes(w_in, 1, 2)
    w_all = _prep_w_in(w_in_t)
    wvt_all = _prep_w_v_t(w_in_t, IN_SECTIONS[0], IN_SECTIONS[1])
    h, u = _ln0_call(x, ada[0])
    for l in range(depth):
        bg = jnp.pad(b_gates[l], (0, LANES - 4 * H)).reshape(1, LANES)

        u2d = u.reshape(T, D)
        vt = _proj_t_call(u, wvt_all, l, 512, "proj_vT")
        zq = [_proj_dil_call(u, w_all, l, g, 1024, f"projb{g}") for g in range(N_PATTERNS)]
        gates = _gate_call(u2d, w_all, l, bg).reshape(B, S, LANES)

        cw8 = jnp.pad(conv_w[l], ((0, SUBLANES - CONV_WIDTH), (0, 0)))
        cb = conv_b[l].reshape(1, 2 * M_WIDTH)
        qt, ga = _proj_conv_call(u, w_all, l, W_QK, (W_GA, W_GA + D // 2), D // 2, cw8, cb,
                                 M_HEAD_DIM ** -0.5, True, "proj_conv_q")
        k, ogb = _proj_conv_call(u, w_all, l, W_QK + M_WIDTH, (W_O, W_GB), D, cw8, cb,
                                 1.0, False, "proj_conv_k")
        y_hi, y_lo = _mlstm_call(k, qt, vt, gates, ogb, 0, gn_w[l].reshape(1, M_WIDTH))

        att = [_attn_call(zq[g], g, 0) for g in range(N_PATTERNS)]
        h, u = _mix_call(y_hi, y_lo, att, ga, ogb, 1, h, w_a[l].astype(BF16), w_b[l].astype(BF16),
                         w_out[l].astype(BF16), ada[l], ln1_g[l].reshape(1, D), ln1_b[l].reshape(1, D))
        last = l == depth - 1
        outs = _ffn_call(u, h, w1[l].astype(BF16), w3[l].astype(BF16), w2[l].astype(BF16), ada[l],
                         ln2_g[l].reshape(1, D), ln2_b[l].reshape(1, D),
                         ada[l] if last else ada[l + 1], not last)
        h = outs[0]
        if not last:
            u = outs[1]
    return h
```

```python
import functools

import jax
import jax.numpy as jnp
import numpy as np
from jax import lax
from jax.experimental import pallas as pl
from jax.experimental.pallas import tpu as pltpu

F32 = jnp.float32
BF16 = jnp.bfloat16

D_MODEL = 1024
DEPTH = 2
M_HEADS = 4
M_HEAD_DIM = 256
M_WIDTH = M_HEADS * M_HEAD_DIM
M_CHUNK = 128
CONV_WIDTH = 5
ATT_SLOTS = 8
ATT_HEAD_DIM = 64
ATT_PATTERNS = ((128, 1), (512, 4), (2048, 16))
N_PATTERNS = 3
ATT_HEADS = ATT_SLOTS * N_PATTERNS
ATT_WIDTH = ATT_HEADS * ATT_HEAD_DIM
ATT_OUT = ATT_SLOTS * ATT_HEAD_DIM
ATT_HALF = 64
D_FF = -(-8 * D_MODEL // (3 * 256)) * 256
DN_ALPHA = (2 * DEPTH) ** 0.25
LN_EPS = 1e-5
NEG = -1e30
IN_SECTIONS = (2 * M_WIDTH, M_WIDTH, M_WIDTH, 4 * M_HEADS, 3 * ATT_WIDTH, 2 * D_MODEL)

VMEM_LIMIT_BYTES = 56 * 1024 * 1024
LANES = 128
SUBLANES = 8

LOG2E = 1.4426950408889634

ROWS_STREAM = 512
ROWS_PROJ = 1024
COLS_CONV = 512
ROWS_MIX = 512
ROWS_FFN = 512
FF_CHUNK = 256
CONV_ROWS = 128


def _params(*sem):
    return pltpu.CompilerParams(dimension_semantics=sem, vmem_limit_bytes=VMEM_LIMIT_BYTES)


def _layer_norm_rows(x):
    mu = jnp.mean(x, axis=-1, keepdims=True)
    xc = x - mu
    var = jnp.mean(xc * xc, axis=-1, keepdims=True)
    return xc * lax.rsqrt(var + LN_EPS)


def _ada_kernel(c_ref, w_ref, b_ref, o_ref):
    c = c_ref[...]
    ca = c * jax.nn.sigmoid(c)
    o_ref[...] = jnp.dot(ca, w_ref[...], precision=lax.Precision.HIGHEST,
                         preferred_element_type=F32) + b_ref[...]


def _ada_call(c_pad, w_ada, b_ada):
    depth = w_ada.shape[0]
    return pl.pallas_call(
        _ada_kernel,
        grid=(depth, 6),
        in_specs=[pl.BlockSpec((SUBLANES, D_MODEL), lambda l, k: (0, 0)),
                  pl.BlockSpec((None, D_MODEL, D_MODEL), lambda l, k: (l, 0, k)),
                  pl.BlockSpec((None, 1, D_MODEL), lambda l, k: (l, 0, k))],
        out_specs=pl.BlockSpec((None, None, SUBLANES, D_MODEL), lambda l, k: (l, k, 0, 0)),
        out_shape=jax.ShapeDtypeStruct((depth, 6, SUBLANES, D_MODEL), F32),
        compiler_params=_params("arbitrary", "arbitrary"),
        name="ada",
    )(c_pad, w_ada, b_ada.reshape(depth, 1, 6 * D_MODEL))


def _ln0_kernel(x_ref, ada_ref, h_ref, u_ref):
    y = _layer_norm_rows(x_ref[...])
    h_ref[...] = y
    u_ref[...] = (y * (1.0 + ada_ref[1:2, :]) + ada_ref[0:1, :]).astype(BF16)


def _ln0_call(x, ada0, ts=ROWS_STREAM):
    B, S, D = x.shape
    row = pl.BlockSpec((None, ts, D), lambda b, i: (b, i, 0))
    return pl.pallas_call(
        _ln0_kernel,
        grid=(B, S // ts),
        in_specs=[row, pl.BlockSpec((None, SUBLANES, D), lambda b, i: (b, 0, 0))],
        out_specs=[row, row],
        out_shape=[jax.ShapeDtypeStruct((B, S, D), F32), jax.ShapeDtypeStruct((B, S, D), BF16)],
        compiler_params=_params("arbitrary", "arbitrary"),
        name="ln0",
    )(x, ada0)


W_QK, W_O, W_GA, W_GB, W_ATT = 0, 2 * M_WIDTH, 3 * M_WIDTH, 3 * M_WIDTH + D_MODEL, 3 * M_WIDTH + 2 * D_MODEL
W_GATE = W_ATT + 3 * ATT_WIDTH
W_BLOCK = 512
W_ALL = W_GATE + W_BLOCK
PREP_ALIGN = 4 * M_HEADS


def _prep_plan():
    s0, s1, s2, s3, s4, s5 = np.cumsum((0,) + IN_SECTIONS)[:6].tolist()
    runs = [(W_QK, s0, 2 * M_WIDTH, 0), (W_O, s2, M_WIDTH, 0), (W_GA, s5, 2 * D_MODEL, 0),
            (W_ATT, s4, ATT_WIDTH, 1), (W_ATT + ATT_WIDTH, s4 + ATT_WIDTH, 2 * ATT_WIDTH, 0),
            (W_GATE, s3, 4 * M_HEADS, 0)]
    plan = np.zeros((3, W_ALL // W_BLOCK), np.int32)
    for dst, src, width, is_q in runs:
        for k in range(-(-width // W_BLOCK)):
            assert (src + k * W_BLOCK) % PREP_ALIGN == 0
            plan[:, dst // W_BLOCK + k] = ((src + k * W_BLOCK) // PREP_ALIGN, is_q,
                                           min(W_BLOCK, width - k * W_BLOCK))
    return plan


def _prep_kernel(src_ref, isq_ref, valid_ref, wt_ref, o_ref):
    ob = pl.program_id(1)
    scale = jnp.where(isq_ref[ob] == 1, ATT_HEAD_DIM ** -0.5 * LOG2E, 1.0)
    y = wt_ref[0].T * scale
    lane = lax.broadcasted_iota(jnp.int32, y.shape, 1)
    o_ref[...] = jnp.where(lane < valid_ref[ob], y, 0.0).astype(BF16)


def _prep_w_in(w_in_t):
    depth, _, D = w_in_t.shape
    plan = _prep_plan()
    return pl.pallas_call(
        _prep_kernel,
        grid_spec=pltpu.PrefetchScalarGridSpec(
            num_scalar_prefetch=3, grid=(depth, W_ALL // W_BLOCK),
            in_specs=[pl.BlockSpec((pl.Element(1), pl.Element(W_BLOCK), pl.Element(D)),
                                   lambda l, ob, src, *_: (l, src[ob] * PREP_ALIGN, 0))],
            out_specs=pl.BlockSpec((None, D, W_BLOCK), lambda l, ob, *_: (l, 0, ob))),
        out_shape=jax.ShapeDtypeStruct((depth, D, W_ALL), BF16),
        compiler_params=_params("arbitrary", "arbitrary"),
        name="prep_w_in",
    )(*(jnp.asarray(p) for p in plan), w_in_t)


def _w_spec(l, rows, width, block_index):
    return pl.BlockSpec((None, rows, width), lambda *g: (l, 0, block_index(*g)))


def _proj_t_kernel(wt_ref, u_ref, o_ref):
    o_ref[...] = lax.dot_general(wt_ref[...], u_ref[...], (((1,), (1,)), ((), ())),
                                 preferred_element_type=F32).astype(o_ref.dtype)


def _proj_t_call(u, wt_all, l, tm, name):
    B, S, D = u.shape
    N = wt_all.shape[1]
    return pl.pallas_call(
        _proj_t_kernel,
        grid=(B, S // tm),
        in_specs=[pl.BlockSpec((None, N, D), lambda b, i: (l, 0, 0)),
                  pl.BlockSpec((None, tm, D), lambda b, i: (b, i, 0))],
        out_specs=pl.BlockSpec((None, N, tm), lambda b, i: (b, 0, i)),
        out_shape=jax.ShapeDtypeStruct((B, N, S), BF16),
        compiler_params=_params("arbitrary", "arbitrary"),
        name=name,
    )(wt_all, u)


def _cast_kernel(w_ref, o_ref):
    o_ref[...] = w_ref[...].astype(o_ref.dtype)


def _prep_w_v_t(w_in_t, row0, rows, tr=ROWS_STREAM):
    depth, _, D = w_in_t.shape
    return pl.pallas_call(
        _cast_kernel,
        grid=(depth, rows // tr),
        in_specs=[pl.BlockSpec((None, tr, D), lambda l, j: (l, row0 // tr + j, 0))],
        out_specs=pl.BlockSpec((None, tr, D), lambda l, j: (l, j, 0)),
        out_shape=jax.ShapeDtypeStruct((depth, rows, D), BF16),
        compiler_params=_params("arbitrary", "arbitrary"),
        name="prep_w_vT",
    )(w_in_t)


def _proj_dil_kernel(u_ref, wq_ref, wk_ref, wv_ref, o_ref, *scratch, dil):
    u = u_ref[...]
    rows = u_ref.shape[0] // dil
    gw = 2 * LANES
    j = 0
    for w_ref in (wq_ref, wk_ref, wv_ref):
        for part in range(w_ref.shape[1] // gw):
            acc = jnp.dot(u, w_ref[:, part * gw:(part + 1) * gw], preferred_element_type=F32)
            if dil == 1:
                o_ref[0, :, j * gw:(j + 1) * gw] = acc.astype(o_ref.dtype)
            else:
                acc_ref = scratch[0]
                for c in range(gw // LANES):
                    acc_ref[j, c] = acc[:, c * LANES:(c + 1) * LANES]
                for r in range(dil):
                    for c in range(gw // LANES):
                        o_ref[r, :, j * gw + c * LANES:j * gw + (c + 1) * LANES] = (
                            acc_ref[j, c, pl.ds(r, rows, stride=dil), :].astype(o_ref.dtype))
            j += 1


def _proj_dil_call(u, w_all, l, g, tm, name):
    B, S, D = u.shape
    dil = ATT_PATTERNS[g][1]
    N = 3 * ATT_OUT
    w_specs = [_w_spec(l, D, ATT_OUT, lambda b, i, t=t: (W_ATT + t * ATT_WIDTH) // ATT_OUT + g) for t in range(3)]
    scratch = [] if dil == 1 else [pltpu.VMEM((N // (2 * LANES), 2, tm, LANES), F32)]
    return pl.pallas_call(
        functools.partial(_proj_dil_kernel, dil=dil),
        grid=(B, S // tm),
        in_specs=[pl.BlockSpec((None, tm, D), lambda b, i: (b, i, 0))] + w_specs,
        out_specs=pl.BlockSpec((None, dil, tm // dil, N), lambda b, i: (b, 0, i, 0)),
        out_shape=jax.ShapeDtypeStruct((B, dil, S // dil, N), BF16),
        scratch_shapes=scratch,
        compiler_params=_params("arbitrary", "arbitrary"),
        name=name,
    )(u, w_all, w_all, w_all)


def _gate_kernel(u_ref, w_ref, b_ref, o_ref):
    g = jnp.dot(u_ref[...], w_ref[...], preferred_element_type=F32) + b_ref[...]
    lf = jnp.minimum(g, 0.0) - jnp.log1p(jnp.exp(-jnp.abs(g)))
    row = lax.broadcasted_iota(jnp.int32, (M_CHUNK, LANES), 0)
    lane = lax.broadcasted_iota(jnp.int32, (M_CHUNK, LANES), 1)
    H = M_HEADS
    for c in range(u_ref.shape[0] // M_CHUNK):
        sl = slice(c * M_CHUNK, (c + 1) * M_CHUNK)
        gc, pre, suf = g[sl], lf[sl], lf[sl]
        d = 1
        while d < M_CHUNK:
            pre = pre + jnp.where(row >= d, pltpu.roll(pre, d, 0), 0.0)
            suf = suf + jnp.where(row < M_CHUNK - d, pltpu.roll(suf, M_CHUNK - d, 0), 0.0)
            d *= 2
        a_f = gc - pltpu.roll(pre, LANES - H, 1)
        a_b = gc - pltpu.roll(suf, LANES - H, 1)
        e_f = jnp.broadcast_to(pltpu.roll(pre, 3 * H, 1)[M_CHUNK - 1:M_CHUNK], pre.shape)
        e_b = jnp.broadcast_to(pltpu.roll(suf, 2 * H, 1)[0:1], suf.shape)
        o_ref[sl, :] = jnp.where(lane < H, a_f, jnp.where(lane < 2 * H, pre,
                                 jnp.where(lane < 3 * H, a_b, jnp.where(lane < 4 * H, suf,
                                           jnp.where(lane < 5 * H, e_f, e_b)))))


def _gate_call(u2d, w_all, l, bg, tm=ROWS_STREAM):
    T, D = u2d.shape
    return pl.pallas_call(
        _gate_kernel,
        grid=(T // tm,),
        in_specs=[pl.BlockSpec((tm, D), lambda i: (i, 0)),
                  _w_spec(l, D, LANES, lambda i: W_GATE // LANES),
                  pl.BlockSpec((1, LANES), lambda i: (0, 0))],
        out_specs=pl.BlockSpec((tm, LANES), lambda i: (i, 0)),
        out_shape=jax.ShapeDtypeStruct((T, LANES), F32),
        compiler_params=_params("arbitrary"),
        name="gates",
    )(u2d, w_all, bg)


def _proj_conv_kernel(u_ref, up_ref, un_ref, wc_ref, wo_ref, cw_ref, cb_ref, y_ref, z_ref, *,
                      scale, transpose_out):
    i = pl.program_id(1)
    tm = u_ref.shape[0]
    hr = up_ref.shape[0]
    before = jnp.where(i > 0, up_ref[...], jnp.zeros_like(up_ref))
    after = jnp.where(i < pl.num_programs(1) - 1, un_ref[...], jnp.zeros_like(un_ref))
    lhs = jnp.concatenate([before, u_ref[...], after], axis=0)
    x = jnp.dot(lhs, wc_ref[...], preferred_element_type=F32)
    z_ref[...] = jnp.dot(u_ref[...], wo_ref[...], preferred_element_type=F32).astype(z_ref.dtype)
    rt, ap = CONV_ROWS, SUBLANES
    for c in range(wc_ref.shape[1] // LANES):
        cs = slice(c * LANES, (c + 1) * LANES)
        for rb in range(tm // rt):
            r0 = hr + rb * rt
            blk = x[r0 - ap:r0 + rt + ap, cs]
            y = cb_ref[:, cs]
            for j in range(CONV_WIDTH):
                k = j - CONV_WIDTH // 2
                xs = blk if k == 0 else pltpu.roll(blk, (-k) % (rt + 2 * ap), 0)
                y = y + cw_ref[j:j + 1, cs] * xs[ap:ap + rt]
            y = y * jax.nn.sigmoid(y)
            if scale != 1.0:
                y = y * scale
            if transpose_out:
                y_ref[cs, rb * rt:(rb + 1) * rt] = y.T.astype(BF16)
            else:
                y_ref[rb * rt:(rb + 1) * rt, cs] = y.astype(BF16)


def _proj_conv_call(u, w_all, l, col0, wo_cols, to, conv_w8, conv_b, scale, transpose_out, name,
                    tm=ROWS_PROJ, tc=COLS_CONV):
    B, S, D = u.shape
    nj = M_WIDTH // tc
    assert len(wo_cols) == nj and all(c % to == 0 for c in wo_cols)
    wo_index = lambda j: sum(jnp.where(j == jj, c // to, 0) for jj, c in enumerate(wo_cols))
    hr = 2 * SUBLANES
    rb, nrb = tm // hr, S // hr
    c0 = col0 // tc
    if transpose_out:
        y_spec = pl.BlockSpec((None, tc, tm), lambda b, i, j: (b, j, i))
        y_shape = jax.ShapeDtypeStruct((B, M_WIDTH, S), BF16)
    else:
        y_spec = pl.BlockSpec((None, tm, tc), lambda b, i, j: (b, i, j))
        y_shape = jax.ShapeDtypeStruct((B, S, M_WIDTH), BF16)
    return pl.pallas_call(
        functools.partial(_proj_conv_kernel, scale=scale, transpose_out=transpose_out),
        grid=(B, S // tm, nj),
        in_specs=[pl.BlockSpec((None, tm, D), lambda b, i, j: (b, i, 0)),
                  pl.BlockSpec((None, hr, D), lambda b, i, j: (b, jnp.maximum(i * rb - 1, 0), 0)),
                  pl.BlockSpec((None, hr, D), lambda b, i, j: (b, jnp.minimum((i + 1) * rb, nrb - 1), 0)),
                  _w_spec(l, D, tc, lambda b, i, j: c0 + j),
                  _w_spec(l, D, to, lambda b, i, j: wo_index(j)),
                  pl.BlockSpec((SUBLANES, tc), lambda b, i, j: (0, c0 + j)),
                  pl.BlockSpec((1, tc), lambda b, i, j: (0, c0 + j))],
        out_specs=[y_spec, pl.BlockSpec((None, tm, to), lambda b, i, j: (b, i, j))],
        out_shape=[y_shape, jax.ShapeDtypeStruct((B, S, nj * to), BF16)],
        compiler_params=_params("arbitrary", "arbitrary", "arbitrary"),
        name=name,
    )(u, u, u, w_all, w_all, conv_w8, conv_b)


def _mlstm_kernel(kf_ref, kb_ref, qtf_ref, qtb_ref, vtf_ref, vtb_ref, gf_ref, gb_ref, of_ref, ob_ref,
                  gnw_ref, yhi_ref, ylo_ref, hs_ref, ct_ref, n_ref, m_ref):
    L, H, dh = M_CHUNK, M_HEADS, M_HEAD_DIM
    i = pl.program_id(1)
    nc = pl.num_programs(1)

    @pl.when(i == 0)
    def _init():
        ct_ref[...] = jnp.zeros_like(ct_ref)
        n_ref[...] = jnp.zeros_like(n_ref)
        m_ref[...] = jnp.full(m_ref.shape, NEG, F32)

    row = lax.broadcasted_iota(jnp.int32, (L, L), 0)
    col = lax.broadcasted_iota(jnp.int32, (L, L), 1)
    mask = (row <= col, row >= col)
    k_refs, qt_refs, vt_refs = (kf_ref, kb_ref), (qtf_ref, qtb_ref), (vtf_ref, vtb_ref)
    g = (gf_ref[...], gb_ref[...])
    gt = (g[0].T, g[1].T)
    units = [(d, h) for d in range(2) for h in range(H)]
    twice = lambda r: jnp.concatenate([r, r], axis=1)

    gate = []
    for u, (d, h) in enumerate(units):
        ja, jb, je = h + 2 * H * d, H + h + 2 * H * d, 4 * H + h + H * d
        a_col, a_row = g[d][:, ja:ja + 1], gt[d][ja:ja + 1, :]
        b_row, e_row = gt[d][jb:jb + 1, :], gt[d][je:je + 1, :]
        m_prev = m_ref[u:u + 1, :]
        log_d = jnp.where(mask[d], a_col + b_row, NEG)
        m_inter = b_row + m_prev
        m_t = jnp.maximum(jnp.max(log_d, axis=0, keepdims=True), m_inter)
        lw = e_row + a_row
        m_new = jnp.maximum(e_row + m_prev, jnp.max(lw, axis=-1, keepdims=True))
        m_ref[u:u + 1, :] = m_new
        gate.append(dict(dmat=jnp.exp(log_d - m_t), w_inter=jnp.exp(m_inter - m_t), emt=jnp.exp(-m_t),
                         ws=jnp.exp(lw - m_new), decay=jnp.exp(e_row + m_prev - m_new)))

    ops, pre = [], []
    for u, (d, h) in enumerate(units):
        hs = slice(h * dh, (h + 1) * dh)
        k_u, qt_u, vt_u = k_refs[d][:, hs], qt_refs[d][hs, :], vt_refs[d][hs, :]
        ops.append((k_u, qt_u, vt_u))
        st = jnp.dot(k_u, qt_u, preferred_element_type=F32)
        cq = jnp.dot(ct_ref[u].astype(BF16), qt_u, preferred_element_type=F32)
        n8 = jnp.broadcast_to(n_ref[u:u + 1, :].astype(BF16), (SUBLANES, dh))
        qn = jnp.dot(n8, qt_u, preferred_element_type=F32)[0:1]
        pre.append((st, cq, qn))

    h_t = []
    for u, (d, h) in enumerate(units):
        (k_u, qt_u, vt_u), (st, cq, qn), gq = ops[u], pre[u], gate[u]
        p = st * gq["dmat"]
        den = jnp.sum(p, axis=0, keepdims=True) + gq["w_inter"] * qn
        r = 1.0 / jnp.maximum(jnp.abs(den), gq["emt"])
        num = jnp.dot(vt_u, p.astype(BF16), preferred_element_type=F32) + gq["w_inter"] * cq
        h_t.append(num * r)

    for u, (d, h) in enumerate(units):
        (k_u, qt_u, vt_u), gq = ops[u], gate[u]
        vw = (vt_u.astype(F32) * gq["ws"]).astype(BF16)
        ct_ref[u] = twice(gq["decay"]) * ct_ref[u] + jnp.dot(vw, k_u, preferred_element_type=F32)
        w8 = jnp.broadcast_to(gq["ws"].astype(BF16), (SUBLANES, L))
        n_ref[u:u + 1, :] = (twice(gq["decay"]) * n_ref[u:u + 1, :]
                             + jnp.dot(w8, k_u, preferred_element_type=F32)[0:1])

    chunk = (i, nc - 1 - i)

    @pl.when(i < nc // 2)
    def _park():
        for u, (d, h) in enumerate(units):
            hs_ref[chunk[d], h] = h_t[u]

    @pl.when(i >= nc // 2)
    def _finish():
        for d, (o_ref, y_ref) in enumerate(((of_ref, yhi_ref), (ob_ref, ylo_ref))):
            for h in range(H):
                hs = slice(h * dh, (h + 1) * dh)
                x = h_t[d * H + h] + hs_ref[chunk[d], h]
                mu = jnp.mean(x, axis=0, keepdims=True)
                xc = x - mu
                var = jnp.mean(xc * xc, axis=0, keepdims=True)
                hn = (xc * lax.rsqrt(var + LN_EPS)).T
                y_ref[:, hs] = (jax.nn.sigmoid(o_ref[:, hs].astype(F32)) * hn * gnw_ref[:, hs]).astype(BF16)


def _mlstm_call(k, qt, vt, gates, og, o_col, gnw):
    B, S, W = k.shape
    L = M_CHUNK
    nc = S // L
    assert nc % 2 == 0
    half = nc // 2
    fwd, bwd = (lambda i: i), (lambda i: nc - 1 - i)
    second_f, second_b = (lambda i: jnp.maximum(i, half)), (lambda i: jnp.minimum(nc - 1 - i, half - 1))
    rows = lambda w, ch, c=0: pl.BlockSpec((None, L, w), lambda b, i: (b, ch(i), c))
    cols = lambda ch: pl.BlockSpec((None, W, L), lambda b, i: (b, 0, ch(i)))
    return pl.pallas_call(
        _mlstm_kernel,
        grid=(B, nc),
        in_specs=[rows(W, fwd), rows(W, bwd), cols(fwd), cols(bwd), cols(fwd), cols(bwd),
                  rows(LANES, fwd), rows(LANES, bwd), rows(W, second_f, o_col), rows(W, second_b, o_col),
                  pl.BlockSpec((1, W), lambda b, i: (0, 0))],
        out_specs=[pl.BlockSpec((None, L, W), lambda b, i: (b, jnp.maximum(i - half, 0), 0)),
                   pl.BlockSpec((None, L, W), lambda b, i: (b, jnp.minimum(nc - 1 - i, half - 1), 0))],
        out_shape=[jax.ShapeDtypeStruct((B, S // 2, W), BF16)] * 2,
        scratch_shapes=[pltpu.VMEM((nc, M_HEADS, M_HEAD_DIM, L), F32),
                        pltpu.VMEM((2 * M_HEADS, M_HEAD_DIM, M_HEAD_DIM), F32),
                        pltpu.VMEM((2 * M_HEADS, M_HEAD_DIM), F32),
                        pltpu.VMEM((2 * M_HEADS, L), F32)],
        compiler_params=_params("arbitrary", "arbitrary"),
        name="mlstm",
    )(k, k, qt, qt, vt, vt, gates, gates, og, og, gnw)


ATT_UNROLL = 8


def _attn_kernel(q_ref, k_ref, kp_ref, kn_ref, v_ref, vp_ref, vn_ref, bm_ref, o_ref, lse_ref,
                 *scratch, dil):
    hb = ATT_HALF
    tq = q_ref.shape[1]
    nqb = tq // hb
    i = pl.program_id(1)
    ni = pl.num_programs(1)
    if nqb == 1:
        stage = scratch

        def make_window(main, before, after):
            return lambda r, r0, ls: jnp.concatenate([before[r, :, ls], main[r, :, ls], after[r, :, ls]], axis=0)
    else:
        pads, stage = scratch[:2], scratch[2:]

        def make_window(main, before, after, pad):
            pad[:, 0:hb, :] = before[...]
            pad[:, hb:hb + tq, :] = main[...]
            pad[:, hb + tq:, :] = after[...]
            return lambda r, r0, ls: pad[r, pl.ds(r0, 3 * hb), ls]
    k_window = make_window(k_ref, kp_ref, kn_ref, *(() if nqb == 1 else (pads[0],)))
    v_window = make_window(v_ref, vp_ref, vn_ref, *(() if nqb == 1 else (pads[1],)))
    even = lax.broadcasted_iota(jnp.int32, (hb, LANES), 1) < ATT_HEAD_DIM
    ones = jnp.ones((3 * hb, LANES), BF16)
    if dil > 1:
        os_ref, ls_ref = stage

    npair = ATT_SLOTS // 2
    total = dil * nqb
    step = min(ATT_UNROLL, total)

    def body(it, carry):
        where, scores = [], []
        for j in range(step):
            idx = it * step + j
            r, qb = idx // nqb, idx % nqb
            r0 = pl.multiple_of(qb * hb, hb)
            case = (jnp.logical_and(i == 0, qb == 0).astype(jnp.int32)
                    + 2 * jnp.logical_and(i == ni - 1, qb == nqb - 1).astype(jnp.int32))
            for p in range(npair):
                ls = slice(p * LANES, (p + 1) * LANES)
                q2 = q_ref[r, pl.ds(r0, hb), ls]
                qs = jnp.concatenate([jnp.where(even, q2, 0), jnp.where(even, 0, q2)], axis=0)
                kw = k_window(r, r0, ls)
                s = lax.dot_general(qs, kw, (((1,), (1,)), ((), ())), preferred_element_type=F32)
                scores.append(s + bm_ref[case, p])
                where.append((r, r0, p, ls))
        s = jnp.concatenate(scores, axis=0)
        m = jnp.max(s, axis=-1, keepdims=True)
        e = jnp.exp2(s - m).astype(BF16)
        outs = []
        for u, (r, r0, p, ls) in enumerate(where):
            vw = jnp.concatenate([v_window(r, r0, ls), ones], axis=1)
            outs.append(jnp.dot(e[u * 2 * hb:(u + 1) * 2 * hb], vw, preferred_element_type=F32))
        ol = jnp.concatenate(outs, axis=0)
        l = ol[:, LANES:]
        o = ol[:, :LANES] * (1.0 / l)
        lse = m + jnp.log2(l)
        for u, (r, r0, p, ls) in enumerate(where):
            top, bot = slice(u * 2 * hb, u * 2 * hb + hb), slice(u * 2 * hb + hb, (u + 1) * 2 * hb)
            o2 = jnp.where(even, o[top], o[bot])
            l2 = jnp.where(even, lse[top], lse[bot])
            if dil > 1:
                os_ref[r, pl.ds(r0, hb), ls] = o2
                ls_ref[r, pl.ds(r0, hb), ls] = l2
            else:
                o_ref[p, pl.ds(r0, hb), :] = o2
                lse_ref[p, pl.ds(r0, hb), :] = l2
        return carry

    lax.fori_loop(0, total // step, body, 0)
    if dil > 1:
        for r in range(dil):
            for p in range(ATT_SLOTS // 2):
                ls = slice(p * LANES, (p + 1) * LANES)
                o_ref[p, pl.ds(r, tq, stride=dil), :] = os_ref[r, :, ls]
                lse_ref[p, pl.ds(r, tq, stride=dil), :] = ls_ref[r, :, ls]


def _attn_bias(g):
    h = np.arange(1, ATT_HEADS + 1, dtype=np.float32)
    slopes = np.exp2(-8.0 * h / ATT_HEADS).astype(np.float32).reshape(N_PATTERNS, ATT_SLOTS)[g]
    dil = ATT_PATTERNS[g][1]
    hb = ATT_HALF
    qi = np.arange(hb)[:, None] + hb
    kj = np.arange(3 * hb)[None, :]
    dist = np.abs(qi - kj)
    out = np.empty((4, ATT_SLOTS // 2, 2 * hb, 3 * hb), np.float32)
    for case in range(4):
        valid = dist <= hb
        if case & 1:
            valid = valid & (kj >= hb)
        if case & 2:
            valid = valid & (kj < 2 * hb)
        for p in range(ATT_SLOTS // 2):
            for e in range(2):
                bias = -slopes[2 * p + e] * (dil * dist).astype(np.float32) * np.float32(LOG2E)
                out[case, p, e * hb:(e + 1) * hb] = np.where(valid, bias, np.float32(NEG))
    return jnp.asarray(out)


def _attn_call(zq, g, col0):
    B, dil, n, W = zq.shape
    S = n * dil
    hb = ATT_HALF
    tq = min(n, max(ROWS_PROJ // dil, hb))
    cw = ATT_OUT
    qc, kc, vc = (col0 // cw + t for t in range(3))
    rb, nrb = tq // hb, n // hb

    def main(c):
        return pl.BlockSpec((None, dil, tq, cw), lambda b, i: (b, 0, i, c))

    def prev(c):
        return pl.BlockSpec((None, dil, hb, cw), lambda b, i: (b, 0, jnp.maximum(i * rb - 1, 0), c))

    def nxt(c):
        return pl.BlockSpec((None, dil, hb, cw), lambda b, i: (b, 0, jnp.minimum((i + 1) * rb, nrb - 1), c))

    npair = ATT_SLOTS // 2
    out_spec = pl.BlockSpec((None, npair, tq * dil, LANES), lambda b, i: (b, 0, i, 0))
    scratch = [] if tq == hb else [pltpu.VMEM((dil, tq + 2 * hb, cw), BF16)] * 2
    if dil > 1:
        scratch += [pltpu.VMEM((dil, tq, cw), F32), pltpu.VMEM((dil, tq, cw), F32)]
    return pl.pallas_call(
        functools.partial(_attn_kernel, dil=dil),
        grid=(B, n // tq),
        in_specs=[main(qc), main(kc), prev(kc), nxt(kc), main(vc), prev(vc), nxt(vc),
                  pl.BlockSpec((4, ATT_SLOTS // 2, 2 * hb, 3 * hb), lambda b, i: (0, 0, 0, 0))],
        out_specs=[out_spec, out_spec],
        out_shape=[jax.ShapeDtypeStruct((B, npair, S, LANES), F32)] * 2,
        scratch_shapes=scratch,
        compiler_params=_params("arbitrary", "arbitrary"),
        name=f"attn{g}",
    )(zq, zq, zq, zq, zq, zq, zq, _attn_bias(g))


def _mix_kernel(yhi_ref, ylo_ref, o0_ref, o1_ref, o2_ref, l0_ref, l1_ref, l2_ref, ga_ref, gb_ref,
                h_ref, wa_ref, wb_ref, wo_ref, ada_ref, lng_ref, lnb_ref, hn_ref, u_ref):
    upper = pl.program_id(1) >= pl.num_programs(1) // 2
    ym = jnp.where(upper, yhi_ref[...], ylo_ref[...])
    pairs = []
    for p in range(ATT_SLOTS // 2):
        l0, l1, l2 = l0_ref[p], l1_ref[p], l2_ref[p]
        lm = jnp.maximum(jnp.maximum(l0, l1), l2)
        e0, e1, e2 = jnp.exp2(l0 - lm), jnp.exp2(l1 - lm), jnp.exp2(l2 - lm)
        pairs.append((e0 * o0_ref[p] + e1 * o1_ref[p] + e2 * o2_ref[p]) * (1.0 / (e0 + e1 + e2)))
    att = jnp.concatenate(pairs, axis=-1)
    y_a = jnp.dot(ym, wa_ref[...], preferred_element_type=F32)
    y_b = jnp.dot(att.astype(BF16), wb_ref[...], preferred_element_type=F32)
    mix = (jax.nn.sigmoid(ga_ref[...].astype(F32)) * y_a
           + jax.nn.sigmoid(gb_ref[...].astype(F32)) * y_b)
    y = jnp.dot(mix.astype(BF16), wo_ref[...], preferred_element_type=F32)
    hn = _layer_norm_rows(DN_ALPHA * h_ref[...] + ada_ref[2:3, :] * y) * lng_ref[...] + lnb_ref[...]
    hn_ref[...] = hn
    u_ref[...] = (hn * (1.0 + ada_ref[4:5, :]) + ada_ref[3:4, :]).astype(BF16)


def _mix_call(y_hi, y_lo, att, ga, gb, gb_col, h, wa, wb, wo, ada_l, lng, lnb, tm=ROWS_MIX):
    B, S, D = h.shape
    half = S // tm // 2
    row = lambda w, c=0: pl.BlockSpec((None, tm, w), lambda b, i: (b, i, c))
    full = lambda a: pl.BlockSpec(a.shape, lambda b, i: (0,) * a.ndim)
    (o0, l0), (o1, l1), (o2, l2) = att
    return pl.pallas_call(
        _mix_kernel,
        grid=(B, S // tm),
        in_specs=[pl.BlockSpec((None, tm, D), lambda b, i: (b, jnp.maximum(i - half, 0), 0)),
                  pl.BlockSpec((None, tm, D), lambda b, i: (b, jnp.minimum(i, half - 1), 0))]
                 + [pl.BlockSpec((None, ATT_SLOTS // 2, tm, LANES), lambda b, i: (b, 0, i, 0))] * 6
                 + [row(D), row(D, gb_col), row(D),
                  full(wa), full(wb), full(wo),
                  pl.BlockSpec((None, SUBLANES, D), lambda b, i: (b, 0, 0)), full(lng), full(lnb)],
        out_specs=[row(D), row(D)],
        out_shape=[jax.ShapeDtypeStruct((B, S, D), F32), jax.ShapeDtypeStruct((B, S, D), BF16)],
        compiler_params=_params("arbitrary", "arbitrary"),
        name="mix",
    )(y_hi, y_lo, o0, o1, o2, l0, l1, l2, ga, gb, h, wa, wb, wo, ada_l, lng, lnb)


def _ffn_kernel(u_ref, h_ref, w1_ref, w3_ref, w2_ref, ada_ref, lng_ref, lnb_ref, adan_ref,
                hn_ref, *rest, emit_u):
    acc_ref = rest[-1]
    u = u_ref[...]
    for f in range(w1_ref.shape[1] // FF_CHUNK):
        fs = slice(f * FF_CHUNK, (f + 1) * FF_CHUNK)
        a = jnp.dot(u, w1_ref[:, fs], preferred_element_type=F32)
        b = jnp.dot(u, w3_ref[:, fs], preferred_element_type=F32)
        t = (a * jax.nn.sigmoid(a) * b).astype(BF16)
        part = jnp.dot(t, w2_ref[fs, :], preferred_element_type=F32)
        if f == 0:
            acc_ref[...] = part
        else:
            acc_ref[...] += part
    hn = (_layer_norm_rows(DN_ALPHA * h_ref[...] + ada_ref[5:6, :] * acc_ref[...])
          * lng_ref[...] + lnb_ref[...])
    hn_ref[...] = hn
    if emit_u:
        rest[0][...] = (hn * (1.0 + adan_ref[1:2, :]) + adan_ref[0:1, :]).astype(BF16)


def _ffn_call(u, h, w1, w3, w2, ada_l, lng, lnb, ada_next, emit_u, tm=ROWS_FFN):
    B, S, D = h.shape
    row = pl.BlockSpec((None, tm, D), lambda b, i: (b, i, 0))
    full = lambda a: pl.BlockSpec(a.shape, lambda b, i: (0,) * a.ndim)
    ada_spec = pl.BlockSpec((None, SUBLANES, D), lambda b, i: (b, 0, 0))
    out_specs = [row, row] if emit_u else [row]
    out_shape = [jax.ShapeDtypeStruct((B, S, D), F32)]
    if emit_u:
        out_shape.append(jax.ShapeDtypeStruct((B, S, D), BF16))
    return pl.pallas_call(
        functools.partial(_ffn_kernel, emit_u=emit_u),
        grid=(B, S // tm),
        in_specs=[row, row, full(w1), full(w3), full(w2), ada_spec, full(lng), full(lnb), ada_spec],
        out_specs=out_specs, out_shape=out_shape,
        scratch_shapes=[pltpu.VMEM((tm, D), F32)],
        compiler_params=_params("arbitrary", "arbitrary"),
        name="ffn",
    )(u, h, w1, w3, w2, ada_l, lng, lnb, ada_next)


def kernel(x, c, w_in, b_gates, conv_w, conv_b, gn_w, w_a, w_b, w_out, w_ada, b_ada, ln1_g, ln1_b,
           w1, w3, w2, ln2_g, ln2_b):
    B, S, D = x.shape
    depth = w_in.shape[0]
    T = B * S
    H = M_HEADS

    c_pad = jnp.zeros((SUBLANES, D), F32).at[:B].set(c)
    ada = _ada_call(c_pad, w_ada, b_ada)
    ada = jnp.pad(ada[:, :, :B].transpose(0, 2, 1, 3), ((0, 0), (0, 0), (0, 2), (0, 0)))

    w_in_t = jnp.swapaxes(w_in, 1, 2)
    w_all = _prep_w_in(w_in_t)
    wvt_all = _prep_w_v_t(w_in_t, IN_SECTIONS[0], IN_SECTIONS[1])
    h, u = _ln0_call(x, ada[0])
    for l in range(depth):
        bg = jnp.pad(b_gates[l], (0, LANES - 4 * H)).reshape(1, LANES)

        u2d = u.reshape(T, D)
        vt = _proj_t_call(u, wvt_all, l, ROWS_STREAM, "proj_vT")
        zq = [_proj_dil_call(u, w_all, l, g, ROWS_PROJ, f"projb{g}") for g in range(N_PATTERNS)]
        gates = _gate_call(u2d, w_all, l, bg).reshape(B, S, LANES)

        cw8 = jnp.pad(conv_w[l], ((0, SUBLANES - CONV_WIDTH), (0, 0)))
        cb = conv_b[l].reshape(1, 2 * M_WIDTH)
        qt, ga = _proj_conv_call(u, w_all, l, W_QK, (W_GA, W_GA + D // 2), D // 2, cw8, cb,
                                 M_HEAD_DIM ** -0.5, True, "proj_conv_q")
        k, ogb = _proj_conv_call(u, w_all, l, W_QK + M_WIDTH, (W_O, W_GB), D, cw8, cb,
                                 1.0, False, "proj_conv_k")
        y_hi, y_lo = _mlstm_call(k, qt, vt, gates, ogb, 0, gn_w[l].reshape(1, M_WIDTH))

        att = [_attn_call(zq[g], g, 0) for g in range(N_PATTERNS)]
        h, u = _mix_call(y_hi, y_lo, att, ga, ogb, 1, h, w_a[l].astype(BF16), w_b[l].astype(BF16),
                         w_out[l].astype(BF16), ada[l], ln1_g[l].reshape(1, D), ln1_b[l].reshape(1, D))
        last = l == depth - 1
        outs = _ffn_call(u, h, w1[l].astype(BF16), w3[l].astype(BF16), w2[l].astype(BF16), ada[l],
                         ln2_g[l].reshape(1, D), ln2_b[l].reshape(1, D),
                         ada[l] if last else ada[l + 1], not last)
        h = outs[0]
        if not last:
            u = outs[1]
    return h
```

```python
import functools

import jax
import jax.numpy as jnp
import numpy as np
from jax import lax
from jax.experimental import pallas as pl
from jax.experimental.pallas import tpu as pltpu

F32 = jnp.float32
BF16 = jnp.bfloat16

D_MODEL = 1024
DEPTH = 2
M_HEADS = 4
M_HEAD_DIM = 256
M_WIDTH = M_HEADS * M_HEAD_DIM
M_CHUNK = 128
CONV_WIDTH = 5
ATT_SLOTS = 8
ATT_HEAD_DIM = 64
ATT_PATTERNS = ((128, 1), (512, 4), (2048, 16))
N_PATTERNS = 3
ATT_HEADS = ATT_SLOTS * N_PATTERNS
ATT_WIDTH = ATT_HEADS * ATT_HEAD_DIM
ATT_OUT = ATT_SLOTS * ATT_HEAD_DIM
ATT_HALF = 64
D_FF = -(-8 * D_MODEL // (3 * 256)) * 256
DN_ALPHA = (2 * DEPTH) ** 0.25
LN_EPS = 1e-5
NEG = -1e30
IN_SECTIONS = (2 * M_WIDTH, M_WIDTH, M_WIDTH, 4 * M_HEADS, 3 * ATT_WIDTH, 2 * D_MODEL)

VMEM_LIMIT_BYTES = 56 * 1024 * 1024
LANES = 128
SUBLANES = 8

LOG2E = 1.4426950408889634

ROWS_STREAM = 512
ROWS_PROJ = 1024
COLS_CONV = 512
ROWS_MIX = 512
ROWS_FFN = 512
FF_CHUNK = 256
CONV_ROWS = 128


def _params(*sem):
    return pltpu.CompilerParams(dimension_semantics=sem, vmem_limit_bytes=VMEM_LIMIT_BYTES)


def _layer_norm_rows(x):
    mu = jnp.mean(x, axis=-1, keepdims=True)
    xc = x - mu
    var = jnp.mean(xc * xc, axis=-1, keepdims=True)
    return xc * lax.rsqrt(var + LN_EPS)


def _ada_kernel(c_ref, w_ref, b_ref, o_ref):
    c = c_ref[...]
    ca = c * jax.nn.sigmoid(c)
    o_ref[...] = jnp.dot(ca, w_ref[...], precision=lax.Precision.HIGHEST,
                         preferred_element_type=F32) + b_ref[...]


def _ada_call(c_pad, w_ada, b_ada):
    depth = w_ada.shape[0]
    return pl.pallas_call(
        _ada_kernel,
        grid=(depth, 6),
        in_specs=[pl.BlockSpec((SUBLANES, D_MODEL), lambda l, k: (0, 0)),
                  pl.BlockSpec((None, D_MODEL, D_MODEL), lambda l, k: (l, 0, k)),
                  pl.BlockSpec((None, 1, D_MODEL), lambda l, k: (l, 0, k))],
        out_specs=pl.BlockSpec((None, None, SUBLANES, D_MODEL), lambda l, k: (l, k, 0, 0)),
        out_shape=jax.ShapeDtypeStruct((depth, 6, SUBLANES, D_MODEL), F32),
        compiler_params=_params("arbitrary", "arbitrary"),
        name="ada",
    )(c_pad, w_ada, b_ada.reshape(depth, 1, 6 * D_MODEL))


def _ln0_kernel(x_ref, ada_ref, h_ref, u_ref):
    y = _layer_norm_rows(x_ref[...])
    h_ref[...] = y
    u_ref[...] = (y * (1.0 + ada_ref[1:2, :]) + ada_ref[0:1, :]).astype(BF16)


def _ln0_call(x, ada0, ts=ROWS_STREAM):
    B, S, D = x.shape
    row = pl.BlockSpec((None, ts, D), lambda b, i: (b, i, 0))
    return pl.pallas_call(
        _ln0_kernel,
        grid=(B, S // ts),
        in_specs=[row, pl.BlockSpec((None, SUBLANES, D), lambda b, i: (b, 0, 0))],
        out_specs=[row, row],
        out_shape=[jax.ShapeDtypeStruct((B, S, D), F32), jax.ShapeDtypeStruct((B, S, D), BF16)],
        compiler_params=_params("arbitrary", "arbitrary"),
        name="ln0",
    )(x, ada0)


W_QK, W_O, W_GA, W_GB, W_ATT = 0, 2 * M_WIDTH, 3 * M_WIDTH, 3 * M_WIDTH + D_MODEL, 3 * M_WIDTH + 2 * D_MODEL
W_GATE = W_ATT + 3 * ATT_WIDTH
W_BLOCK = 512
W_ALL = W_GATE + W_BLOCK
PREP_ALIGN = 4 * M_HEADS


def _prep_plan():
    s0, s1, s2, s3, s4, s5 = np.cumsum((0,) + IN_SECTIONS)[:6].tolist()
    runs = [(W_QK, s0, 2 * M_WIDTH, 0), (W_O, s2, M_WIDTH, 0), (W_GA, s5, 2 * D_MODEL, 0),
            (W_ATT, s4, ATT_WIDTH, 1), (W_ATT + ATT_WIDTH, s4 + ATT_WIDTH, 2 * ATT_WIDTH, 0),
            (W_GATE, s3, 4 * M_HEADS, 0)]
    plan = np.zeros((3, W_ALL // W_BLOCK), np.int32)
    for dst, src, width, is_q in runs:
        for k in range(-(-width // W_BLOCK)):
            assert (src + k * W_BLOCK) % PREP_ALIGN == 0
            plan[:, dst // W_BLOCK + k] = ((src + k * W_BLOCK) // PREP_ALIGN, is_q,
                                           min(W_BLOCK, width - k * W_BLOCK))
    return plan


def _prep_kernel(src_ref, isq_ref, valid_ref, wt_ref, o_ref):
    ob = pl.program_id(1)
    scale = jnp.where(isq_ref[ob] == 1, ATT_HEAD_DIM ** -0.5 * LOG2E, 1.0)
    y = wt_ref[0].T * scale
    lane = lax.broadcasted_iota(jnp.int32, y.shape, 1)
    o_ref[...] = jnp.where(lane < valid_ref[ob], y, 0.0).astype(BF16)


def _prep_w_in(w_in_t):
    depth, _, D = w_in_t.shape
    plan = _prep_plan()
    return pl.pallas_call(
        _prep_kernel,
        grid_spec=pltpu.PrefetchScalarGridSpec(
            num_scalar_prefetch=3, grid=(depth, W_ALL // W_BLOCK),
            in_specs=[pl.BlockSpec((pl.Element(1), pl.Element(W_BLOCK), pl.Element(D)),
                                   lambda l, ob, src, *_: (l, src[ob] * PREP_ALIGN, 0))],
            out_specs=pl.BlockSpec((None, D, W_BLOCK), lambda l, ob, *_: (l, 0, ob))),
        out_shape=jax.ShapeDtypeStruct((depth, D, W_ALL), BF16),
        compiler_params=_params("arbitrary", "arbitrary"),
        name="prep_w_in",
    )(*(jnp.asarray(p) for p in plan), w_in_t)


def _w_spec(l, rows, width, block_index):
    return pl.BlockSpec((None, rows, width), lambda *g: (l, 0, block_index(*g)))


def _proj_t_kernel(wt_ref, u_ref, o_ref):
    o_ref[...] = lax.dot_general(wt_ref[...], u_ref[...], (((1,), (1,)), ((), ())),
                                 preferred_element_type=F32).astype(o_ref.dtype)


def _proj_t_call(u, wt_all, l, tm, name):
    B, S, D = u.shape
    N = wt_all.shape[1]
    return pl.pallas_call(
        _proj_t_kernel,
        grid=(B, S // tm),
        in_specs=[pl.BlockSpec((None, N, D), lambda b, i: (l, 0, 0)),
                  pl.BlockSpec((None, tm, D), lambda b, i: (b, i, 0))],
        out_specs=pl.BlockSpec((None, N, tm), lambda b, i: (b, 0, i)),
        out_shape=jax.ShapeDtypeStruct((B, N, S), BF16),
        compiler_params=_params("arbitrary", "arbitrary"),
        name=name,
    )(wt_all, u)


def _cast_kernel(w_ref, o_ref):
    o_ref[...] = w_ref[...].astype(o_ref.dtype)


def _prep_w_v_t(w_in_t, row0, rows, tr=ROWS_STREAM):
    depth, _, D = w_in_t.shape
    return pl.pallas_call(
        _cast_kernel,
        grid=(depth, rows // tr),
        in_specs=[pl.BlockSpec((None, tr, D), lambda l, j: (l, row0 // tr + j, 0))],
        out_specs=pl.BlockSpec((None, tr, D), lambda l, j: (l, j, 0)),
        out_shape=jax.ShapeDtypeStruct((depth, rows, D), BF16),
        compiler_params=_params("arbitrary", "arbitrary"),
        name="prep_w_vT",
    )(w_in_t)


def _proj_dil_kernel(u_ref, wq_ref, wk_ref, wv_ref, o_ref, *scratch, dil):
    u = u_ref[...]
    rows = u_ref.shape[0] // dil
    gw = 2 * LANES
    j = 0
    for w_ref in (wq_ref, wk_ref, wv_ref):
        for part in range(w_ref.shape[1] // gw):
            acc = jnp.dot(u, w_ref[:, part * gw:(part + 1) * gw], preferred_element_type=F32)
            if dil == 1:
                o_ref[0, :, j * gw:(j + 1) * gw] = acc.astype(o_ref.dtype)
            else:
                acc_ref = scratch[0]
                for c in range(gw // LANES):
                    acc_ref[j, c] = acc[:, c * LANES:(c + 1) * LANES]
                for r in range(dil):
                    for c in range(gw // LANES):
                        o_ref[r, :, j * gw + c * LANES:j * gw + (c + 1) * LANES] = (
                            acc_ref[j, c, pl.ds(r, rows, stride=dil), :].astype(o_ref.dtype))
            j += 1


def _proj_dil_call(u, w_all, l, g, tm, name):
    B, S, D = u.shape
    dil = ATT_PATTERNS[g][1]
    N = 3 * ATT_OUT
    w_specs = [_w_spec(l, D, ATT_OUT, lambda b, i, t=t: (W_ATT + t * ATT_WIDTH) // ATT_OUT + g) for t in range(3)]
    scratch = [] if dil == 1 else [pltpu.VMEM((N // (2 * LANES), 2, tm, LANES), F32)]
    return pl.pallas_call(
        functools.partial(_proj_dil_kernel, dil=dil),
        grid=(B, S // tm),
        in_specs=[pl.BlockSpec((None, tm, D), lambda b, i: (b, i, 0))] + w_specs,
        out_specs=pl.BlockSpec((None, dil, tm // dil, N), lambda b, i: (b, 0, i, 0)),
        out_shape=jax.ShapeDtypeStruct((B, dil, S // dil, N), BF16),
        scratch_shapes=scratch,
        compiler_params=_params("arbitrary", "arbitrary"),
        name=name,
    )(u, w_all, w_all, w_all)


def _gate_kernel(u_ref, w_ref, b_ref, o_ref):
    g = jnp.dot(u_ref[...], w_ref[...], preferred_element_type=F32) + b_ref[...]
    lf = jnp.minimum(g, 0.0) - jnp.log1p(jnp.exp(-jnp.abs(g)))
    row = lax.broadcasted_iota(jnp.int32, (M_CHUNK, LANES), 0)
    lane = lax.broadcasted_iota(jnp.int32, (M_CHUNK, LANES), 1)
    H = M_HEADS
    for c in range(u_ref.shape[0] // M_CHUNK):
        sl = slice(c * M_CHUNK, (c + 1) * M_CHUNK)
        gc, pre, suf = g[sl], lf[sl], lf[sl]
        d = 1
        while d < M_CHUNK:
            pre = pre + jnp.where(row >= d, pltpu.roll(pre, d, 0), 0.0)
            suf = suf + jnp.where(row < M_CHUNK - d, pltpu.roll(suf, M_CHUNK - d, 0), 0.0)
            d *= 2
        a_f = gc - pltpu.roll(pre, LANES - H, 1)
        a_b = gc - pltpu.roll(suf, LANES - H, 1)
        e_f = jnp.broadcast_to(pltpu.roll(pre, 3 * H, 1)[M_CHUNK - 1:M_CHUNK], pre.shape)
        e_b = jnp.broadcast_to(pltpu.roll(suf, 2 * H, 1)[0:1], suf.shape)
        o_ref[sl, :] = jnp.where(lane < H, a_f, jnp.where(lane < 2 * H, pre,
                                 jnp.where(lane < 3 * H, a_b, jnp.where(lane < 4 * H, suf,
                                           jnp.where(lane < 5 * H, e_f, e_b)))))


def _gate_call(u2d, w_all, l, bg, tm=ROWS_STREAM):
    T, D = u2d.shape
    return pl.pallas_call(
        _gate_kernel,
        grid=(T // tm,),
        in_specs=[pl.BlockSpec((tm, D), lambda i: (i, 0)),
                  _w_spec(l, D, LANES, lambda i: W_GATE // LANES),
                  pl.BlockSpec((1, LANES), lambda i: (0, 0))],
        out_specs=pl.BlockSpec((tm, LANES), lambda i: (i, 0)),
        out_shape=jax.ShapeDtypeStruct((T, LANES), F32),
        compiler_params=_params("arbitrary"),
        name="gates",
    )(u2d, w_all, bg)


def _proj_conv_kernel(u_ref, up_ref, un_ref, wc_ref, wo_ref, cw_ref, cb_ref, y_ref, z_ref, *,
                      scale, transpose_out):
    i = pl.program_id(1)
    tm = u_ref.shape[0]
    hr = up_ref.shape[0]
    before = jnp.where(i > 0, up_ref[...], jnp.zeros_like(up_ref))
    after = jnp.where(i < pl.num_programs(1) - 1, un_ref[...], jnp.zeros_like(un_ref))
    lhs = jnp.concatenate([before, u_ref[...], after], axis=0)
    x = jnp.dot(lhs, wc_ref[...], preferred_element_type=F32)
    z_ref[...] = jnp.dot(u_ref[...], wo_ref[...], preferred_element_type=F32).astype(z_ref.dtype)
    rt, ap = CONV_ROWS, SUBLANES
    for c in range(wc_ref.shape[1] // LANES):
        cs = slice(c * LANES, (c + 1) * LANES)
        for rb in range(tm // rt):
            r0 = hr + rb * rt
            blk = x[r0 - ap:r0 + rt + ap, cs]
            y = cb_ref[:, cs]
            for j in range(CONV_WIDTH):
                k = j - CONV_WIDTH // 2
                xs = blk if k == 0 else pltpu.roll(blk, (-k) % (rt + 2 * ap), 0)
                y = y + cw_ref[j:j + 1, cs] * xs[ap:ap + rt]
            y = y * jax.nn.sigmoid(y)
            if scale != 1.0:
                y = y * scale
            if transpose_out:
                y_ref[cs, rb * rt:(rb + 1) * rt] = y.T.astype(BF16)
            else:
                y_ref[rb * rt:(rb + 1) * rt, cs] = y.astype(BF16)


def _proj_conv_call(u, w_all, l, col0, wo_cols, to, conv_w8, conv_b, scale, transpose_out, name,
                    tm=ROWS_PROJ, tc=COLS_CONV):
    B, S, D = u.shape
    nj = M_WIDTH // tc
    assert len(wo_cols) == nj and all(c % to == 0 for c in wo_cols)
    wo_index = lambda j: sum(jnp.where(j == jj, c // to, 0) for jj, c in enumerate(wo_cols))
    hr = 2 * SUBLANES
    rb, nrb = tm // hr, S // hr
    c0 = col0 // tc
    if transpose_out:
        y_spec = pl.BlockSpec((None, tc, tm), lambda b, i, j: (b, j, i))
        y_shape = jax.ShapeDtypeStruct((B, M_WIDTH, S), BF16)
    else:
        y_spec = pl.BlockSpec((None, tm, tc), lambda b, i, j: (b, i, j))
        y_shape = jax.ShapeDtypeStruct((B, S, M_WIDTH), BF16)
    return pl.pallas_call(
        functools.partial(_proj_conv_kernel, scale=scale, transpose_out=transpose_out),
        grid=(B, S // tm, nj),
        in_specs=[pl.BlockSpec((None, tm, D), lambda b, i, j: (b, i, 0)),
                  pl.BlockSpec((None, hr, D), lambda b, i, j: (b, jnp.maximum(i * rb - 1, 0), 0)),
                  pl.BlockSpec((None, hr, D), lambda b, i, j: (b, jnp.minimum((i + 1) * rb, nrb - 1), 0)),
                  _w_spec(l, D, tc, lambda b, i, j: c0 + j),
                  _w_spec(l, D, to, lambda b, i, j: wo_index(j)),
                  pl.BlockSpec((SUBLANES, tc), lambda b, i, j: (0, c0 + j)),
                  pl.BlockSpec((1, tc), lambda b, i, j: (0, c0 + j))],
        out_specs=[y_spec, pl.BlockSpec((None, tm, to), lambda b, i, j: (b, i, j))],
        out_shape=[y_shape, jax.ShapeDtypeStruct((B, S, nj * to), BF16)],
        compiler_params=_params("arbitrary", "arbitrary", "arbitrary"),
        name=name,
    )(u, u, u, w_all, w_all, conv_w8, conv_b)


def _mlstm_kernel(kf_ref, kb_ref, qtf_ref, qtb_ref, vtf_ref, vtb_ref, gf_ref, gb_ref, of_ref, ob_ref,
                  gnw_ref, yhi_ref, ylo_ref, hs_ref, ct_ref, n_ref, m_ref):
    L, H, dh = M_CHUNK, M_HEADS, M_HEAD_DIM
    i = pl.program_id(1)
    nc = pl.num_programs(1)

    @pl.when(i == 0)
    def _init():
        ct_ref[...] = jnp.zeros_like(ct_ref)
        n_ref[...] = jnp.zeros_like(n_ref)
        m_ref[...] = jnp.full(m_ref.shape, NEG, F32)

    row = lax.broadcasted_iota(jnp.int32, (L, L), 0)
    col = lax.broadcasted_iota(jnp.int32, (L, L), 1)
    mask = (row <= col, row >= col)
    k_refs, qt_refs, vt_refs = (kf_ref, kb_ref), (qtf_ref, qtb_ref), (vtf_ref, vtb_ref)
    g = (gf_ref[...], gb_ref[...])
    gt = (g[0].T, g[1].T)
    units = [(d, h) for d in range(2) for h in range(H)]
    twice = lambda r: jnp.concatenate([r, r], axis=1)

    gate = []
    for u, (d, h) in enumerate(units):
        ja, jb, je = h + 2 * H * d, H + h + 2 * H * d, 4 * H + h + H * d
        a_col, a_row = g[d][:, ja:ja + 1], gt[d][ja:ja + 1, :]
        b_row, e_row = gt[d][jb:jb + 1, :], gt[d][je:je + 1, :]
        m_prev = m_ref[u:u + 1, :]
        log_d = jnp.where(mask[d], a_col + b_row, NEG)
        m_inter = b_row + m_prev
        m_t = jnp.maximum(jnp.max(log_d, axis=0, keepdims=True), m_inter)
        lw = e_row + a_row
        m_new = jnp.maximum(e_row + m_prev, jnp.max(lw, axis=-1, keepdims=True))
        m_ref[u:u + 1, :] = m_new
        gate.append(dict(dmat=jnp.exp(log_d - m_t), w_inter=jnp.exp(m_inter - m_t), emt=jnp.exp(-m_t),
                         ws=jnp.exp(lw - m_new), decay=jnp.exp(e_row + m_prev - m_new)))

    ops, pre = [], []
    for u, (d, h) in enumerate(units):
        hs = slice(h * dh, (h + 1) * dh)
        k_u, qt_u, vt_u = k_refs[d][:, hs], qt_refs[d][hs, :], vt_refs[d][hs, :]
        ops.append((k_u, qt_u, vt_u))
        st = jnp.dot(k_u, qt_u, preferred_element_type=F32)
        cq = jnp.dot(ct_ref[u].astype(BF16), qt_u, preferred_element_type=F32)
        n8 = jnp.broadcast_to(n_ref[u:u + 1, :].astype(BF16), (SUBLANES, dh))
        qn = jnp.dot(n8, qt_u, preferred_element_type=F32)[0:1]
        pre.append((st, cq, qn))

    h_t = []
    for u, (d, h) in enumerate(units):
        (k_u, qt_u, vt_u), (st, cq, qn), gq = ops[u], pre[u], gate[u]
        p = st * gq["dmat"]
        den = jnp.sum(p, axis=0, keepdims=True) + gq["w_inter"] * qn
        r = 1.0 / jnp.maximum(jnp.abs(den), gq["emt"])
        num = jnp.dot(vt_u, p.astype(BF16), preferred_element_type=F32) + gq["w_inter"] * cq
        h_t.append(num * r)

    for u, (d, h) in enumerate(units):
        (k_u, qt_u, vt_u), gq = ops[u], gate[u]
        vw = (vt_u.astype(F32) * gq["ws"]).astype(BF16)
        ct_ref[u] = twice(gq["decay"]) * ct_ref[u] + jnp.dot(vw, k_u, preferred_element_type=F32)
        w8 = jnp.broadcast_to(gq["ws"].astype(BF16), (SUBLANES, L))
        n_ref[u:u + 1, :] = (twice(gq["decay"]) * n_ref[u:u + 1, :]
                             + jnp.dot(w8, k_u, preferred_element_type=F32)[0:1])

    chunk = (i, nc - 1 - i)

    @pl.when(i < nc // 2)
    def _park():
        for u, (d, h) in enumerate(units):
            hs_ref[chunk[d], h] = h_t[u]

    @pl.when(i >= nc // 2)
    def _finish():
        for d, (o_ref, y_ref) in enumerate(((of_ref, yhi_ref), (ob_ref, ylo_ref))):
            for h in range(H):
                hs = slice(h * dh, (h + 1) * dh)
                x = h_t[d * H + h] + hs_ref[chunk[d], h]
                mu = jnp.mean(x, axis=0, keepdims=True)
                xc = x - mu
                var = jnp.mean(xc * xc, axis=0, keepdims=True)
                hn = (xc * lax.rsqrt(var + LN_EPS)).T
                y_ref[:, hs] = (jax.nn.sigmoid(o_ref[:, hs].astype(F32)) * hn * gnw_ref[:, hs]).astype(BF16)


def _mlstm_call(k, qt, vt, gates, og, o_col, gnw):
    B, S, W = k.shape
    L = M_CHUNK
    nc = S // L
    assert nc % 2 == 0
    half = nc // 2
    fwd, bwd = (lambda i: i), (lambda i: nc - 1 - i)
    second_f, second_b = (lambda i: jnp.maximum(i, half)), (lambda i: jnp.minimum(nc - 1 - i, half - 1))
    rows = lambda w, ch, c=0: pl.BlockSpec((None, L, w), lambda b, i: (b, ch(i), c))
    cols = lambda ch: pl.BlockSpec((None, W, L), lambda b, i: (b, 0, ch(i)))
    return pl.pallas_call(
        _mlstm_kernel,
        grid=(B, nc),
        in_specs=[rows(W, fwd), rows(W, bwd), cols(fwd), cols(bwd), cols(fwd), cols(bwd),
                  rows(LANES, fwd), rows(LANES, bwd), rows(W, second_f, o_col), rows(W, second_b, o_col),
                  pl.BlockSpec((1, W), lambda b, i: (0, 0))],
        out_specs=[pl.BlockSpec((None, L, W), lambda b, i: (b, jnp.maximum(i - half, 0), 0)),
                   pl.BlockSpec((None, L, W), lambda b, i: (b, jnp.minimum(nc - 1 - i, half - 1), 0))],
        out_shape=[jax.ShapeDtypeStruct((B, S // 2, W), BF16)] * 2,
        scratch_shapes=[pltpu.VMEM((nc, M_HEADS, M_HEAD_DIM, L), F32),
                        pltpu.VMEM((2 * M_HEADS, M_HEAD_DIM, M_HEAD_DIM), F32),
                        pltpu.VMEM((2 * M_HEADS, M_HEAD_DIM), F32),
                        pltpu.VMEM((2 * M_HEADS, L), F32)],
        compiler_params=_params("arbitrary", "arbitrary"),
        name="mlstm",
    )(k, k, qt, qt, vt, vt, gates, gates, og, og, gnw)


ATT_UNROLL = 16


def _attn_kernel(q_ref, k_ref, kp_ref, kn_ref, v_ref, vp_ref, vn_ref, bm_ref, o_ref, lse_ref,
                 *scratch, dil):
    hb = ATT_HALF
    tq = q_ref.shape[1]
    nqb = tq // hb
    i = pl.program_id(1)
    ni = pl.num_programs(1)
    if nqb == 1:
        stage = scratch

        def make_window(main, before, after):
            return lambda r, r0, ls: jnp.concatenate([before[r, :, ls], main[r, :, ls], after[r, :, ls]], axis=0)
    else:
        pads, stage = scratch[:2], scratch[2:]

        def make_window(main, before, after, pad):
            pad[:, 0:hb, :] = before[...]
            pad[:, hb:hb + tq, :] = main[...]
            pad[:, hb + tq:, :] = after[...]
            return lambda r, r0, ls: pad[r, pl.ds(r0, 3 * hb), ls]
    k_window = make_window(k_ref, kp_ref, kn_ref, *(() if nqb == 1 else (pads[0],)))
    v_window = make_window(v_ref, vp_ref, vn_ref, *(() if nqb == 1 else (pads[1],)))
    even = lax.broadcasted_iota(jnp.int32, (hb, LANES), 1) < ATT_HEAD_DIM
    ones = jnp.ones((3 * hb, LANES), BF16)
    if dil > 1:
        os_ref, ls_ref = stage

    npair = ATT_SLOTS // 2
    total = dil * nqb
    step = min(ATT_UNROLL, total)

    def body(it, carry):
        where, scores = [], []
        for j in range(step):
            idx = it * step + j
            r, qb = idx // nqb, idx % nqb
            r0 = pl.multiple_of(qb * hb, hb)
            case = (jnp.logical_and(i == 0, qb == 0).astype(jnp.int32)
                    + 2 * jnp.logical_and(i == ni - 1, qb == nqb - 1).astype(jnp.int32))
            for p in range(npair):
                ls = slice(p * LANES, (p + 1) * LANES)
                q2 = q_ref[r, pl.ds(r0, hb), ls]
                qs = jnp.concatenate([jnp.where(even, q2, 0), jnp.where(even, 0, q2)], axis=0)
                kw = k_window(r, r0, ls)
                s = lax.dot_general(qs, kw, (((1,), (1,)), ((), ())), preferred_element_type=F32)
                scores.append(s + bm_ref[case, p])
                where.append((r, r0, p, ls))
        s = jnp.concatenate(scores, axis=0)
        m = jnp.max(s, axis=-1, keepdims=True)
        e = jnp.exp2(s - m).astype(BF16)
        outs = []
        for u, (r, r0, p, ls) in enumerate(where):
            vw = jnp.concatenate([v_window(r, r0, ls), ones], axis=1)
            outs.append(jnp.dot(e[u * 2 * hb:(u + 1) * 2 * hb], vw, preferred_element_type=F32))
        ol = jnp.concatenate(outs, axis=0)
        l = ol[:, LANES:]
        o = ol[:, :LANES] * (1.0 / l)
        lse = m + jnp.log2(l)
        for u, (r, r0, p, ls) in enumerate(where):
            top, bot = slice(u * 2 * hb, u * 2 * hb + hb), slice(u * 2 * hb + hb, (u + 1) * 2 * hb)
            o2 = jnp.where(even, o[top], o[bot])
            l2 = jnp.where(even, lse[top], lse[bot])
            if dil > 1:
                os_ref[r, pl.ds(r0, hb), ls] = o2
                ls_ref[r, pl.ds(r0, hb), ls] = l2
            else:
                o_ref[p, pl.ds(r0, hb), :] = o2
                lse_ref[p, pl.ds(r0, hb), :] = l2
        return carry

    lax.fori_loop(0, total // step, body, 0)
    if dil > 1:
        for r in range(dil):
            for p in range(ATT_SLOTS // 2):
                ls = slice(p * LANES, (p + 1) * LANES)
                o_ref[p, pl.ds(r, tq, stride=dil), :] = os_ref[r, :, ls]
                lse_ref[p, pl.ds(r, tq, stride=dil), :] = ls_ref[r, :, ls]


def _attn_bias(g):
    h = np.arange(1, ATT_HEADS + 1, dtype=np.float32)
    slopes = np.exp2(-8.0 * h / ATT_HEADS).astype(np.float32).reshape(N_PATTERNS, ATT_SLOTS)[g]
    dil = ATT_PATTERNS[g][1]
    hb = ATT_HALF
    qi = np.arange(hb)[:, None] + hb
    kj = np.arange(3 * hb)[None, :]
    dist = np.abs(qi - kj)
    out = np.empty((4, ATT_SLOTS // 2, 2 * hb, 3 * hb), np.float32)
    for case in range(4):
        valid = dist <= hb
        if case & 1:
            valid = valid & (kj >= hb)
        if case & 2:
            valid = valid & (kj < 2 * hb)
        for p in range(ATT_SLOTS // 2):
            for e in range(2):
                bias = -slopes[2 * p + e] * (dil * dist).astype(np.float32) * np.float32(LOG2E)
                out[case, p, e * hb:(e + 1) * hb] = np.where(valid, bias, np.float32(NEG))
    return jnp.asarray(out)


def _attn_call(zq, g, col0):
    B, dil, n, W = zq.shape
    S = n * dil
    hb = ATT_HALF
    tq = min(n, max(ROWS_PROJ // dil, hb))
    cw = ATT_OUT
    qc, kc, vc = (col0 // cw + t for t in range(3))
    rb, nrb = tq // hb, n // hb

    def main(c):
        return pl.BlockSpec((None, dil, tq, cw), lambda b, i: (b, 0, i, c))

    def prev(c):
        return pl.BlockSpec((None, dil, hb, cw), lambda b, i: (b, 0, jnp.maximum(i * rb - 1, 0), c))

    def nxt(c):
        return pl.BlockSpec((None, dil, hb, cw), lambda b, i: (b, 0, jnp.minimum((i + 1) * rb, nrb - 1), c))

    npair = ATT_SLOTS // 2
    out_spec = pl.BlockSpec((None, npair, tq * dil, LANES), lambda b, i: (b, 0, i, 0))
    scratch = [] if tq == hb else [pltpu.VMEM((dil, tq + 2 * hb, cw), BF16)] * 2
    if dil > 1:
        scratch += [pltpu.VMEM((dil, tq, cw), F32), pltpu.VMEM((dil, tq, cw), F32)]
    return pl.pallas_call(
        functools.partial(_attn_kernel, dil=dil),
        grid=(B, n // tq),
        in_specs=[main(qc), main(kc), prev(kc), nxt(kc), main(vc), prev(vc), nxt(vc),
                  pl.BlockSpec((4, ATT_SLOTS // 2, 2 * hb, 3 * hb), lambda b, i: (0, 0, 0, 0))],
        out_specs=[out_spec, out_spec],
        out_shape=[jax.ShapeDtypeStruct((B, npair, S, LANES), F32)] * 2,
        scratch_shapes=scratch,
        compiler_params=_params("arbitrary", "arbitrary"),
        name=f"attn{g}",
    )(zq, zq, zq, zq, zq, zq, zq, _attn_bias(g))


def _mix_kernel(yhi_ref, ylo_ref, o0_ref, o1_ref, o2_ref, l0_ref, l1_ref, l2_ref, ga_ref, gb_ref,
                h_ref, wa_ref, wb_ref, wo_ref, ada_ref, lng_ref, lnb_ref, hn_ref, u_ref):
    upper = pl.program_id(1) >= pl.num_programs(1) // 2
    ym = jnp.where(upper, yhi_ref[...], ylo_ref[...])
    pairs = []
    for p in range(ATT_SLOTS // 2):
        l0, l1, l2 = l0_ref[p], l1_ref[p], l2_ref[p]
        lm = jnp.maximum(jnp.maximum(l0, l1), l2)
        e0, e1, e2 = jnp.exp2(l0 - lm), jnp.exp2(l1 - lm), jnp.exp2(l2 - lm)
        pairs.append((e0 * o0_ref[p] + e1 * o1_ref[p] + e2 * o2_ref[p]) * (1.0 / (e0 + e1 + e2)))
    att = jnp.concatenate(pairs, axis=-1)
    y_a = jnp.dot(ym, wa_ref[...], preferred_element_type=F32)
    y_b = jnp.dot(att.astype(BF16), wb_ref[...], preferred_element_type=F32)
    mix = (jax.nn.sigmoid(ga_ref[...].astype(F32)) * y_a
           + jax.nn.sigmoid(gb_ref[...].astype(F32)) * y_b)
    y = jnp.dot(mix.astype(BF16), wo_ref[...], preferred_element_type=F32)
    hn = _layer_norm_rows(DN_ALPHA * h_ref[...] + ada_ref[2:3, :] * y) * lng_ref[...] + lnb_ref[...]
    hn_ref[...] = hn
    u_ref[...] = (hn * (1.0 + ada_ref[4:5, :]) + ada_ref[3:4, :]).astype(BF16)


def _mix_call(y_hi, y_lo, att, ga, gb, gb_col, h, wa, wb, wo, ada_l, lng, lnb, tm=ROWS_MIX):
    B, S, D = h.shape
    half = S // tm // 2
    row = lambda w, c=0: pl.BlockSpec((None, tm, w), lambda b, i: (b, i, c))
    full = lambda a: pl.BlockSpec(a.shape, lambda b, i: (0,) * a.ndim)
    (o0, l0), (o1, l1), (o2, l2) = att
    return pl.pallas_call(
        _mix_kernel,
        grid=(B, S // tm),
        in_specs=[pl.BlockSpec((None, tm, D), lambda b, i: (b, jnp.maximum(i - half, 0), 0)),
                  pl.BlockSpec((None, tm, D), lambda b, i: (b, jnp.minimum(i, half - 1), 0))]
                 + [pl.BlockSpec((None, ATT_SLOTS // 2, tm, LANES), lambda b, i: (b, 0, i, 0))] * 6
                 + [row(D), row(D, gb_col), row(D),
                  full(wa), full(wb), full(wo),
                  pl.BlockSpec((None, SUBLANES, D), lambda b, i: (b, 0, 0)), full(lng), full(lnb)],
        out_specs=[row(D), row(D)],
        out_shape=[jax.ShapeDtypeStruct((B, S, D), F32), jax.ShapeDtypeStruct((B, S, D), BF16)],
        compiler_params=_params("arbitrary", "arbitrary"),
        name="mix",
    )(y_hi, y_lo, o0, o1, o2, l0, l1, l2, ga, gb, h, wa, wb, wo, ada_l, lng, lnb)


def _ffn_kernel(u_ref, h_ref, w1_ref, w3_ref, w2_ref, ada_ref, lng_ref, lnb_ref, adan_ref,
                hn_ref, *rest, emit_u):
    acc_ref = rest[-1]
    u = u_ref[...]
    for f in range(w1_ref.shape[1] // FF_CHUNK):
        fs = slice(f * FF_CHUNK, (f + 1) * FF_CHUNK)
        a = jnp.dot(u, w1_ref[:, fs], preferred_element_type=F32)
        b = jnp.dot(u, w3_ref[:, fs], preferred_element_type=F32)
        t = (a * jax.nn.sigmoid(a) * b).astype(BF16)
        part = jnp.dot(t, w2_ref[fs, :], preferred_element_type=F32)
        if f == 0:
            acc_ref[...] = part
        else:
            acc_ref[...] += part
    hn = (_layer_norm_rows(DN_ALPHA * h_ref[...] + ada_ref[5:6, :] * acc_ref[...])
          * lng_ref[...] + lnb_ref[...])
    hn_ref[...] = hn
    if emit_u:
        rest[0][...] = (hn * (1.0 + adan_ref[1:2, :]) + adan_ref[0:1, :]).astype(BF16)


def _ffn_call(u, h, w1, w3, w2, ada_l, lng, lnb, ada_next, emit_u, tm=ROWS_FFN):
    B, S, D = h.shape
    row = pl.BlockSpec((None, tm, D), lambda b, i: (b, i, 0))
    full = lambda a: pl.BlockSpec(a.shape, lambda b, i: (0,) * a.ndim)
    ada_spec = pl.BlockSpec((None, SUBLANES, D), lambda b, i: (b, 0, 0))
    out_specs = [row, row] if emit_u else [row]
    out_shape = [jax.ShapeDtypeStruct((B, S, D), F32)]
    if emit_u:
        out_shape.append(jax.ShapeDtypeStruct((B, S, D), BF16))
    return pl.pallas_call(
        functools.partial(_ffn_kernel, emit_u=emit_u),
        grid=(B, S // tm),
        in_specs=[row, row, full(w1), full(w3), full(w2), ada_spec, full(lng), full(lnb), ada_spec],
        out_specs=out_specs, out_shape=out_shape,
        scratch_shapes=[pltpu.VMEM((tm, D), F32)],
        compiler_params=_params("arbitrary", "arbitrary"),
        name="ffn",
    )(u, h, w1, w3, w2, ada_l, lng, lnb, ada_next)


def kernel(x, c, w_in, b_gates, conv_w, conv_b, gn_w, w_a, w_b, w_out, w_ada, b_ada, ln1_g, ln1_b,
           w1, w3, w2, ln2_g, ln2_b):
    B, S, D = x.shape
    depth = w_in.shape[0]
    T = B * S
    H = M_HEADS

    c_pad = jnp.zeros((SUBLANES, D), F32).at[:B].set(c)
    ada = _ada_call(c_pad, w_ada, b_ada)
    ada = jnp.pad(ada[:, :, :B].transpose(0, 2, 1, 3), ((0, 0), (0, 0), (0, 2), (0, 0)))

    w_in_t = jnp.swapaxes(w_in, 1, 2)
    w_all = _prep_w_in(w_in_t)
    wvt_all = _prep_w_v_t(w_in_t, IN_SECTIONS[0], IN_SECTIONS[1])
    h, u = _ln0_call(x, ada[0])
    for l in range(depth):
        bg = jnp.pad(b_gates[l], (0, LANES - 4 * H)).reshape(1, LANES)

        u2d = u.reshape(T, D)
        vt = _proj_t_call(u, wvt_all, l, ROWS_STREAM, "proj_vT")
        zq = [_proj_dil_call(u, w_all, l, g, ROWS_PROJ, f"projb{g}") for g in range(N_PATTERNS)]
        gates = _gate_call(u2d, w_all, l, bg).reshape(B, S, LANES)

        cw8 = jnp.pad(conv_w[l], ((0, SUBLANES - CONV_WIDTH), (0, 0)))
        cb = conv_b[l].reshape(1, 2 * M_WIDTH)
        qt, ga = _proj_conv_call(u, w_all, l, W_QK, (W_GA, W_GA + D // 2), D // 2, cw8, cb,
                                 M_HEAD_DIM ** -0.5, True, "proj_conv_q")
        k, ogb = _proj_conv_call(u, w_all, l, W_QK + M_WIDTH, (W_O, W_GB), D, cw8, cb,
                                 1.0, False, "proj_conv_k")
        y_hi, y_lo = _mlstm_call(k, qt, vt, gates, ogb, 0, gn_w[l].reshape(1, M_WIDTH))

        att = [_attn_call(zq[g], g, 0) for g in range(N_PATTERNS)]
        h, u = _mix_call(y_hi, y_lo, att, ga, ogb, 1, h, w_a[l].astype(BF16), w_b[l].astype(BF16),
                         w_out[l].astype(BF16), ada[l], ln1_g[l].reshape(1, D), ln1_b[l].reshape(1, D))
        last = l == depth - 1
        outs = _ffn_call(u, h, w1[l].astype(BF16), w3[l].astype(BF16), w2[l].astype(BF16), ada[l],
                         ln2_g[l].reshape(1, D), ln2_b[l].reshape(1, D),
                         ada[l] if last else ada[l + 1], not last)
        h = outs[0]
        if not last:
            u = outs[1]
    return h
```

```python
import functools

import jax
import jax.numpy as jnp
import numpy as np
from jax import lax
from jax.experimental import pallas as pl
from jax.experimental.pallas import tpu as pltpu

F32 = jnp.float32
BF16 = jnp.bfloat16

D_MODEL = 1024
DEPTH = 2
M_HEADS = 4
M_HEAD_DIM = 256
M_WIDTH = M_HEADS * M_HEAD_DIM
M_CHUNK = 256
CONV_WIDTH = 5
ATT_SLOTS = 8
ATT_HEAD_DIM = 64
ATT_PATTERNS = ((128, 1), (512, 4), (2048, 16))
N_PATTERNS = 3
ATT_HEADS = ATT_SLOTS * N_PATTERNS
ATT_WIDTH = ATT_HEADS * ATT_HEAD_DIM
ATT_OUT = ATT_SLOTS * ATT_HEAD_DIM
ATT_HALF = 64
D_FF = -(-8 * D_MODEL // (3 * 256)) * 256
DN_ALPHA = (2 * DEPTH) ** 0.25
LN_EPS = 1e-5
NEG = -1e30
IN_SECTIONS = (2 * M_WIDTH, M_WIDTH, M_WIDTH, 4 * M_HEADS, 3 * ATT_WIDTH, 2 * D_MODEL)

VMEM_LIMIT_BYTES = 56 * 1024 * 1024
LANES = 128
SUBLANES = 8

LOG2E = 1.4426950408889634

ROWS_STREAM = 512
ROWS_PROJ = 1024
COLS_CONV = 512
ROWS_MIX = 512
ROWS_FFN = 512
FF_CHUNK = 256
CONV_ROWS = 128


def _params(*sem):
    return pltpu.CompilerParams(dimension_semantics=sem, vmem_limit_bytes=VMEM_LIMIT_BYTES)


def _layer_norm_rows(x):
    mu = jnp.mean(x, axis=-1, keepdims=True)
    xc = x - mu
    var = jnp.mean(xc * xc, axis=-1, keepdims=True)
    return xc * lax.rsqrt(var + LN_EPS)


def _ada_kernel(c_ref, w_ref, b_ref, o_ref):
    c = c_ref[...]
    ca = c * jax.nn.sigmoid(c)
    o_ref[...] = jnp.dot(ca, w_ref[...], precision=lax.Precision.HIGHEST,
                         preferred_element_type=F32) + b_ref[...]


def _ada_call(c_pad, w_ada, b_ada):
    depth = w_ada.shape[0]
    return pl.pallas_call(
        _ada_kernel,
        grid=(depth, 6),
        in_specs=[pl.BlockSpec((SUBLANES, D_MODEL), lambda l, k: (0, 0)),
                  pl.BlockSpec((None, D_MODEL, D_MODEL), lambda l, k: (l, 0, k)),
                  pl.BlockSpec((None, 1, D_MODEL), lambda l, k: (l, 0, k))],
        out_specs=pl.BlockSpec((None, None, SUBLANES, D_MODEL), lambda l, k: (l, k, 0, 0)),
        out_shape=jax.ShapeDtypeStruct((depth, 6, SUBLANES, D_MODEL), F32),
        compiler_params=_params("arbitrary", "arbitrary"),
        name="ada",
    )(c_pad, w_ada, b_ada.reshape(depth, 1, 6 * D_MODEL))


def _ln0_kernel(x_ref, ada_ref, h_ref, u_ref):
    y = _layer_norm_rows(x_ref[...])
    h_ref[...] = y
    u_ref[...] = (y * (1.0 + ada_ref[1:2, :]) + ada_ref[0:1, :]).astype(BF16)


def _ln0_call(x, ada0, ts=ROWS_STREAM):
    B, S, D = x.shape
    row = pl.BlockSpec((None, ts, D), lambda b, i: (b, i, 0))
    return pl.pallas_call(
        _ln0_kernel,
        grid=(B, S // ts),
        in_specs=[row, pl.BlockSpec((None, SUBLANES, D), lambda b, i: (b, 0, 0))],
        out_specs=[row, row],
        out_shape=[jax.ShapeDtypeStruct((B, S, D), F32), jax.ShapeDtypeStruct((B, S, D), BF16)],
        compiler_params=_params("arbitrary", "arbitrary"),
        name="ln0",
    )(x, ada0)


W_QK, W_O, W_GA, W_GB, W_ATT = 0, 2 * M_WIDTH, 3 * M_WIDTH, 3 * M_WIDTH + D_MODEL, 3 * M_WIDTH + 2 * D_MODEL
W_GATE = W_ATT + 3 * ATT_WIDTH
W_BLOCK = 512
W_ALL = W_GATE + W_BLOCK
PREP_ALIGN = 4 * M_HEADS


def _prep_plan():
    s0, s1, s2, s3, s4, s5 = np.cumsum((0,) + IN_SECTIONS)[:6].tolist()
    runs = [(W_QK, s0, 2 * M_WIDTH, 0), (W_O, s2, M_WIDTH, 0), (W_GA, s5, 2 * D_MODEL, 0),
            (W_ATT, s4, ATT_WIDTH, 1), (W_ATT + ATT_WIDTH, s4 + ATT_WIDTH, 2 * ATT_WIDTH, 0),
            (W_GATE, s3, 4 * M_HEADS, 0)]
    plan = np.zeros((3, W_ALL // W_BLOCK), np.int32)
    for dst, src, width, is_q in runs:
        for k in range(-(-width // W_BLOCK)):
            assert (src + k * W_BLOCK) % PREP_ALIGN == 0
            plan[:, dst // W_BLOCK + k] = ((src + k * W_BLOCK) // PREP_ALIGN, is_q,
                                           min(W_BLOCK, width - k * W_BLOCK))
    return plan


def _prep_kernel(src_ref, isq_ref, valid_ref, wt_ref, o_ref):
    ob = pl.program_id(1)
    scale = jnp.where(isq_ref[ob] == 1, ATT_HEAD_DIM ** -0.5 * LOG2E, 1.0)
    y = wt_ref[0].T * scale
    lane = lax.broadcasted_iota(jnp.int32, y.shape, 1)
    o_ref[...] = jnp.where(lane < valid_ref[ob], y, 0.0).astype(BF16)


def _prep_w_in(w_in_t):
    depth, _, D = w_in_t.shape
    plan = _prep_plan()
    return pl.pallas_call(
        _prep_kernel,
        grid_spec=pltpu.PrefetchScalarGridSpec(
            num_scalar_prefetch=3, grid=(depth, W_ALL // W_BLOCK),
            in_specs=[pl.BlockSpec((pl.Element(1), pl.Element(W_BLOCK), pl.Element(D)),
                                   lambda l, ob, src, *_: (l, src[ob] * PREP_ALIGN, 0))],
            out_specs=pl.BlockSpec((None, D, W_BLOCK), lambda l, ob, *_: (l, 0, ob))),
        out_shape=jax.ShapeDtypeStruct((depth, D, W_ALL), BF16),
        compiler_params=_params("arbitrary", "arbitrary"),
        name="prep_w_in",
    )(*(jnp.asarray(p) for p in plan), w_in_t)


def _w_spec(l, rows, width, block_index):
    return pl.BlockSpec((None, rows, width), lambda *g: (l, 0, block_index(*g)))


def _proj_t_kernel(wt_ref, u_ref, o_ref):
    o_ref[...] = lax.dot_general(wt_ref[...], u_ref[...], (((1,), (1,)), ((), ())),
                                 preferred_element_type=F32).astype(o_ref.dtype)


def _proj_t_call(u, wt_all, l, tm, name):
    B, S, D = u.shape
    N = wt_all.shape[1]
    return pl.pallas_call(
        _proj_t_kernel,
        grid=(B, S // tm),
        in_specs=[pl.BlockSpec((None, N, D), lambda b, i: (l, 0, 0)),
                  pl.BlockSpec((None, tm, D), lambda b, i: (b, i, 0))],
        out_specs=pl.BlockSpec((None, N, tm), lambda b, i: (b, 0, i)),
        out_shape=jax.ShapeDtypeStruct((B, N, S), BF16),
        compiler_params=_params("arbitrary", "arbitrary"),
        name=name,
    )(wt_all, u)


def _cast_kernel(w_ref, o_ref):
    o_ref[...] = w_ref[...].astype(o_ref.dtype)


def _prep_w_v_t(w_in_t, row0, rows, tr=ROWS_STREAM):
    depth, _, D = w_in_t.shape
    return pl.pallas_call(
        _cast_kernel,
        grid=(depth, rows // tr),
        in_specs=[pl.BlockSpec((None, tr, D), lambda l, j: (l, row0 // tr + j, 0))],
        out_specs=pl.BlockSpec((None, tr, D), lambda l, j: (l, j, 0)),
        out_shape=jax.ShapeDtypeStruct((depth, rows, D), BF16),
        compiler_params=_params("arbitrary", "arbitrary"),
        name="prep_w_vT",
    )(w_in_t)


def _proj_dil_kernel(u_ref, wq_ref, wk_ref, wv_ref, o_ref, *scratch, dil):
    u = u_ref[...]
    rows = u_ref.shape[0] // dil
    gw = 2 * LANES
    j = 0
    for w_ref in (wq_ref, wk_ref, wv_ref):
        for part in range(w_ref.shape[1] // gw):
            acc = jnp.dot(u, w_ref[:, part * gw:(part + 1) * gw], preferred_element_type=F32)
            if dil == 1:
                o_ref[0, :, j * gw:(j + 1) * gw] = acc.astype(o_ref.dtype)
            else:
                acc_ref = scratch[0]
                for c in range(gw // LANES):
                    acc_ref[j, c] = acc[:, c * LANES:(c + 1) * LANES]
                for r in range(dil):
                    for c in range(gw // LANES):
                        o_ref[r, :, j * gw + c * LANES:j * gw + (c + 1) * LANES] = (
                            acc_ref[j, c, pl.ds(r, rows, stride=dil), :].astype(o_ref.dtype))
            j += 1


def _proj_dil_call(u, w_all, l, g, tm, name):
    B, S, D = u.shape
    dil = ATT_PATTERNS[g][1]
    N = 3 * ATT_OUT
    w_specs = [_w_spec(l, D, ATT_OUT, lambda b, i, t=t: (W_ATT + t * ATT_WIDTH) // ATT_OUT + g) for t in range(3)]
    scratch = [] if dil == 1 else [pltpu.VMEM((N // (2 * LANES), 2, tm, LANES), F32)]
    return pl.pallas_call(
        functools.partial(_proj_dil_kernel, dil=dil),
        grid=(B, S // tm),
        in_specs=[pl.BlockSpec((None, tm, D), lambda b, i: (b, i, 0))] + w_specs,
        out_specs=pl.BlockSpec((None, dil, tm // dil, N), lambda b, i: (b, 0, i, 0)),
        out_shape=jax.ShapeDtypeStruct((B, dil, S // dil, N), BF16),
        scratch_shapes=scratch,
        compiler_params=_params("arbitrary", "arbitrary"),
        name=name,
    )(u, w_all, w_all, w_all)


def _gate_kernel(u_ref, w_ref, b_ref, o_ref):
    g = jnp.dot(u_ref[...], w_ref[...], preferred_element_type=F32) + b_ref[...]
    lf = jnp.minimum(g, 0.0) - jnp.log1p(jnp.exp(-jnp.abs(g)))
    row = lax.broadcasted_iota(jnp.int32, (M_CHUNK, LANES), 0)
    lane = lax.broadcasted_iota(jnp.int32, (M_CHUNK, LANES), 1)
    H = M_HEADS
    for c in range(u_ref.shape[0] // M_CHUNK):
        sl = slice(c * M_CHUNK, (c + 1) * M_CHUNK)
        gc, pre, suf = g[sl], lf[sl], lf[sl]
        d = 1
        while d < M_CHUNK:
            pre = pre + jnp.where(row >= d, pltpu.roll(pre, d, 0), 0.0)
            suf = suf + jnp.where(row < M_CHUNK - d, pltpu.roll(suf, M_CHUNK - d, 0), 0.0)
            d *= 2
        a_f = gc - pltpu.roll(pre, LANES - H, 1)
        a_b = gc - pltpu.roll(suf, LANES - H, 1)
        e_f = jnp.broadcast_to(pltpu.roll(pre, 3 * H, 1)[M_CHUNK - 1:M_CHUNK], pre.shape)
        e_b = jnp.broadcast_to(pltpu.roll(suf, 2 * H, 1)[0:1], suf.shape)
        o_ref[sl, :] = jnp.where(lane < H, a_f, jnp.where(lane < 2 * H, pre,
                                 jnp.where(lane < 3 * H, a_b, jnp.where(lane < 4 * H, suf,
                                           jnp.where(lane < 5 * H, e_f, e_b)))))


def _gate_call(u2d, w_all, l, bg, tm=ROWS_STREAM):
    T, D = u2d.shape
    return pl.pallas_call(
        _gate_kernel,
        grid=(T // tm,),
        in_specs=[pl.BlockSpec((tm, D), lambda i: (i, 0)),
                  _w_spec(l, D, LANES, lambda i: W_GATE // LANES),
                  pl.BlockSpec((1, LANES), lambda i: (0, 0))],
        out_specs=pl.BlockSpec((tm, LANES), lambda i: (i, 0)),
        out_shape=jax.ShapeDtypeStruct((T, LANES), F32),
        compiler_params=_params("arbitrary"),
        name="gates",
    )(u2d, w_all, bg)


def _proj_conv_kernel(u_ref, up_ref, un_ref, wc_ref, wo_ref, cw_ref, cb_ref, y_ref, z_ref, *,
                      scale, transpose_out):
    i = pl.program_id(1)
    tm = u_ref.shape[0]
    hr = up_ref.shape[0]
    before = jnp.where(i > 0, up_ref[...], jnp.zeros_like(up_ref))
    after = jnp.where(i < pl.num_programs(1) - 1, un_ref[...], jnp.zeros_like(un_ref))
    lhs = jnp.concatenate([before, u_ref[...], after], axis=0)
    x = jnp.dot(lhs, wc_ref[...], preferred_element_type=F32)
    z_ref[...] = jnp.dot(u_ref[...], wo_ref[...], preferred_element_type=F32).astype(z_ref.dtype)
    rt, ap = CONV_ROWS, SUBLANES
    for c in range(wc_ref.shape[1] // LANES):
        cs = slice(c * LANES, (c + 1) * LANES)
        for rb in range(tm // rt):
            r0 = hr + rb * rt
            blk = x[r0 - ap:r0 + rt + ap, cs]
            y = cb_ref[:, cs]
            for j in range(CONV_WIDTH):
                k = j - CONV_WIDTH // 2
                xs = blk if k == 0 else pltpu.roll(blk, (-k) % (rt + 2 * ap), 0)
                y = y + cw_ref[j:j + 1, cs] * xs[ap:ap + rt]
            y = y * jax.nn.sigmoid(y)
            if scale != 1.0:
                y = y * scale
            if transpose_out:
                y_ref[cs, rb * rt:(rb + 1) * rt] = y.T.astype(BF16)
            else:
                y_ref[rb * rt:(rb + 1) * rt, cs] = y.astype(BF16)


def _proj_conv_call(u, w_all, l, col0, wo_cols, to, conv_w8, conv_b, scale, transpose_out, name,
                    tm=ROWS_PROJ, tc=COLS_CONV):
    B, S, D = u.shape
    nj = M_WIDTH // tc
    assert len(wo_cols) == nj and all(c % to == 0 for c in wo_cols)
    wo_index = lambda j: sum(jnp.where(j == jj, c // to, 0) for jj, c in enumerate(wo_cols))
    hr = 2 * SUBLANES
    rb, nrb = tm // hr, S // hr
    c0 = col0 // tc
    if transpose_out:
        y_spec = pl.BlockSpec((None, tc, tm), lambda b, i, j: (b, j, i))
        y_shape = jax.ShapeDtypeStruct((B, M_WIDTH, S), BF16)
    else:
        y_spec = pl.BlockSpec((None, tm, tc), lambda b, i, j: (b, i, j))
        y_shape = jax.ShapeDtypeStruct((B, S, M_WIDTH), BF16)
    return pl.pallas_call(
        functools.partial(_proj_conv_kernel, scale=scale, transpose_out=transpose_out),
        grid=(B, S // tm, nj),
        in_specs=[pl.BlockSpec((None, tm, D), lambda b, i, j: (b, i, 0)),
                  pl.BlockSpec((None, hr, D), lambda b, i, j: (b, jnp.maximum(i * rb - 1, 0), 0)),
                  pl.BlockSpec((None, hr, D), lambda b, i, j: (b, jnp.minimum((i + 1) * rb, nrb - 1), 0)),
                  _w_spec(l, D, tc, lambda b, i, j: c0 + j),
                  _w_spec(l, D, to, lambda b, i, j: wo_index(j)),
                  pl.BlockSpec((SUBLANES, tc), lambda b, i, j: (0, c0 + j)),
                  pl.BlockSpec((1, tc), lambda b, i, j: (0, c0 + j))],
        out_specs=[y_spec, pl.BlockSpec((None, tm, to), lambda b, i, j: (b, i, j))],
        out_shape=[y_shape, jax.ShapeDtypeStruct((B, S, nj * to), BF16)],
        compiler_params=_params("arbitrary", "arbitrary", "arbitrary"),
        name=name,
    )(u, u, u, w_all, w_all, conv_w8, conv_b)


def _mlstm_kernel(kf_ref, kb_ref, qtf_ref, qtb_ref, vtf_ref, vtb_ref, gf_ref, gb_ref, of_ref, ob_ref,
                  gnw_ref, yhi_ref, ylo_ref, hs_ref, ct_ref, n_ref, m_ref):
    L, H, dh = M_CHUNK, M_HEADS, M_HEAD_DIM
    i = pl.program_id(1)
    nc = pl.num_programs(1)

    @pl.when(i == 0)
    def _init():
        ct_ref[...] = jnp.zeros_like(ct_ref)
        n_ref[...] = jnp.zeros_like(n_ref)
        m_ref[...] = jnp.full(m_ref.shape, NEG, F32)

    row = lax.broadcasted_iota(jnp.int32, (L, L), 0)
    col = lax.broadcasted_iota(jnp.int32, (L, L), 1)
    mask = (row <= col, row >= col)
    k_refs, qt_refs, vt_refs = (kf_ref, kb_ref), (qtf_ref, qtb_ref), (vtf_ref, vtb_ref)
    g = (gf_ref[...], gb_ref[...])
    gt = (g[0].T, g[1].T)
    units = [(d, h) for d in range(2) for h in range(H)]
    twice = lambda r: r if L == dh else jnp.concatenate([r] * (dh // L), axis=1)

    gate = []
    for u, (d, h) in enumerate(units):
        ja, jb, je = h + 2 * H * d, H + h + 2 * H * d, 4 * H + h + H * d
        a_col, a_row = g[d][:, ja:ja + 1], gt[d][ja:ja + 1, :]
        b_row, e_row = gt[d][jb:jb + 1, :], gt[d][je:je + 1, :]
        m_prev = m_ref[u:u + 1, :]
        log_d = jnp.where(mask[d], a_col + b_row, NEG)
        m_inter = b_row + m_prev
        m_t = jnp.maximum(jnp.max(log_d, axis=0, keepdims=True), m_inter)
        lw = e_row + a_row
        m_new = jnp.maximum(e_row + m_prev, jnp.max(lw, axis=-1, keepdims=True))
        m_ref[u:u + 1, :] = m_new
        gate.append(dict(dmat=jnp.exp(log_d - m_t), w_inter=jnp.exp(m_inter - m_t), emt=jnp.exp(-m_t),
                         ws=jnp.exp(lw - m_new), decay=jnp.exp(e_row + m_prev - m_new)))

    ops, pre = [], []
    for u, (d, h) in enumerate(units):
        hs = slice(h * dh, (h + 1) * dh)
        k_u, qt_u, vt_u = k_refs[d][:, hs], qt_refs[d][hs, :], vt_refs[d][hs, :]
        ops.append((k_u, qt_u, vt_u))
        st = jnp.dot(k_u, qt_u, preferred_element_type=F32)
        cq = jnp.dot(ct_ref[u].astype(BF16), qt_u, preferred_element_type=F32)
        n8 = jnp.broadcast_to(n_ref[u:u + 1, :].astype(BF16), (SUBLANES, dh))
        qn = jnp.dot(n8, qt_u, preferred_element_type=F32)[0:1]
        pre.append((st, cq, qn))

    h_t = []
    for u, (d, h) in enumerate(units):
        (k_u, qt_u, vt_u), (st, cq, qn), gq = ops[u], pre[u], gate[u]
        p = st * gq["dmat"]
        den = jnp.sum(p, axis=0, keepdims=True) + gq["w_inter"] * qn
        r = 1.0 / jnp.maximum(jnp.abs(den), gq["emt"])
        num = jnp.dot(vt_u, p.astype(BF16), preferred_element_type=F32) + gq["w_inter"] * cq
        h_t.append(num * r)

    for u, (d, h) in enumerate(units):
        (k_u, qt_u, vt_u), gq = ops[u], gate[u]
        vw = (vt_u.astype(F32) * gq["ws"]).astype(BF16)
        ct_ref[u] = twice(gq["decay"]) * ct_ref[u] + jnp.dot(vw, k_u, preferred_element_type=F32)
        w8 = jnp.broadcast_to(gq["ws"].astype(BF16), (SUBLANES, L))
        n_ref[u:u + 1, :] = (twice(gq["decay"]) * n_ref[u:u + 1, :]
                             + jnp.dot(w8, k_u, preferred_element_type=F32)[0:1])

    chunk = (i, nc - 1 - i)

    @pl.when(i < nc // 2)
    def _park():
        for u, (d, h) in enumerate(units):
            hs_ref[chunk[d], h] = h_t[u]

    @pl.when(i >= nc // 2)
    def _finish():
        for d, (o_ref, y_ref) in enumerate(((of_ref, yhi_ref), (ob_ref, ylo_ref))):
            for h in range(H):
                hs = slice(h * dh, (h + 1) * dh)
                x = h_t[d * H + h] + hs_ref[chunk[d], h]
                mu = jnp.mean(x, axis=0, keepdims=True)
                xc = x - mu
                var = jnp.mean(xc * xc, axis=0, keepdims=True)
                hn = (xc * lax.rsqrt(var + LN_EPS)).T
                y_ref[:, hs] = (jax.nn.sigmoid(o_ref[:, hs].astype(F32)) * hn * gnw_ref[:, hs]).astype(BF16)


def _mlstm_call(k, qt, vt, gates, og, o_col, gnw):
    B, S, W = k.shape
    L = M_CHUNK
    nc = S // L
    assert nc % 2 == 0
    half = nc // 2
    fwd, bwd = (lambda i: i), (lambda i: nc - 1 - i)
    second_f, second_b = (lambda i: jnp.maximum(i, half)), (lambda i: jnp.minimum(nc - 1 - i, half - 1))
    rows = lambda w, ch, c=0: pl.BlockSpec((None, L, w), lambda b, i: (b, ch(i), c))
    cols = lambda ch: pl.BlockSpec((None, W, L), lambda b, i: (b, 0, ch(i)))
    return pl.pallas_call(
        _mlstm_kernel,
        grid=(B, nc),
        in_specs=[rows(W, fwd), rows(W, bwd), cols(fwd), cols(bwd), cols(fwd), cols(bwd),
                  rows(LANES, fwd), rows(LANES, bwd), rows(W, second_f, o_col), rows(W, second_b, o_col),
                  pl.BlockSpec((1, W), lambda b, i: (0, 0))],
        out_specs=[pl.BlockSpec((None, L, W), lambda b, i: (b, jnp.maximum(i - half, 0), 0)),
                   pl.BlockSpec((None, L, W), lambda b, i: (b, jnp.minimum(nc - 1 - i, half - 1), 0))],
        out_shape=[jax.ShapeDtypeStruct((B, S // 2, W), BF16)] * 2,
        scratch_shapes=[pltpu.VMEM((nc, M_HEADS, M_HEAD_DIM, L), F32),
                        pltpu.VMEM((2 * M_HEADS, M_HEAD_DIM, M_HEAD_DIM), F32),
                        pltpu.VMEM((2 * M_HEADS, M_HEAD_DIM), F32),
                        pltpu.VMEM((2 * M_HEADS, L), F32)],
        compiler_params=_params("arbitrary", "arbitrary"),
        name="mlstm",
    )(k, k, qt, qt, vt, vt, gates, gates, og, og, gnw)


ATT_UNROLL = 16


def _attn_kernel(q_ref, k_ref, kp_ref, kn_ref, v_ref, vp_ref, vn_ref, bm_ref, o_ref, lse_ref,
                 *scratch, dil):
    hb = ATT_HALF
    tq = q_ref.shape[1]
    nqb = tq // hb
    i = pl.program_id(1)
    ni = pl.num_programs(1)
    if nqb == 1:
        stage = scratch

        def make_window(main, before, after):
            return lambda r, r0, ls: jnp.concatenate([before[r, :, ls], main[r, :, ls], after[r, :, ls]], axis=0)
    else:
        pads, stage = scratch[:2], scratch[2:]

        def make_window(main, before, after, pad):
            pad[:, 0:hb, :] = before[...]
            pad[:, hb:hb + tq, :] = main[...]
            pad[:, hb + tq:, :] = after[...]
            return lambda r, r0, ls: pad[r, pl.ds(r0, 3 * hb), ls]
    k_window = make_window(k_ref, kp_ref, kn_ref, *(() if nqb == 1 else (pads[0],)))
    v_window = make_window(v_ref, vp_ref, vn_ref, *(() if nqb == 1 else (pads[1],)))
    even = lax.broadcasted_iota(jnp.int32, (hb, LANES), 1) < ATT_HEAD_DIM
    ones = jnp.ones((3 * hb, LANES), BF16)
    if dil > 1:
        os_ref, ls_ref = stage

    npair = ATT_SLOTS // 2
    total = dil * nqb
    step = min(ATT_UNROLL, total)

    def body(it, carry):
        where, scores = [], []
        for j in range(step):
            idx = it * step + j
            r, qb = idx // nqb, idx % nqb
            r0 = pl.multiple_of(qb * hb, hb)
            case = (jnp.logical_and(i == 0, qb == 0).astype(jnp.int32)
                    + 2 * jnp.logical_and(i == ni - 1, qb == nqb - 1).astype(jnp.int32))
            for p in range(npair):
                ls = slice(p * LANES, (p + 1) * LANES)
                q2 = q_ref[r, pl.ds(r0, hb), ls]
                qs = jnp.concatenate([jnp.where(even, q2, 0), jnp.where(even, 0, q2)], axis=0)
                kw = k_window(r, r0, ls)
                s = lax.dot_general(qs, kw, (((1,), (1,)), ((), ())), preferred_element_type=F32)
                scores.append(s + bm_ref[case, p])
                where.append((r, r0, p, ls))
        s = jnp.concatenate(scores, axis=0)
        m = jnp.max(s, axis=-1, keepdims=True)
        e = jnp.exp2(s - m).astype(BF16)
        outs = []
        for u, (r, r0, p, ls) in enumerate(where):
            vw = jnp.concatenate([v_window(r, r0, ls), ones], axis=1)
            outs.append(jnp.dot(e[u * 2 * hb:(u + 1) * 2 * hb], vw, preferred_element_type=F32))
        ol = jnp.concatenate(outs, axis=0)
        l = ol[:, LANES:]
        o = ol[:, :LANES] * (1.0 / l)
        lse = m + jnp.log2(l)
        for u, (r, r0, p, ls) in enumerate(where):
            top, bot = slice(u * 2 * hb, u * 2 * hb + hb), slice(u * 2 * hb + hb, (u + 1) * 2 * hb)
            o2 = jnp.where(even, o[top], o[bot])
            l2 = jnp.where(even, lse[top], lse[bot])
            if dil > 1:
                os_ref[r, pl.ds(r0, hb), ls] = o2
                ls_ref[r, pl.ds(r0, hb), ls] = l2
            else:
                o_ref[p, pl.ds(r0, hb), :] = o2
                lse_ref[p, pl.ds(r0, hb), :] = l2
        return carry

    lax.fori_loop(0, total // step, body, 0)
    if dil > 1:
        for r in range(dil):
            for p in range(ATT_SLOTS // 2):
                ls = slice(p * LANES, (p + 1) * LANES)
                o_ref[p, pl.ds(r, tq, stride=dil), :] = os_ref[r, :, ls]
                lse_ref[p, pl.ds(r, tq, stride=dil), :] = ls_ref[r, :, ls]


def _attn_bias(g):
    h = np.arange(1, ATT_HEADS + 1, dtype=np.float32)
    slopes = np.exp2(-8.0 * h / ATT_HEADS).astype(np.float32).reshape(N_PATTERNS, ATT_SLOTS)[g]
    dil = ATT_PATTERNS[g][1]
    hb = ATT_HALF
    qi = np.arange(hb)[:, None] + hb
    kj = np.arange(3 * hb)[None, :]
    dist = np.abs(qi - kj)
    out = np.empty((4, ATT_SLOTS // 2, 2 * hb, 3 * hb), np.float32)
    for case in range(4):
        valid = dist <= hb
        if case & 1:
            valid = valid & (kj >= hb)
        if case & 2:
            valid = valid & (kj < 2 * hb)
        for p in range(ATT_SLOTS // 2):
            for e in range(2):
                bias = -slopes[2 * p + e] * (dil * dist).astype(np.float32) * np.float32(LOG2E)
                out[case, p, e * hb:(e + 1) * hb] = np.where(valid, bias, np.float32(NEG))
    return jnp.asarray(out)


def _attn_call(zq, g, col0):
    B, dil, n, W = zq.shape
    S = n * dil
    hb = ATT_HALF
    tq = min(n, max(ROWS_PROJ // dil, hb))
    cw = ATT_OUT
    qc, kc, vc = (col0 // cw + t for t in range(3))
    rb, nrb = tq // hb, n // hb

    def main(c):
        return pl.BlockSpec((None, dil, tq, cw), lambda b, i: (b, 0, i, c))

    def prev(c):
        return pl.BlockSpec((None, dil, hb, cw), lambda b, i: (b, 0, jnp.maximum(i * rb - 1, 0), c))

    def nxt(c):
        return pl.BlockSpec((None, dil, hb, cw), lambda b, i: (b, 0, jnp.minimum((i + 1) * rb, nrb - 1), c))

    npair = ATT_SLOTS // 2
    out_spec = pl.BlockSpec((None, npair, tq * dil, LANES), lambda b, i: (b, 0, i, 0))
    scratch = [] if tq == hb else [pltpu.VMEM((dil, tq + 2 * hb, cw), BF16)] * 2
    if dil > 1:
        scratch += [pltpu.VMEM((dil, tq, cw), F32), pltpu.VMEM((dil, tq, cw), F32)]
    return pl.pallas_call(
        functools.partial(_attn_kernel, dil=dil),
        grid=(B, n // tq),
        in_specs=[main(qc), main(kc), prev(kc), nxt(kc), main(vc), prev(vc), nxt(vc),
                  pl.BlockSpec((4, ATT_SLOTS // 2, 2 * hb, 3 * hb), lambda b, i: (0, 0, 0, 0))],
        out_specs=[out_spec, out_spec],
        out_shape=[jax.ShapeDtypeStruct((B, npair, S, LANES), F32)] * 2,
        scratch_shapes=scratch,
        compiler_params=_params("arbitrary", "arbitrary"),
        name=f"attn{g}",
    )(zq, zq, zq, zq, zq, zq, zq, _attn_bias(g))


def _mix_kernel(yhi_ref, ylo_ref, o0_ref, o1_ref, o2_ref, l0_ref, l1_ref, l2_ref, ga_ref, gb_ref,
                h_ref, wa_ref, wb_ref, wo_ref, ada_ref, lng_ref, lnb_ref, hn_ref, u_ref):
    upper = pl.program_id(1) >= pl.num_programs(1) // 2
    ym = jnp.where(upper, yhi_ref[...], ylo_ref[...])
    pairs = []
    for p in range(ATT_SLOTS // 2):
        l0, l1, l2 = l0_ref[p], l1_ref[p], l2_ref[p]
        lm = jnp.maximum(jnp.maximum(l0, l1), l2)
        e0, e1, e2 = jnp.exp2(l0 - lm), jnp.exp2(l1 - lm), jnp.exp2(l2 - lm)
        pairs.append((e0 * o0_ref[p] + e1 * o1_ref[p] + e2 * o2_ref[p]) * (1.0 / (e0 + e1 + e2)))
    att = jnp.concatenate(pairs, axis=-1)
    y_a = jnp.dot(ym, wa_ref[...], preferred_element_type=F32)
    y_b = jnp.dot(att.astype(BF16), wb_ref[...], preferred_element_type=F32)
    mix = (jax.nn.sigmoid(ga_ref[...].astype(F32)) * y_a
           + jax.nn.sigmoid(gb_ref[...].astype(F32)) * y_b)
    y = jnp.dot(mix.astype(BF16), wo_ref[...], preferred_element_type=F32)
    hn = _layer_norm_rows(DN_ALPHA * h_ref[...] + ada_ref[2:3, :] * y) * lng_ref[...] + lnb_ref[...]
    hn_ref[...] = hn
    u_ref[...] = (hn * (1.0 + ada_ref[4:5, :]) + ada_ref[3:4, :]).astype(BF16)


def _mix_call(y_hi, y_lo, att, ga, gb, gb_col, h, wa, wb, wo, ada_l, lng, lnb, tm=ROWS_MIX):
    B, S, D = h.shape
    half = S // tm // 2
    row = lambda w, c=0: pl.BlockSpec((None, tm, w), lambda b, i: (b, i, c))
    full = lambda a: pl.BlockSpec(a.shape, lambda b, i: (0,) * a.ndim)
    (o0, l0), (o1, l1), (o2, l2) = att
    return pl.pallas_call(
        _mix_kernel,
        grid=(B, S // tm),
        in_specs=[pl.BlockSpec((None, tm, D), lambda b, i: (b, jnp.maximum(i - half, 0), 0)),
                  pl.BlockSpec((None, tm, D), lambda b, i: (b, jnp.minimum(i, half - 1), 0))]
                 + [pl.BlockSpec((None, ATT_SLOTS // 2, tm, LANES), lambda b, i: (b, 0, i, 0))] * 6
                 + [row(D), row(D, gb_col), row(D),
                  full(wa), full(wb), full(wo),
                  pl.BlockSpec((None, SUBLANES, D), lambda b, i: (b, 0, 0)), full(lng), full(lnb)],
        out_specs=[row(D), row(D)],
        out_shape=[jax.ShapeDtypeStruct((B, S, D), F32), jax.ShapeDtypeStruct((B, S, D), BF16)],
        compiler_params=_params("arbitrary", "arbitrary"),
        name="mix",
    )(y_hi, y_lo, o0, o1, o2, l0, l1, l2, ga, gb, h, wa, wb, wo, ada_l, lng, lnb)


def _ffn_kernel(u_ref, h_ref, w1_ref, w3_ref, w2_ref, ada_ref, lng_ref, lnb_ref, adan_ref,
                hn_ref, *rest, emit_u):
    acc_ref = rest[-1]
    u = u_ref[...]
    for f in range(w1_ref.shape[1] // FF_CHUNK):
        fs = slice(f * FF_CHUNK, (f + 1) * FF_CHUNK)
        a = jnp.dot(u, w1_ref[:, fs], preferred_element_type=F32)
        b = jnp.dot(u, w3_ref[:, fs], preferred_element_type=F32)
        t = (a * jax.nn.sigmoid(a) * b).astype(BF16)
        part = jnp.dot(t, w2_ref[fs, :], preferred_element_type=F32)
        if f == 0:
            acc_ref[...] = part
        else:
            acc_ref[...] += part
    hn = (_layer_norm_rows(DN_ALPHA * h_ref[...] + ada_ref[5:6, :] * acc_ref[...])
          * lng_ref[...] + lnb_ref[...])
    hn_ref[...] = hn
    if emit_u:
        rest[0][...] = (hn * (1.0 + adan_ref[1:2, :]) + adan_ref[0:1, :]).astype(BF16)


def _ffn_call(u, h, w1, w3, w2, ada_l, lng, lnb, ada_next, emit_u, tm=ROWS_FFN):
    B, S, D = h.shape
    row = pl.BlockSpec((None, tm, D), lambda b, i: (b, i, 0))
    full = lambda a: pl.BlockSpec(a.shape, lambda b, i: (0,) * a.ndim)
    ada_spec = pl.BlockSpec((None, SUBLANES, D), lambda b, i: (b, 0, 0))
    out_specs = [row, row] if emit_u else [row]
    out_shape = [jax.ShapeDtypeStruct((B, S, D), F32)]
    if emit_u:
        out_shape.append(jax.ShapeDtypeStruct((B, S, D), BF16))
    return pl.pallas_call(
        functools.partial(_ffn_kernel, emit_u=emit_u),
        grid=(B, S // tm),
        in_specs=[row, row, full(w1), full(w3), full(w2), ada_spec, full(lng), full(lnb), ada_spec],
        out_specs=out_specs, out_shape=out_shape,
        scratch_shapes=[pltpu.VMEM((tm, D), F32)],
        compiler_params=_params("arbitrary", "arbitrary"),
        name="ffn",
    )(u, h, w1, w3, w2, ada_l, lng, lnb, ada_next)


def kernel(x, c, w_in, b_gates, conv_w, conv_b, gn_w, w_a, w_b, w_out, w_ada, b_ada, ln1_g, ln1_b,
           w1, w3, w2, ln2_g, ln2_b):
    B, S, D = x.shape
    depth = w_in.shape[0]
    T = B * S
    H = M_HEADS

    c_pad = jnp.zeros((SUBLANES, D), F32).at[:B].set(c)
    ada = _ada_call(c_pad, w_ada, b_ada)
    ada = jnp.pad(ada[:, :, :B].transpose(0, 2, 1, 3), ((0, 0), (0, 0), (0, 2), (0, 0)))

    w_in_t = jnp.swapaxes(w_in, 1, 2)
    w_all = _prep_w_in(w_in_t)
    wvt_all = _prep_w_v_t(w_in_t, IN_SECTIONS[0], IN_SECTIONS[1])
    h, u = _ln0_call(x, ada[0])
    for l in range(depth):
        bg = jnp.pad(b_gates[l], (0, LANES - 4 * H)).reshape(1, LANES)

        u2d = u.reshape(T, D)
        vt = _proj_t_call(u, wvt_all, l, ROWS_STREAM, "proj_vT")
        zq = [_proj_dil_call(u, w_all, l, g, ROWS_PROJ, f"projb{g}") for g in range(N_PATTERNS)]
        gates = _gate_call(u2d, w_all, l, bg).reshape(B, S, LANES)

        cw8 = jnp.pad(conv_w[l], ((0, SUBLANES - CONV_WIDTH), (0, 0)))
        cb = conv_b[l].reshape(1, 2 * M_WIDTH)
        qt, ga = _proj_conv_call(u, w_all, l, W_QK, (W_GA, W_GA + D // 2), D // 2, cw8, cb,
                                 M_HEAD_DIM ** -0.5, True, "proj_conv_q")
        k, ogb = _proj_conv_call(u, w_all, l, W_QK + M_WIDTH, (W_O, W_GB), D, cw8, cb,
                                 1.0, False, "proj_conv_k")
        y_hi, y_lo = _mlstm_call(k, qt, vt, gates, ogb, 0, gn_w[l].reshape(1, M_WIDTH))

        att = [_attn_call(zq[g], g, 0) for g in range(N_PATTERNS)]
        h, u = _mix_call(y_hi, y_lo, att, ga, ogb, 1, h, w_a[l].astype(BF16), w_b[l].astype(BF16),
                         w_out[l].astype(BF16), ada[l], ln1_g[l].reshape(1, D), ln1_b[l].reshape(1, D))
        last = l == depth - 1
        outs = _ffn_call(u, h, w1[l].astype(BF16), w3[l].astype(BF16), w2[l].astype(BF16), ada[l],
                         ln2_g[l].reshape(1, D), ln2_b[l].reshape(1, D),
                         ada[l] if last else ada[l + 1], not last)
        h = outs[0]
        if not last:
            u = outs[1]
    return h
```
